```python
import jax, jax.numpy as jnp
from jax import lax
import numpy as np

D_MODEL = 1024
BATCH = 2
SEQ = 8192
DEPTH = 4

GRID_W = 64
CTX_LEN = 256
N_MIXERS = 3
EPS = 1e-6
POOL_WINDOWS = (2, 4, 8, 16)
POOL_GROUPS = 4
POOL_GW = D_MODEL // POOL_GROUPS
HEAD_DIM = 128
N_HEADS = D_MODEL // HEAD_DIM
N_KV_HEADS = N_HEADS // 2
QKV_WIDTH = (N_HEADS + 2 * N_KV_HEADS) * HEAD_DIM
ROPE_BASE = 10000.0
Q_BLOCK = 128
CHUNK = 128
GMLP_HALF = 2 * D_MODEL
GMLP_GROUPS = 8
GMLP_GW = GMLP_HALF // GMLP_GROUPS
D_FF = 4 * D_MODEL

kernel_name = "hybrid_pool_gqa_gmlp_dit_block"


def n_layers_of(kind):
    return len(range(kind, DEPTH, N_MIXERS))


def rms_norm(x, g):
    xf = x.astype(jnp.float32)
    y = xf * lax.rsqrt(jnp.mean(xf * xf, axis=-1, keepdims=True) + EPS)
    return (y * g.astype(jnp.float32)).astype(x.dtype)


def layer_norm(x, g, b):
    xf = x.astype(jnp.float32)
    mu = jnp.mean(xf, axis=-1, keepdims=True)
    xc = xf - mu
    y = xc * lax.rsqrt(jnp.mean(xc * xc, axis=-1, keepdims=True) + EPS)
    return (y * g.astype(jnp.float32) + b.astype(jnp.float32)).astype(x.dtype)


def modulate(h, shift, scale):
    return h * (1 + scale[:, None, :]) + shift[:, None, :]


def pool_mix(h, w, scale):
    B, L, D = h.shape
    hf = h.astype(jnp.float32)
    cs = jnp.concatenate([jnp.zeros((B, 1, D), jnp.float32), jnp.cumsum(hf, axis=1)], axis=1)
    csg = cs.reshape(B, L + 1, POOL_GROUPS, POOL_GW)
    hg = hf.reshape(B, L, POOL_GROUPS, POOL_GW)
    pos = jnp.arange(L)
    outs = []
    for g, win in enumerate(POOL_WINDOWS):
        lo = jnp.clip(pos - win // 2, 0, L)
        hi = jnp.clip(pos + win - win // 2, 0, L)
        s = jnp.take(csg[:, :, g], hi, axis=1) - jnp.take(csg[:, :, g], lo, axis=1)
        cnt = (hi - lo).astype(jnp.float32)[None, :, None]
        outs.append(s / cnt - hg[:, :, g])
    p = jnp.stack(outs, axis=2).astype(h.dtype)
    y = jnp.einsum("blgc,gcd->blgd", p, w).reshape(B, L, D)
    return y * scale


def axial_rope_tables(L):
    rows_n = L // GRID_W
    row = jnp.repeat(jnp.arange(rows_n), GRID_W).astype(jnp.float32)
    col = jnp.tile(jnp.arange(GRID_W), rows_n).astype(jnp.float32)
    half = HEAD_DIM // 2
    inv = ROPE_BASE ** (-jnp.arange(0, half, 2, dtype=jnp.float32) / half)
    ang_r = row[:, None] * inv[None, :]
    ang_c = col[:, None] * inv[None, :]
    return jnp.cos(ang_r), jnp.sin(ang_r), jnp.cos(ang_c), jnp.sin(ang_c)


def rotate(x, cos, sin):
    x1, x2 = jnp.split(x, 2, axis=-1)
    cos = cos[None, :, None, :]
    sin = sin[None, :, None, :]
    return jnp.concatenate([x1 * cos - x2 * sin, x1 * sin + x2 * cos], axis=-1)


def apply_axial_rope(x, tables):
    cr, sr, cc, scol = tables
    xf = x.astype(jnp.float32)
    xr, xc = jnp.split(xf, 2, axis=-1)
    return jnp.concatenate([rotate(xr, cr, sr), rotate(xc, cc, scol)], axis=-1).astype(x.dtype)


def gqa_core(q, k, v):
    B, Lq = q.shape[0], q.shape[1]
    G = N_HEADS // N_KV_HEADS
    qg = q.reshape(B, Lq, N_KV_HEADS, G, HEAD_DIM)
    s = jnp.einsum("bqkgd,bskd->bkgqs", qg, k).astype(jnp.float32) * (HEAD_DIM ** -0.5)
    p = jax.nn.softmax(s, axis=-1).astype(v.dtype)
    o = jnp.einsum("bkgqs,bskd->bqkgd", p, v)
    return o.reshape(B, Lq, N_HEADS * HEAD_DIM)


def qkv_proj(h, w_qkv, q_g, k_g):
    B, L, _ = h.shape
    qkv = h @ w_qkv
    q, k, v = jnp.split(qkv, [N_HEADS * HEAD_DIM, (N_HEADS + N_KV_HEADS) * HEAD_DIM], axis=-1)
    q = rms_norm(q.reshape(B, L, N_HEADS, HEAD_DIM), q_g)
    k = rms_norm(k.reshape(B, L, N_KV_HEADS, HEAD_DIM), k_g)
    v = v.reshape(B, L, N_KV_HEADS, HEAD_DIM)
    return q, k, v


def attn_mix(h_ctx, h_lat, w_qkv, w_o, q_g, k_g, ctx_out):
    B, S, _ = h_lat.shape
    qc, kc, vc = qkv_proj(h_ctx, w_qkv, q_g, k_g)
    ql, kl, vl = qkv_proj(h_lat, w_qkv, q_g, k_g)
    tables = axial_rope_tables(S)
    ql = apply_axial_rope(ql, tables)
    kl = apply_axial_rope(kl, tables)
    k_all = jnp.concatenate([kc, kl], axis=1)
    v_all = jnp.concatenate([vc, vl], axis=1)
    nb = S // Q_BLOCK
    qb = ql.reshape(B, nb, Q_BLOCK, N_HEADS, HEAD_DIM).transpose(1, 0, 2, 3, 4)
    ob = lax.map(lambda qq: gqa_core(qq, k_all, v_all), qb)
    y_lat = ob.transpose(1, 0, 2, 3).reshape(B, S, N_HEADS * HEAD_DIM) @ w_o
    y_ctx = gqa_core(qc, kc, vc) @ w_o if ctx_out else None
    return y_ctx, y_lat


def gmlp_mix(h, w_in, ln_g, ln_b, ws, bs, w_out):
    B, L, _ = h.shape
    z = jax.nn.gelu(h @ w_in)
    u, v = jnp.split(z, 2, axis=-1)
    v = layer_norm(v, ln_g, ln_b)
    vg = v.reshape(B, L // CHUNK, CHUNK, GMLP_GROUPS, GMLP_GW)
    sv = jnp.einsum("gqp,bnpgc->bnqgc", ws, vg) + bs.T[None, None, :, :, None]
    return (u * sv.reshape(B, L, GMLP_HALF)) @ w_out


def sq_relu_mlp(h, w1, w2):
    return jnp.square(jax.nn.relu(h @ w1)) @ w2


def setup_inputs(seed: int = 0) -> dict:
    key = jax.random.key(seed)
    ks = jax.random.split(key, 24)
    f32 = jnp.float32
    D = D_MODEL
    nP, nA, nG = n_layers_of(0), n_layers_of(1), n_layers_of(2)

    def nrm(k, shape, s):
        return jax.random.normal(k, shape, f32) * s

    return {
        "x": nrm(ks[0], (BATCH, SEQ, D), 1.0),
        "c": nrm(ks[1], (BATCH, D), 1.0),
        "ctx": nrm(ks[2], (BATCH, CTX_LEN, D), 1.0),
        "c_ctx": nrm(ks[3], (D,), 1.0),
        "ada_w": nrm(ks[4], (DEPTH, D, 6 * D), 0.5 * D ** -0.5),
        "ada_b": nrm(ks[5], (DEPTH, 6 * D), 0.02),
        "norm_g": 1.0 + nrm(ks[6], (DEPTH, 2, D), 0.02),
        "mlp_w1": nrm(ks[7], (DEPTH, D, D_FF), D ** -0.5),
        "mlp_w2": nrm(ks[8], (DEPTH, D_FF, D), D_FF ** -0.5),
        "pool_w": nrm(ks[9], (nP, POOL_GROUPS, POOL_GW, POOL_GW), POOL_GW ** -0.5),
        "pool_scale": 1.0 + nrm(ks[10], (nP, D), 0.02),
        "attn_w_qkv": nrm(ks[11], (nA, D, QKV_WIDTH), D ** -0.5),
        "attn_w_o": nrm(ks[12], (nA, N_HEADS * HEAD_DIM, D), (N_HEADS * HEAD_DIM) ** -0.5),
        "attn_q_g": 1.0 + nrm(ks[13], (nA, HEAD_DIM), 0.02),
        "attn_k_g": 1.0 + nrm(ks[14], (nA, HEAD_DIM), 0.02),
        "gm_w_in": nrm(ks[15], (nG, D, 2 * GMLP_HALF), D ** -0.5),
        "gm_ln_g": 1.0 + nrm(ks[16], (nG, GMLP_HALF), 0.02),
        "gm_ln_b": nrm(ks[17], (nG, GMLP_HALF), 0.02),
        "gm_ws": nrm(ks[18], (nG, GMLP_GROUPS, CHUNK, CHUNK), CHUNK ** -0.5),
        "gm_bs": nrm(ks[19], (nG, GMLP_GROUPS, CHUNK), 0.02),
        "gm_w_out": nrm(ks[20], (nG, GMLP_HALF, D), GMLP_HALF ** -0.5),
        "final_g": 1.0 + nrm(ks[21], (D,), 0.02),
    }


def reference(x, c, ctx, c_ctx, ada_w, ada_b, norm_g, mlp_w1, mlp_w2, pool_w, pool_scale,
              attn_w_qkv, attn_w_o, attn_q_g, attn_k_g, gm_w_in, gm_ln_g, gm_ln_b, gm_ws, gm_bs,
              gm_w_out, final_g):
    last_ctx_read = max([i for i in range(DEPTH) if i % N_MIXERS == 1], default=-1)
    s_lat = jax.nn.silu(c)
    s_ctx = jax.nn.silu(c_ctx)[None, :]
    h_lat, h_ctx = x, ctx
    for i in range(DEPTH):
        kind, j = i % N_MIXERS, i // N_MIXERS
        ctx_in = i <= last_ctx_read
        ctx_out = i < last_ctx_read
        sh1, sc1, g1, sh2, sc2, g2 = jnp.split(s_lat @ ada_w[i] + ada_b[i], 6, axis=-1)
        a_l = modulate(rms_norm(h_lat, norm_g[i, 0]), sh1, sc1)
        if ctx_in:
            csh1, csc1, cg1, csh2, csc2, cg2 = jnp.split(s_ctx @ ada_w[i] + ada_b[i], 6, axis=-1)
            a_c = modulate(rms_norm(h_ctx, norm_g[i, 0]), csh1, csc1)
        y_c = None
        if kind == 0:
            y_l = pool_mix(a_l, pool_w[j], pool_scale[j])
            if ctx_out:
                y_c = pool_mix(a_c, pool_w[j], pool_scale[j])
        elif kind == 1:
            y_c, y_l = attn_mix(a_c, a_l, attn_w_qkv[j], attn_w_o[j], attn_q_g[j], attn_k_g[j], ctx_out)
        else:
            y_l = gmlp_mix(a_l, gm_w_in[j], gm_ln_g[j], gm_ln_b[j], gm_ws[j], gm_bs[j], gm_w_out[j])
            if ctx_out:
                y_c = gmlp_mix(a_c, gm_w_in[j], gm_ln_g[j], gm_ln_b[j], gm_ws[j], gm_bs[j], gm_w_out[j])
        h_lat = h_lat + g1[:, None, :] * y_l
        m_l = modulate(rms_norm(h_lat, norm_g[i, 1]), sh2, sc2)
        h_lat = h_lat + g2[:, None, :] * sq_relu_mlp(m_l, mlp_w1[i], mlp_w2[i])
        if ctx_out:
            h_ctx = h_ctx + cg1[:, None, :] * y_c
            m_c = modulate(rms_norm(h_ctx, norm_g[i, 1]), csh2, csc2)
            h_ctx = h_ctx + cg2[:, None, :] * sq_relu_mlp(m_c, mlp_w1[i], mlp_w2[i])
    return rms_norm(h_lat, final_g)
```

```python
import functools

import jax
import jax.numpy as jnp
import numpy as np
from jax.experimental import pallas as pl
from jax.experimental.pallas import tpu as pltpu

D_MODEL = 1024
DEPTH = 4
N_MIXERS = 3
GRID_W = 64
EPS = 1e-6
POOL_WINDOWS = (2, 4, 8, 16)
POOL_GW = D_MODEL // len(POOL_WINDOWS)
POOL_HALO = 16
HEAD_DIM = 128
N_HEADS = D_MODEL // HEAD_DIM
N_KV_HEADS = N_HEADS // 2
Q_PER_KV = N_HEADS // N_KV_HEADS
Q_WIDTH = N_HEADS * HEAD_DIM
KV_WIDTH = N_KV_HEADS * HEAD_DIM
ROPE_BASE = 10000.0
CHUNK = 128
GMLP_HALF = 2 * D_MODEL
GMLP_GROUPS = 8
GMLP_GW = GMLP_HALF // GMLP_GROUPS
D_FF = 4 * D_MODEL
LOG2E = 1.4426950408889634

VMEM_LIMIT_BYTES = 56 * 1024 * 1024
BF16 = jnp.bfloat16
F32 = jnp.float32


def _params(*semantics):
    return pltpu.CompilerParams(dimension_semantics=semantics, vmem_limit_bytes=VMEM_LIMIT_BYTES)


def _resident(shape):
    zeros = (0,) * len(shape)
    return pl.BlockSpec(shape, lambda *_: zeros, pipeline_mode=pl.Buffered(1))


def _mod_spec(mod):
    if mod.shape[0] == 1:
        return pl.BlockSpec((1, 6, D_MODEL), lambda b, i: (0, 0, 0))
    return pl.BlockSpec((1, 6, D_MODEL), lambda b, i: (b, 0, 0))


def _rms(x, g):
    return x * jax.lax.rsqrt(jnp.mean(x * x, axis=-1, keepdims=True) + EPS) * g


def _norm_mod(x, g, shift, scale):
    return _rms(x, g) * (1.0 + scale) + shift


def _dot(a, b):
    return jnp.dot(a, b, preferred_element_type=F32)


def _ada_kernel(s_ref, w_ref, b_ref, o_ref):
    s = s_ref[...]
    s = s * jax.nn.sigmoid(s)
    o_ref[0] = _dot(s.astype(BF16), w_ref[0].astype(BF16)) + b_ref[0]


def _ada_call(s_in, ada_w, ada_b):
    rows = s_in.shape[0]
    tn = 1536
    return pl.pallas_call(
        _ada_kernel,
        grid=(DEPTH, 6 * D_MODEL // tn),
        in_specs=[
            pl.BlockSpec((rows, D_MODEL), lambda l, j: (0, 0)),
            pl.BlockSpec((1, D_MODEL, tn), lambda l, j: (l, 0, j)),
            pl.BlockSpec((1, 1, tn), lambda l, j: (l, 0, j)),
        ],
        out_specs=pl.BlockSpec((1, rows, tn), lambda l, j: (l, 0, j)),
        out_shape=jax.ShapeDtypeStruct((DEPTH, rows, 6 * D_MODEL), F32),
        compiler_params=_params("arbitrary", "arbitrary"),
        name="ada_mod",
    )(s_in, ada_w, ada_b.reshape(DEPTH, 1, 6 * D_MODEL))


def _mlp_kernel(*refs, has_pre, has_final):
    refs = list(refs)
    h_ref, mod_ref, g_ref = refs[:3]
    refs = refs[3:]
    if has_pre:
        t_ref, wp_ref = refs[:2]
        refs = refs[2:]
    w1_ref, w2_ref = refs[:2]
    refs = refs[2:]
    if has_final:
        fg_ref = refs[0]
        refs = refs[1:]
    o_ref = refs[0]

    x = h_ref[0]
    if has_pre:
        x = x + mod_ref[0, 2:3, :] * _dot(t_ref[0], wp_ref[...])
    m = _norm_mod(x, g_ref[...], mod_ref[0, 3:4, :], mod_ref[0, 4:5, :]).astype(BF16)
    a = jnp.maximum(_dot(m, w1_ref[...]), 0.0)
    a = (a * a).astype(BF16)
    y = x + mod_ref[0, 5:6, :] * _dot(a, w2_ref[...])
    if has_final:
        y = _rms(y, fg_ref[...])
    o_ref[0] = y


def _mlp_call(h, mod, g, w1, w2, pre=None, final_g=None, tm=512):
    B, L, D = h.shape
    tm = min(tm, L)
    tok = lambda width: pl.BlockSpec((1, tm, width), lambda b, i: (b, i, 0))
    args = [h, mod, g.reshape(1, D)]
    specs = [tok(D), _mod_spec(mod), _resident((1, D))]
    if pre is not None:
        t, wp = pre
        args += [t, wp]
        specs += [tok(t.shape[-1]), _resident(wp.shape)]
    args += [w1, w2]
    specs += [_resident(w1.shape), _resident(w2.shape)]
    if final_g is not None:
        args.append(final_g.reshape(1, D))
        specs.append(_resident((1, D)))
    return pl.pallas_call(
        functools.partial(_mlp_kernel, has_pre=pre is not None, has_final=final_g is not None),
        grid=(B, L // tm),
        in_specs=specs,
        out_specs=tok(D),
        out_shape=jax.ShapeDtypeStruct(h.shape, F32),
        compiler_params=_params("parallel", "parallel"),
        name="mlp",
    )(*args)


def _pool_kernel(h_ref, hp_ref, hn_ref, mod_ref, g_ref, w_ref, sc_ref, o_ref, ext_ref, *, seq_len, tm):
    i = pl.program_id(1)
    n = pl.num_programs(1)
    g = g_ref[...]
    shift, scale, gate = mod_ref[0, 0:1, :], mod_ref[0, 1:2, :], mod_ref[0, 2:3, :]
    x = h_ref[0]
    a_main = _norm_mod(x, g, shift, scale)
    a_prev = jnp.where(i > 0, _norm_mod(hp_ref[0], g, shift, scale), 0.0)
    a_next = jnp.where(i < n - 1, _norm_mod(hn_ref[0], g, shift, scale), 0.0)
    ext_ref[0:POOL_HALO, :] = a_prev
    ext_ref[POOL_HALO:POOL_HALO + tm, :] = a_main
    ext_ref[POOL_HALO + tm:, :] = a_next

    pos = i * tm + jax.lax.broadcasted_iota(jnp.int32, (tm, 1), 0)
    ys = []
    for grp, win in enumerate(POOL_WINDOWS):
        cols = slice(grp * POOL_GW, (grp + 1) * POOL_GW)
        lo_off, hi_off = -(win // 2), win - win // 2
        s = ext_ref[POOL_HALO + lo_off:POOL_HALO + lo_off + tm, cols]
        for d in range(lo_off + 1, hi_off):
            s = s + ext_ref[POOL_HALO + d:POOL_HALO + d + tm, cols]
        cnt = (jnp.minimum(pos + hi_off, seq_len) - jnp.maximum(pos + lo_off, 0)).astype(F32)
        p = s / cnt - a_main[:, cols]
        ys.append(_dot(p.astype(BF16), w_ref[grp]))
    y = jnp.concatenate(ys, axis=-1) * sc_ref[...]
    o_ref[0] = x + gate * y


def _pool_call(h, mod, g, w, scale, tm=512):
    B, L, D = h.shape
    tm = min(tm, L)
    r = tm // POOL_HALO
    last = L // POOL_HALO - 1
    return pl.pallas_call(
        functools.partial(_pool_kernel, seq_len=L, tm=tm),
        grid=(B, L // tm),
        in_specs=[
            pl.BlockSpec((1, tm, D), lambda b, i: (b, i, 0)),
            pl.BlockSpec((1, POOL_HALO, D), lambda b, i: (b, jnp.maximum(i * r - 1, 0), 0)),
            pl.BlockSpec((1, POOL_HALO, D), lambda b, i: (b, jnp.minimum((i + 1) * r, last), 0)),
            _mod_spec(mod),
            _resident((1, D)),
            _resident(w.shape),
            _resident((1, D)),
        ],
        out_specs=pl.BlockSpec((1, tm, D), lambda b, i: (b, i, 0)),
        out_shape=jax.ShapeDtypeStruct(h.shape, F32),
        scratch_shapes=[pltpu.VMEM((tm + 2 * POOL_HALO, D), F32)],
        compiler_params=_params("parallel", "parallel"),
        name="pool_mix",
    )(h, h, h, mod, g.reshape(1, D), w, scale.reshape(1, D))


def _head_perm():
    q = HEAD_DIM // 4
    return np.concatenate([np.arange(0, q), np.arange(2 * q, 3 * q), np.arange(q, 2 * q), np.arange(3 * q, 4 * q)])


def _rope_tables(L):
    rows_n = L // GRID_W
    row = jnp.repeat(jnp.arange(rows_n), GRID_W).astype(F32)
    col = jnp.tile(jnp.arange(GRID_W), rows_n).astype(F32)
    half = HEAD_DIM // 2
    inv = ROPE_BASE ** (-jnp.arange(0, half, 2, dtype=F32) / half)
    ang_r = row[:, None] * inv[None, :]
    ang_c = col[:, None] * inv[None, :]
    cos = jnp.concatenate([jnp.cos(ang_r), jnp.cos(ang_c)], axis=-1)
    sin = jnp.concatenate([jnp.sin(ang_r), jnp.sin(ang_c)], axis=-1)
    return jnp.concatenate([cos, cos], axis=-1), jnp.concatenate([-sin, sin], axis=-1)


def _qkv_kernel(*refs, rope, want_q):
    refs = list(refs)
    h_ref, mod_ref, g_ref, w_ref, qg_ref, kg_ref = refs[:6]
    refs = refs[6:]
    if rope:
        cos_ref, sin_ref = refs[:2]
        refs = refs[2:]
    if want_q:
        q_ref = refs[0]
        refs = refs[1:]
    k_ref, v_ref = refs

    a = _norm_mod(h_ref[0], g_ref[...], mod_ref[0, 0:1, :], mod_ref[0, 1:2, :]).astype(BF16)

    def heads(col0, n_heads, gain, out_ref):
        y = _dot(a, w_ref[:, col0:col0 + n_heads * HEAD_DIM])
        for hd in range(n_heads):
            t = _rms(y[:, hd * HEAD_DIM:(hd + 1) * HEAD_DIM], gain)
            if rope:
                t = t * cos_ref[...] + pltpu.roll(t, HEAD_DIM // 2, axis=1) * sin_ref[...]
            out_ref[0, :, hd * HEAD_DIM:(hd + 1) * HEAD_DIM] = t.astype(out_ref.dtype)

    if want_q:
        heads(0, N_HEADS, qg_ref[...], q_ref)
    heads(Q_WIDTH, N_KV_HEADS, kg_ref[...], k_ref)
    v_ref[0] = _dot(a, w_ref[:, Q_WIDTH + KV_WIDTH:]).astype(v_ref.dtype)


def _qkv_call(h, mod, g, w, q_g, k_g, rope_tabs, want_q, tm=512):
    B, L, D = h.shape
    tm = min(tm, L)
    tok = lambda width: pl.BlockSpec((1, tm, width), lambda b, i: (b, i, 0))
    args = [h, mod, g.reshape(1, D), w, q_g.reshape(1, HEAD_DIM), k_g.reshape(1, HEAD_DIM)]
    specs = [tok(D), _mod_spec(mod), _resident((1, D)), _resident(w.shape),
             _resident((1, HEAD_DIM)), _resident((1, HEAD_DIM))]
    if rope_tabs is not None:
        args += list(rope_tabs)
        specs += [pl.BlockSpec((tm, HEAD_DIM), lambda b, i: (i, 0))] * 2
    out_shape, out_specs = [], []
    if want_q:
        out_shape.append(jax.ShapeDtypeStruct((B, L, Q_WIDTH), BF16))
        out_specs.append(tok(Q_WIDTH))
    out_shape += [jax.ShapeDtypeStruct((B, L, KV_WIDTH), BF16)] * 2
    out_specs += [tok(KV_WIDTH)] * 2
    return pl.pallas_call(
        functools.partial(_qkv_kernel, rope=rope_tabs is not None, want_q=want_q),
        grid=(B, L // tm),
        in_specs=specs,
        out_specs=out_specs,
        out_shape=out_shape,
        compiler_params=_params("parallel", "parallel"),
        name="qkv_proj",
    )(*args)


def _flash_kernel(q_ref, k_ref, v_ref, o_ref, m_ref, l_ref, acc_ref, *, tq, tk):
    j = pl.program_id(3)

    @pl.when(j == 0)
    def _():
        m_ref[...] = jnp.full(m_ref.shape, -jnp.inf, F32)
        l_ref[...] = jnp.zeros(l_ref.shape, F32)
        acc_ref[...] = jnp.zeros(acc_ref.shape, F32)

    q = jnp.concatenate([q_ref[0, :, hd * HEAD_DIM:(hd + 1) * HEAD_DIM] for hd in range(Q_PER_KV)], axis=0)
    s = jax.lax.dot_general(q, k_ref[0], (((1,), (1,)), ((), ())), preferred_element_type=F32)
    c = (HEAD_DIM ** -0.5) * LOG2E
    m_prev = m_ref[...]
    m_next = jnp.maximum(m_prev, jnp.max(s, axis=1, keepdims=True))
    alpha = jnp.exp2((m_prev - m_next) * c)
    p = jnp.exp2((s - jnp.tile(m_next, (1, tk // 128))) * c)
    l_ref[...] = alpha * l_ref[...] + jnp.sum(p, axis=1, keepdims=True)
    acc_ref[...] = alpha * acc_ref[...] + _dot(p.astype(BF16), v_ref[0])
    m_ref[...] = m_next

    @pl.when(j == pl.num_programs(3) - 1)
    def _():
        o = acc_ref[...] / l_ref[...]
        for hd in range(Q_PER_KV):
            o_ref[0, :, hd * HEAD_DIM:(hd + 1) * HEAD_DIM] = o[hd * tq:(hd + 1) * tq].astype(o_ref.dtype)


def _flash_call(q, k, v, tq=512, tk=768):
    B, L, _ = q.shape
    Lk = k.shape[1]
    rows = Q_PER_KV * tq
    return pl.pallas_call(
        functools.partial(_flash_kernel, tq=tq, tk=tk),
        grid=(B, N_KV_HEADS, L // tq, Lk // tk),
        in_specs=[
            pl.BlockSpec((1, tq, Q_PER_KV * HEAD_DIM), lambda b, kh, i, j: (b, i, kh)),
            pl.BlockSpec((1, tk, HEAD_DIM), lambda b, kh, i, j: (b, j, kh)),
            pl.BlockSpec((1, tk, HEAD_DIM), lambda b, kh, i, j: (b, j, kh)),
        ],
        out_specs=pl.BlockSpec((1, tq, Q_PER_KV * HEAD_DIM), lambda b, kh, i, j: (b, i, kh)),
        out_shape=jax.ShapeDtypeStruct(q.shape, BF16),
        scratch_shapes=[pltpu.VMEM((rows, 128), F32), pltpu.VMEM((rows, 128), F32),
                        pltpu.VMEM((rows, HEAD_DIM), F32)],
        compiler_params=_params("parallel", "parallel", "parallel", "arbitrary"),
        name="flash_gqa",
    )(q, k, v)


def _gmlp_kernel(h_ref, mod_ref, g_ref, win_ref, lng_ref, lnb_ref, ws_ref, bst_ref, o_ref, *, tm):
    a = _norm_mod(h_ref[0], g_ref[...], mod_ref[0, 0:1, :], mod_ref[0, 1:2, :]).astype(BF16)
    z = jax.nn.gelu(_dot(a, win_ref[...]), approximate=True)
    u, v = z[:, :GMLP_HALF], z[:, GMLP_HALF:]
    mu = jnp.mean(v, axis=-1, keepdims=True)
    vc = v - mu
    v = vc * jax.lax.rsqrt(jnp.mean(vc * vc, axis=-1, keepdims=True) + EPS) * lng_ref[...] + lnb_ref[...]
    v = v.astype(BF16)
    for ch in range(tm // CHUNK):
        rows = slice(ch * CHUNK, (ch + 1) * CHUNK)
        for grp in range(GMLP_GROUPS):
            cols = slice(grp * GMLP_GW, (grp + 1) * GMLP_GW)
            sv = _dot(ws_ref[grp], v[rows, cols]) + bst_ref[:, grp:grp + 1]
            o_ref[0, rows, cols] = (u[rows, cols] * sv).astype(o_ref.dtype)


def _gmlp_call(h, mod, g, w_in, ln_g, ln_b, ws, bs, tm=256):
    B, L, D = h.shape
    return pl.pallas_call(
        functools.partial(_gmlp_kernel, tm=tm),
        grid=(B, L // tm),
        in_specs=[
            pl.BlockSpec((1, tm, D), lambda b, i: (b, i, 0)),
            _mod_spec(mod),
            _resident((1, D)),
            _resident(w_in.shape),
            _resident((1, GMLP_HALF)),
            _resident((1, GMLP_HALF)),
            _resident(ws.shape),
            _resident((CHUNK, GMLP_GROUPS)),
        ],
        out_specs=pl.BlockSpec((1, tm, GMLP_HALF), lambda b, i: (b, i, 0)),
        out_shape=jax.ShapeDtypeStruct((B, L, GMLP_HALF), BF16),
        compiler_params=_params("parallel", "parallel"),
        name="gmlp_gate",
    )(h, mod, g.reshape(1, D), w_in, ln_g.reshape(1, GMLP_HALF), ln_b.reshape(1, GMLP_HALF), ws, bs.T)


def kernel(x, c, ctx, c_ctx, ada_w, ada_b, norm_g, mlp_w1, mlp_w2, pool_w, pool_scale, attn_w_qkv, attn_w_o,
           attn_q_g, attn_k_g, gm_w_in, gm_ln_g, gm_ln_b, gm_ws, gm_bs, gm_w_out, final_g):
    B, S, D = x.shape
    last_ctx_read = max([i for i in range(DEPTH) if i % N_MIXERS == 1], default=-1)

    pad = (-(B + 1)) % 8
    s_in = jnp.concatenate([c, c_ctx[None, :], jnp.zeros((pad, D), F32)], axis=0)
    mods = _ada_call(s_in, ada_w, ada_b)
    mod_lat = mods[:, :B].reshape(DEPTH, B, 6, D)
    mod_ctx = mods[:, B:B + 1].reshape(DEPTH, 1, 6, D)

    perm = _head_perm()
    h_lat, h_ctx = x, ctx
    for i in range(DEPTH):
        kind, j = i % N_MIXERS, i // N_MIXERS
        ctx_in = i <= last_ctx_read
        ctx_out = i < last_ctx_read
        w1, w2 = mlp_w1[i].astype(BF16), mlp_w2[i].astype(BF16)
        fin = final_g if i == DEPTH - 1 else None
        if kind == 0:
            pw = pool_w[j].astype(BF16)
            h_lat = _pool_call(h_lat, mod_lat[i], norm_g[i, 0], pw, pool_scale[j])
            h_lat = _mlp_call(h_lat, mod_lat[i], norm_g[i, 1], w1, w2, final_g=fin)
            if ctx_out:
                h_ctx = _pool_call(h_ctx, mod_ctx[i], norm_g[i, 0], pw, pool_scale[j])
                h_ctx = _mlp_call(h_ctx, mod_ctx[i], norm_g[i, 1], w1, w2)
        elif kind == 1:
            wqkv = attn_w_qkv[j]
            cols = np.concatenate([hd * HEAD_DIM + perm for hd in range(N_HEADS + N_KV_HEADS)]
                                  + [np.arange(Q_WIDTH + KV_WIDTH, wqkv.shape[1])])
            wqkv = wqkv[:, cols].astype(BF16)
            q_g, k_g = attn_q_g[j][perm], attn_k_g[j][perm]
            q, k_l, v_l = _qkv_call(h_lat, mod_lat[i], norm_g[i, 0], wqkv, q_g, k_g, _rope_tables(S), True)
            k_c, v_c = _qkv_call(h_ctx, mod_ctx[i], norm_g[i, 0], wqkv, q_g, k_g, None, False)
            o = _flash_call(q, jnp.concatenate([k_c, k_l], axis=1), jnp.concatenate([v_c, v_l], axis=1))
            if ctx_out:
                raise NotImplementedError("context stream output of an attention layer")
            h_lat = _mlp_call(h_lat, mod_lat[i], norm_g[i, 1], w1, w2,
                              pre=(o, attn_w_o[j].astype(BF16)), final_g=fin)
        else:
            t = _gmlp_call(h_lat, mod_lat[i], norm_g[i, 0], gm_w_in[j].astype(BF16), gm_ln_g[j], gm_ln_b[j],
                           gm_ws[j].astype(BF16), gm_bs[j])
            if ctx_out:
                raise NotImplementedError("context stream output of a gMLP layer")
            h_lat = _mlp_call(h_lat, mod_lat[i], norm_g[i, 1], w1, w2,
                              pre=(t, gm_w_out[j].astype(BF16)), final_g=fin)
    return h_lat
```

```python
import functools

import jax
import jax.numpy as jnp
import numpy as np
from jax.experimental import pallas as pl
from jax.experimental.pallas import tpu as pltpu

D_MODEL = 1024
DEPTH = 4
N_MIXERS = 3
GRID_W = 64
EPS = 1e-6
POOL_WINDOWS = (2, 4, 8, 16)
POOL_GW = D_MODEL // len(POOL_WINDOWS)
POOL_HALO = 16
HEAD_DIM = 128
N_HEADS = D_MODEL // HEAD_DIM
N_KV_HEADS = N_HEADS // 2
Q_PER_KV = N_HEADS // N_KV_HEADS
Q_WIDTH = N_HEADS * HEAD_DIM
KV_WIDTH = N_KV_HEADS * HEAD_DIM
VT_ROWS = HEAD_DIM + 16
ROPE_BASE = 10000.0
CHUNK = 128
GMLP_HALF = 2 * D_MODEL
GMLP_GROUPS = 8
GMLP_GW = GMLP_HALF // GMLP_GROUPS
D_FF = 4 * D_MODEL
LOG2E = 1.4426950408889634

VMEM_LIMIT_BYTES = 56 * 1024 * 1024
BF16 = jnp.bfloat16
F32 = jnp.float32


def _params(*semantics):
    return pltpu.CompilerParams(dimension_semantics=semantics, vmem_limit_bytes=VMEM_LIMIT_BYTES)


def _resident(shape):
    zeros = (0,) * len(shape)
    return pl.BlockSpec(shape, lambda *_: zeros, pipeline_mode=pl.Buffered(1))


def _mod_spec(mod):
    if mod.shape[0] == 1:
        return pl.BlockSpec((1, 6, D_MODEL), lambda b, i: (0, 0, 0))
    return pl.BlockSpec((1, 6, D_MODEL), lambda b, i: (b, 0, 0))


def _rms(x, g):
    return x * jax.lax.rsqrt(jnp.mean(x * x, axis=-1, keepdims=True) + EPS) * g


def _norm_mod(x, g, shift, scale):
    return _rms(x, g) * (1.0 + scale) + shift


def _dot(a, b):
    return jnp.dot(a, b, preferred_element_type=F32)


def _ada_kernel(s_ref, w_ref, b_ref, o_ref):
    s = s_ref[...]
    s = s * jax.nn.sigmoid(s)
    o_ref[0] = _dot(s.astype(BF16), w_ref[0].astype(BF16)) + b_ref[0]


def _ada_call(s_in, ada_w, ada_b):
    rows = s_in.shape[0]
    tn = 1536
    return pl.pallas_call(
        _ada_kernel,
        grid=(DEPTH, 6 * D_MODEL // tn),
        in_specs=[
            pl.BlockSpec((rows, D_MODEL), lambda l, j: (0, 0)),
            pl.BlockSpec((1, D_MODEL, tn), lambda l, j: (l, 0, j)),
            pl.BlockSpec((1, 1, tn), lambda l, j: (l, 0, j)),
        ],
        out_specs=pl.BlockSpec((1, rows, tn), lambda l, j: (l, 0, j)),
        out_shape=jax.ShapeDtypeStruct((DEPTH, rows, 6 * D_MODEL), F32),
        compiler_params=_params("arbitrary", "arbitrary"),
        name="ada_mod",
    )(s_in, ada_w, ada_b.reshape(DEPTH, 1, 6 * D_MODEL))


def _mlp_kernel(*refs, has_pre, has_final):
    refs = list(refs)
    h_ref, mod_ref, g_ref = refs[:3]
    refs = refs[3:]
    if has_pre:
        t_ref, wp_ref = refs[:2]
        refs = refs[2:]
    w1_ref, w2_ref = refs[:2]
    refs = refs[2:]
    if has_final:
        fg_ref = refs[0]
        refs = refs[1:]
    o_ref = refs[0]

    x = h_ref[0]
    if has_pre:
        x = x + mod_ref[0, 2:3, :] * _dot(t_ref[0], wp_ref[...])
    m = _norm_mod(x, g_ref[...], mod_ref[0, 3:4, :], mod_ref[0, 4:5, :]).astype(BF16)
    a = jnp.maximum(_dot(m, w1_ref[...]), 0.0)
    a = (a * a).astype(BF16)
    y = x + mod_ref[0, 5:6, :] * _dot(a, w2_ref[...])
    if has_final:
        y = _rms(y, fg_ref[...])
    o_ref[0] = y


def _mlp_call(h, mod, g, w1, w2, pre=None, final_g=None, tm=512):
    B, L, D = h.shape
    tm = min(tm, L)
    tok = lambda width: pl.BlockSpec((1, tm, width), lambda b, i: (b, i, 0))
    args = [h, mod, g.reshape(1, D)]
    specs = [tok(D), _mod_spec(mod), _resident((1, D))]
    if pre is not None:
        t, wp = pre
        args += [t, wp]
        specs += [tok(t.shape[-1]), _resident(wp.shape)]
    args += [w1, w2]
    specs += [_resident(w1.shape), _resident(w2.shape)]
    if final_g is not None:
        args.append(final_g.reshape(1, D))
        specs.append(_resident((1, D)))
    return pl.pallas_call(
        functools.partial(_mlp_kernel, has_pre=pre is not None, has_final=final_g is not None),
        grid=(B, L // tm),
        in_specs=specs,
        out_specs=tok(D),
        out_shape=jax.ShapeDtypeStruct(h.shape, F32),
        compiler_params=_params("parallel", "parallel"),
        name="mlp",
    )(*args)


def _pool_kernel(h_ref, hp_ref, hn_ref, mod_ref, g_ref, w_ref, sc_ref, o_ref, ext_ref, *, seq_len, tm):
    i = pl.program_id(1)
    n = pl.num_programs(1)
    g = g_ref[...]
    shift, scale, gate = mod_ref[0, 0:1, :], mod_ref[0, 1:2, :], mod_ref[0, 2:3, :]
    x = h_ref[0]
    a_main = _norm_mod(x, g, shift, scale)
    a_prev = jnp.where(i > 0, _norm_mod(hp_ref[0], g, shift, scale), 0.0)
    a_next = jnp.where(i < n - 1, _norm_mod(hn_ref[0], g, shift, scale), 0.0)
    ext_ref[0:POOL_HALO, :] = a_prev
    ext_ref[POOL_HALO:POOL_HALO + tm, :] = a_main
    ext_ref[POOL_HALO + tm:, :] = a_next

    pos = i * tm + jax.lax.broadcasted_iota(jnp.int32, (tm, 1), 0)
    ys = []
    for grp, win in enumerate(POOL_WINDOWS):
        cols = slice(grp * POOL_GW, (grp + 1) * POOL_GW)
        lo_off, hi_off = -(win // 2), win - win // 2
        s = ext_ref[POOL_HALO + lo_off:POOL_HALO + lo_off + tm, cols]
        for d in range(lo_off + 1, hi_off):
            s = s + ext_ref[POOL_HALO + d:POOL_HALO + d + tm, cols]
        cnt = (jnp.minimum(pos + hi_off, seq_len) - jnp.maximum(pos + lo_off, 0)).astype(F32)
        p = s / cnt - a_main[:, cols]
        ys.append(_dot(p.astype(BF16), w_ref[grp]))
    y = jnp.concatenate(ys, axis=-1) * sc_ref[...]
    o_ref[0] = x + gate * y


def _pool_call(h, mod, g, w, scale, tm=512):
    B, L, D = h.shape
    tm = min(tm, L)
    r = tm // POOL_HALO
    last = L // POOL_HALO - 1
    return pl.pallas_call(
        functools.partial(_pool_kernel, seq_len=L, tm=tm),
        grid=(B, L // tm),
        in_specs=[
            pl.BlockSpec((1, tm, D), lambda b, i: (b, i, 0)),
            pl.BlockSpec((1, POOL_HALO, D), lambda b, i: (b, jnp.maximum(i * r - 1, 0), 0)),
            pl.BlockSpec((1, POOL_HALO, D), lambda b, i: (b, jnp.minimum((i + 1) * r, last), 0)),
            _mod_spec(mod),
            _resident((1, D)),
            _resident(w.shape),
            _resident((1, D)),
        ],
        out_specs=pl.BlockSpec((1, tm, D), lambda b, i: (b, i, 0)),
        out_shape=jax.ShapeDtypeStruct(h.shape, F32),
        scratch_shapes=[pltpu.VMEM((tm + 2 * POOL_HALO, D), F32)],
        compiler_params=_params("parallel", "parallel"),
        name="pool_mix",
    )(h, h, h, mod, g.reshape(1, D), w, scale.reshape(1, D))


def _head_perm():
    q = HEAD_DIM // 4
    return np.concatenate([np.arange(0, q), np.arange(2 * q, 3 * q), np.arange(q, 2 * q), np.arange(3 * q, 4 * q)])


def _rope_tables(L):
    rows_n = L // GRID_W
    row = jnp.repeat(jnp.arange(rows_n), GRID_W).astype(F32)
    col = jnp.tile(jnp.arange(GRID_W), rows_n).astype(F32)
    half = HEAD_DIM // 2
    inv = ROPE_BASE ** (-jnp.arange(0, half, 2, dtype=F32) / half)
    ang_r = row[:, None] * inv[None, :]
    ang_c = col[:, None] * inv[None, :]
    cos = jnp.concatenate([jnp.cos(ang_r), jnp.cos(ang_c)], axis=-1)
    sin = jnp.concatenate([jnp.sin(ang_r), jnp.sin(ang_c)], axis=-1)
    return jnp.concatenate([cos, cos], axis=-1), jnp.concatenate([-sin, sin], axis=-1)


def _qkv_kernel(*refs, rope, want_q):
    refs = list(refs)
    h_ref, mod_ref, g_ref, w_ref, qg_ref, kg_ref = refs[:6]
    refs = refs[6:]
    if rope:
        cos_ref, sin_ref = refs[:2]
        refs = refs[2:]
    if want_q:
        q_ref = refs[0]
        refs = refs[1:]
    k_ref, vt_ref = refs

    a = _norm_mod(h_ref[0], g_ref[...], mod_ref[0, 0:1, :], mod_ref[0, 1:2, :]).astype(BF16)

    def heads(col0, n_heads, gain, out_ref):
        y = _dot(a, w_ref[:, col0:col0 + n_heads * HEAD_DIM])
        for hd in range(n_heads):
            t = _rms(y[:, hd * HEAD_DIM:(hd + 1) * HEAD_DIM], gain)
            if rope:
                t = t * cos_ref[...] + pltpu.roll(t, HEAD_DIM // 2, axis=1) * sin_ref[...]
            out_ref[0, :, hd * HEAD_DIM:(hd + 1) * HEAD_DIM] = t.astype(out_ref.dtype)

    if want_q:
        heads(0, N_HEADS, qg_ref[...], q_ref)
    heads(Q_WIDTH, N_KV_HEADS, kg_ref[...], k_ref)
    v = _dot(a, w_ref[:, Q_WIDTH + KV_WIDTH:])
    for hd in range(N_KV_HEADS):
        vt_ref[0, hd, 0:HEAD_DIM, :] = v[:, hd * HEAD_DIM:(hd + 1) * HEAD_DIM].T.astype(vt_ref.dtype)
        vt_ref[0, hd, HEAD_DIM:, :] = jnp.ones((VT_ROWS - HEAD_DIM, v.shape[0]), vt_ref.dtype)


def _qkv_call(h, mod, g, w, q_g, k_g, rope_tabs, want_q, tm=512):
    B, L, D = h.shape
    tm = min(tm, L)
    tok = lambda width: pl.BlockSpec((1, tm, width), lambda b, i: (b, i, 0))
    args = [h, mod, g.reshape(1, D), w, q_g.reshape(1, HEAD_DIM), k_g.reshape(1, HEAD_DIM)]
    specs = [tok(D), _mod_spec(mod), _resident((1, D)), _resident(w.shape),
             _resident((1, HEAD_DIM)), _resident((1, HEAD_DIM))]
    if rope_tabs is not None:
        args += list(rope_tabs)
        specs += [pl.BlockSpec((tm, HEAD_DIM), lambda b, i: (i, 0))] * 2
    out_shape, out_specs = [], []
    if want_q:
        out_shape.append(jax.ShapeDtypeStruct((B, L, Q_WIDTH), BF16))
        out_specs.append(tok(Q_WIDTH))
    out_shape += [jax.ShapeDtypeStruct((B, L, KV_WIDTH), BF16),
                  jax.ShapeDtypeStruct((B, N_KV_HEADS, VT_ROWS, L), BF16)]
    out_specs += [tok(KV_WIDTH), pl.BlockSpec((1, N_KV_HEADS, VT_ROWS, tm), lambda b, i: (b, 0, 0, i))]
    return pl.pallas_call(
        functools.partial(_qkv_kernel, rope=rope_tabs is not None, want_q=want_q),
        grid=(B, L // tm),
        in_specs=specs,
        out_specs=out_specs,
        out_shape=out_shape,
        compiler_params=_params("parallel", "parallel"),
        name="qkv_proj",
    )(*args)


def _flash_kernel(q_ref, k_ref, vt_ref, o_ref, *, tq, tk):
    q = jnp.concatenate([q_ref[0, :, hd * HEAD_DIM:(hd + 1) * HEAD_DIM] for hd in range(Q_PER_KV)], axis=0)
    c = (HEAD_DIM ** -0.5) * LOG2E
    n_kv = k_ref.shape[1] // tk

    def scores(j):
        return jax.lax.dot_general(k_ref[0, j * tk:(j + 1) * tk, :], q, (((1,), (1,)), ((), ())),
                                   preferred_element_type=F32)

    m = jnp.full((1, q.shape[0]), -jnp.inf, F32)
    acc = jnp.zeros((VT_ROWS, q.shape[0]), F32)
    st = scores(0)
    for j in range(n_kv):
        st_next = scores(j + 1) if j + 1 < n_kv else None
        m_next = jnp.maximum(m, jnp.max(st, axis=0, keepdims=True))
        alpha = jnp.exp2((m - m_next) * c)
        p = jnp.exp2((st - m_next) * c).astype(BF16)
        acc = alpha * acc + _dot(vt_ref[0, 0, :, j * tk:(j + 1) * tk], p)
        m, st = m_next, st_next

    o = (acc[:HEAD_DIM] / acc[HEAD_DIM:HEAD_DIM + 1]).T
    for hd in range(Q_PER_KV):
        o_ref[0, :, hd * HEAD_DIM:(hd + 1) * HEAD_DIM] = o[hd * tq:(hd + 1) * tq].astype(o_ref.dtype)


def _flash_call(q, k, vt, tq=512, tk=768):
    B, L, _ = q.shape
    Lk = k.shape[1]
    return pl.pallas_call(
        functools.partial(_flash_kernel, tq=tq, tk=tk),
        grid=(B, N_KV_HEADS, L // tq),
        in_specs=[
            pl.BlockSpec((1, tq, Q_PER_KV * HEAD_DIM), lambda b, kh, i: (b, i, kh)),
            pl.BlockSpec((1, Lk, HEAD_DIM), lambda b, kh, i: (b, 0, kh)),
            pl.BlockSpec((1, 1, VT_ROWS, Lk), lambda b, kh, i: (b, kh, 0, 0)),
        ],
        out_specs=pl.BlockSpec((1, tq, Q_PER_KV * HEAD_DIM), lambda b, kh, i: (b, i, kh)),
        out_shape=jax.ShapeDtypeStruct(q.shape, BF16),
        compiler_params=_params("parallel", "parallel", "parallel"),
        name="flash_gqa",
    )(q, k, vt)


def _gmlp_kernel(h_ref, mod_ref, g_ref, win_ref, lng_ref, lnb_ref, ws_ref, bst_ref, o_ref, *, tm):
    a = _norm_mod(h_ref[0], g_ref[...], mod_ref[0, 0:1, :], mod_ref[0, 1:2, :]).astype(BF16)
    z = jax.nn.gelu(_dot(a, win_ref[...]), approximate=True)
    u, v = z[:, :GMLP_HALF], z[:, GMLP_HALF:]
    mu = jnp.mean(v, axis=-1, keepdims=True)
    vc = v - mu
    v = vc * jax.lax.rsqrt(jnp.mean(vc * vc, axis=-1, keepdims=True) + EPS) * lng_ref[...] + lnb_ref[...]
    v = v.astype(BF16)
    for ch in range(tm // CHUNK):
        rows = slice(ch * CHUNK, (ch + 1) * CHUNK)
        for grp in range(GMLP_GROUPS):
            cols = slice(grp * GMLP_GW, (grp + 1) * GMLP_GW)
            sv = _dot(ws_ref[grp], v[rows, cols]) + bst_ref[:, grp:grp + 1]
            o_ref[0, rows, cols] = (u[rows, cols] * sv).astype(o_ref.dtype)


def _gmlp_call(h, mod, g, w_in, ln_g, ln_b, ws, bs, tm=256):
    B, L, D = h.shape
    return pl.pallas_call(
        functools.partial(_gmlp_kernel, tm=tm),
        grid=(B, L // tm),
        in_specs=[
            pl.BlockSpec((1, tm, D), lambda b, i: (b, i, 0)),
            _mod_spec(mod),
            _resident((1, D)),
            _resident(w_in.shape),
            _resident((1, GMLP_HALF)),
            _resident((1, GMLP_HALF)),
            _resident(ws.shape),
            _resident((CHUNK, GMLP_GROUPS)),
        ],
        out_specs=pl.BlockSpec((1, tm, GMLP_HALF), lambda b, i: (b, i, 0)),
        out_shape=jax.ShapeDtypeStruct((B, L, GMLP_HALF), BF16),
        compiler_params=_params("parallel", "parallel"),
        name="gmlp_gate",
    )(h, mod, g.reshape(1, D), w_in, ln_g.reshape(1, GMLP_HALF), ln_b.reshape(1, GMLP_HALF), ws, bs.T)


def kernel(x, c, ctx, c_ctx, ada_w, ada_b, norm_g, mlp_w1, mlp_w2, pool_w, pool_scale, attn_w_qkv, attn_w_o,
           attn_q_g, attn_k_g, gm_w_in, gm_ln_g, gm_ln_b, gm_ws, gm_bs, gm_w_out, final_g):
    B, S, D = x.shape
    last_ctx_read = max([i for i in range(DEPTH) if i % N_MIXERS == 1], default=-1)

    pad = (-(B + 1)) % 8
    s_in = jnp.concatenate([c, c_ctx[None, :], jnp.zeros((pad, D), F32)], axis=0)
    mods = _ada_call(s_in, ada_w, ada_b)
    mod_lat = mods[:, :B].reshape(DEPTH, B, 6, D)
    mod_ctx = mods[:, B:B + 1].reshape(DEPTH, 1, 6, D)

    perm = _head_perm()
    h_lat, h_ctx = x, ctx
    for i in range(DEPTH):
        kind, j = i % N_MIXERS, i // N_MIXERS
        ctx_in = i <= last_ctx_read
        ctx_out = i < last_ctx_read
        w1, w2 = mlp_w1[i].astype(BF16), mlp_w2[i].astype(BF16)
        fin = final_g if i == DEPTH - 1 else None
        if kind == 0:
            pw = pool_w[j].astype(BF16)
            h_lat = _pool_call(h_lat, mod_lat[i], norm_g[i, 0], pw, pool_scale[j])
            h_lat = _mlp_call(h_lat, mod_lat[i], norm_g[i, 1], w1, w2, final_g=fin)
            if ctx_out:
                h_ctx = _pool_call(h_ctx, mod_ctx[i], norm_g[i, 0], pw, pool_scale[j])
                h_ctx = _mlp_call(h_ctx, mod_ctx[i], norm_g[i, 1], w1, w2)
        elif kind == 1:
            wqkv = attn_w_qkv[j]
            cols = np.concatenate([hd * HEAD_DIM + perm for hd in range(N_HEADS + N_KV_HEADS)]
                                  + [np.arange(Q_WIDTH + KV_WIDTH, wqkv.shape[1])])
            wqkv = wqkv[:, cols].astype(BF16)
            q_g, k_g = attn_q_g[j][perm], attn_k_g[j][perm]
            q, k_l, vt_l = _qkv_call(h_lat, mod_lat[i], norm_g[i, 0], wqkv, q_g, k_g, _rope_tables(S), True)
            k_c, vt_c = _qkv_call(h_ctx, mod_ctx[i], norm_g[i, 0], wqkv, q_g, k_g, None, False)
            o = _flash_call(q, jnp.concatenate([k_c, k_l], axis=1), jnp.concatenate([vt_c, vt_l], axis=-1))
            if ctx_out:
                raise NotImplementedError("context stream output of an attention layer")
            h_lat = _mlp_call(h_lat, mod_lat[i], norm_g[i, 1], w1, w2,
                              pre=(o, attn_w_o[j].astype(BF16)), final_g=fin)
        else:
            t = _gmlp_call(h_lat, mod_lat[i], norm_g[i, 0], gm_w_in[j].astype(BF16), gm_ln_g[j], gm_ln_b[j],
                           gm_ws[j].astype(BF16), gm_bs[j])
            if ctx_out:
                raise NotImplementedError("context stream output of a gMLP layer")
            h_lat = _mlp_call(h_lat, mod_lat[i], norm_g[i, 1], w1, w2,
                              pre=(t, gm_w_out[j].astype(BF16)), final_g=fin)
    return h_lat
```

```python
import functools

import jax
import jax.numpy as jnp
import numpy as np
from jax.experimental import pallas as pl
from jax.experimental.pallas import tpu as pltpu

D_MODEL = 1024
DEPTH = 4
N_MIXERS = 3
GRID_W = 64
EPS = 1e-6
POOL_WINDOWS = (2, 4, 8, 16)
POOL_GW = D_MODEL // len(POOL_WINDOWS)
POOL_HALO = 16
HEAD_DIM = 128
N_HEADS = D_MODEL // HEAD_DIM
N_KV_HEADS = N_HEADS // 2
Q_PER_KV = N_HEADS // N_KV_HEADS
Q_WIDTH = N_HEADS * HEAD_DIM
KV_WIDTH = N_KV_HEADS * HEAD_DIM
VT_ROWS = HEAD_DIM + 16
ROPE_BASE = 10000.0
CHUNK = 128
GMLP_HALF = 2 * D_MODEL
GMLP_GROUPS = 8
GMLP_GW = GMLP_HALF // GMLP_GROUPS
D_FF = 4 * D_MODEL
LOG2E = 1.4426950408889634
FLASH_MAX_EXCESS = 64.0

VMEM_LIMIT_BYTES = 56 * 1024 * 1024
BF16 = jnp.bfloat16
F32 = jnp.float32


def _params(*semantics):
    return pltpu.CompilerParams(dimension_semantics=semantics, vmem_limit_bytes=VMEM_LIMIT_BYTES)


def _resident(shape):
    zeros = (0,) * len(shape)
    return pl.BlockSpec(shape, lambda *_: zeros, pipeline_mode=pl.Buffered(1))


def _mod_spec(mod):
    if mod.shape[0] == 1:
        return pl.BlockSpec((1, 6, D_MODEL), lambda b, i: (0, 0, 0))
    return pl.BlockSpec((1, 6, D_MODEL), lambda b, i: (b, 0, 0))


def _rms(x, g):
    return x * jax.lax.rsqrt(jnp.mean(x * x, axis=-1, keepdims=True) + EPS) * g


def _norm_mod(x, g, shift, scale):
    return _rms(x, g) * (1.0 + scale) + shift


def _dot(a, b):
    return jnp.dot(a, b, preferred_element_type=F32)


def _ada_kernel(s_ref, w_ref, b_ref, o_ref):
    s = s_ref[...]
    s = s * jax.nn.sigmoid(s)
    o_ref[0] = _dot(s.astype(BF16), w_ref[0].astype(BF16)) + b_ref[0]


def _ada_call(s_in, ada_w, ada_b):
    rows = s_in.shape[0]
    tn = 1536
    return pl.pallas_call(
        _ada_kernel,
        grid=(DEPTH, 6 * D_MODEL // tn),
        in_specs=[
            pl.BlockSpec((rows, D_MODEL), lambda l, j: (0, 0)),
            pl.BlockSpec((1, D_MODEL, tn), lambda l, j: (l, 0, j)),
            pl.BlockSpec((1, 1, tn), lambda l, j: (l, 0, j)),
        ],
        out_specs=pl.BlockSpec((1, rows, tn), lambda l, j: (l, 0, j)),
        out_shape=jax.ShapeDtypeStruct((DEPTH, rows, 6 * D_MODEL), F32),
        compiler_params=_params("arbitrary", "arbitrary"),
        name="ada_mod",
    )(s_in, ada_w, ada_b.reshape(DEPTH, 1, 6 * D_MODEL))


def _mlp_kernel(*refs, has_pre, has_final):
    refs = list(refs)
    h_ref, mod_ref, g_ref = refs[:3]
    refs = refs[3:]
    if has_pre:
        t_ref, wp_ref = refs[:2]
        refs = refs[2:]
    w1_ref, w2_ref = refs[:2]
    refs = refs[2:]
    if has_final:
        fg_ref = refs[0]
        refs = refs[1:]
    o_ref = refs[0]

    x = h_ref[0]
    if has_pre:
        x = x + mod_ref[0, 2:3, :] * _dot(t_ref[0], wp_ref[...])
    m = _norm_mod(x, g_ref[...], mod_ref[0, 3:4, :], mod_ref[0, 4:5, :]).astype(BF16)
    a = jnp.maximum(_dot(m, w1_ref[...]), 0.0)
    a = (a * a).astype(BF16)
    y = x + mod_ref[0, 5:6, :] * _dot(a, w2_ref[...])
    if has_final:
        y = _rms(y, fg_ref[...])
    o_ref[0] = y


def _mlp_call(h, mod, g, w1, w2, pre=None, final_g=None, tm=512):
    B, L, D = h.shape
    tm = min(tm, L)
    tok = lambda width: pl.BlockSpec((1, tm, width), lambda b, i: (b, i, 0))
    args = [h, mod, g.reshape(1, D)]
    specs = [tok(D), _mod_spec(mod), _resident((1, D))]
    if pre is not None:
        t, wp = pre
        args += [t, wp]
        specs += [tok(t.shape[-1]), _resident(wp.shape)]
    args += [w1, w2]
    specs += [_resident(w1.shape), _resident(w2.shape)]
    if final_g is not None:
        args.append(final_g.reshape(1, D))
        specs.append(_resident((1, D)))
    return pl.pallas_call(
        functools.partial(_mlp_kernel, has_pre=pre is not None, has_final=final_g is not None),
        grid=(B, L // tm),
        in_specs=specs,
        out_specs=tok(D),
        out_shape=jax.ShapeDtypeStruct(h.shape, F32),
        compiler_params=_params("parallel", "parallel"),
        name="mlp",
    )(*args)


def _pool_kernel(h_ref, hp_ref, hn_ref, mod_ref, g_ref, w_ref, sc_ref, o_ref, ext_ref, *, seq_len, tm):
    i = pl.program_id(1)
    n = pl.num_programs(1)
    g = g_ref[...]
    shift, scale, gate = mod_ref[0, 0:1, :], mod_ref[0, 1:2, :], mod_ref[0, 2:3, :]
    x = h_ref[0]
    a_main = _norm_mod(x, g, shift, scale)
    a_prev = jnp.where(i > 0, _norm_mod(hp_ref[0], g, shift, scale), 0.0)
    a_next = jnp.where(i < n - 1, _norm_mod(hn_ref[0], g, shift, scale), 0.0)
    ext_ref[0:POOL_HALO, :] = a_prev
    ext_ref[POOL_HALO:POOL_HALO + tm, :] = a_main
    ext_ref[POOL_HALO + tm:, :] = a_next

    pos = i * tm + jax.lax.broadcasted_iota(jnp.int32, (tm, 1), 0)
    ys = []
    for grp, win in enumerate(POOL_WINDOWS):
        cols = slice(grp * POOL_GW, (grp + 1) * POOL_GW)
        lo_off, hi_off = -(win // 2), win - win // 2
        s = ext_ref[POOL_HALO + lo_off:POOL_HALO + lo_off + tm, cols]
        for d in range(lo_off + 1, hi_off):
            s = s + ext_ref[POOL_HALO + d:POOL_HALO + d + tm, cols]
        cnt = (jnp.minimum(pos + hi_off, seq_len) - jnp.maximum(pos + lo_off, 0)).astype(F32)
        p = s / cnt - a_main[:, cols]
        ys.append(_dot(p.astype(BF16), w_ref[grp]))
    y = jnp.concatenate(ys, axis=-1) * sc_ref[...]
    o_ref[0] = x + gate * y


def _pool_call(h, mod, g, w, scale, tm=512):
    B, L, D = h.shape
    tm = min(tm, L)
    r = tm // POOL_HALO
    last = L // POOL_HALO - 1
    return pl.pallas_call(
        functools.partial(_pool_kernel, seq_len=L, tm=tm),
        grid=(B, L // tm),
        in_specs=[
            pl.BlockSpec((1, tm, D), lambda b, i: (b, i, 0)),
            pl.BlockSpec((1, POOL_HALO, D), lambda b, i: (b, jnp.maximum(i * r - 1, 0), 0)),
            pl.BlockSpec((1, POOL_HALO, D), lambda b, i: (b, jnp.minimum((i + 1) * r, last), 0)),
            _mod_spec(mod),
            _resident((1, D)),
            _resident(w.shape),
            _resident((1, D)),
        ],
        out_specs=pl.BlockSpec((1, tm, D), lambda b, i: (b, i, 0)),
        out_shape=jax.ShapeDtypeStruct(h.shape, F32),
        scratch_shapes=[pltpu.VMEM((tm + 2 * POOL_HALO, D), F32)],
        compiler_params=_params("parallel", "parallel"),
        name="pool_mix",
    )(h, h, h, mod, g.reshape(1, D), w, scale.reshape(1, D))


def _head_perm():
    q = HEAD_DIM // 4
    return np.concatenate([np.arange(0, q), np.arange(2 * q, 3 * q), np.arange(q, 2 * q), np.arange(3 * q, 4 * q)])


def _rope_tables(L):
    rows_n = L // GRID_W
    row = jnp.repeat(jnp.arange(rows_n), GRID_W).astype(F32)
    col = jnp.tile(jnp.arange(GRID_W), rows_n).astype(F32)
    half = HEAD_DIM // 2
    inv = ROPE_BASE ** (-jnp.arange(0, half, 2, dtype=F32) / half)
    ang_r = row[:, None] * inv[None, :]
    ang_c = col[:, None] * inv[None, :]
    cos = jnp.concatenate([jnp.cos(ang_r), jnp.cos(ang_c)], axis=-1)
    sin = jnp.concatenate([jnp.sin(ang_r), jnp.sin(ang_c)], axis=-1)
    return jnp.concatenate([cos, cos], axis=-1), jnp.concatenate([-sin, sin], axis=-1)


def _qkv_kernel(*refs, rope, want_q):
    refs = list(refs)
    h_ref, mod_ref, g_ref, w_ref, qg_ref, kg_ref = refs[:6]
    refs = refs[6:]
    if rope:
        cos_ref, sin_ref = refs[:2]
        refs = refs[2:]
    if want_q:
        q_ref = refs[0]
        refs = refs[1:]
    k_ref, vt_ref = refs

    a = _norm_mod(h_ref[0], g_ref[...], mod_ref[0, 0:1, :], mod_ref[0, 1:2, :]).astype(BF16)

    def heads(col0, n_heads, gain, out_ref):
        y = _dot(a, w_ref[:, col0:col0 + n_heads * HEAD_DIM])
        for hd in range(n_heads):
            t = _rms(y[:, hd * HEAD_DIM:(hd + 1) * HEAD_DIM], gain)
            if rope:
                t = t * cos_ref[...] + pltpu.roll(t, HEAD_DIM // 2, axis=1) * sin_ref[...]
            out_ref[0, :, hd * HEAD_DIM:(hd + 1) * HEAD_DIM] = t.astype(out_ref.dtype)

    if want_q:
        heads(0, N_HEADS, qg_ref[...], q_ref)
    heads(Q_WIDTH, N_KV_HEADS, kg_ref[...], k_ref)
    v = _dot(a, w_ref[:, Q_WIDTH + KV_WIDTH:])
    for hd in range(N_KV_HEADS):
        vt_ref[0, hd, 0:HEAD_DIM, :] = v[:, hd * HEAD_DIM:(hd + 1) * HEAD_DIM].T.astype(vt_ref.dtype)
        vt_ref[0, hd, HEAD_DIM:, :] = jnp.ones((VT_ROWS - HEAD_DIM, v.shape[0]), vt_ref.dtype)


def _qkv_call(h, mod, g, w, q_g, k_g, rope_tabs, want_q, tm=512):
    B, L, D = h.shape
    tm = min(tm, L)
    tok = lambda width: pl.BlockSpec((1, tm, width), lambda b, i: (b, i, 0))
    args = [h, mod, g.reshape(1, D), w, q_g.reshape(1, HEAD_DIM), k_g.reshape(1, HEAD_DIM)]
    specs = [tok(D), _mod_spec(mod), _resident((1, D)), _resident(w.shape),
             _resident((1, HEAD_DIM)), _resident((1, HEAD_DIM))]
    if rope_tabs is not None:
        args += list(rope_tabs)
        specs += [pl.BlockSpec((tm, HEAD_DIM), lambda b, i: (i, 0))] * 2
    out_shape, out_specs = [], []
    if want_q:
        out_shape.append(jax.ShapeDtypeStruct((B, L, Q_WIDTH), BF16))
        out_specs.append(tok(Q_WIDTH))
    out_shape += [jax.ShapeDtypeStruct((B, L, KV_WIDTH), BF16),
                  jax.ShapeDtypeStruct((B, N_KV_HEADS, VT_ROWS, L), BF16)]
    out_specs += [tok(KV_WIDTH), pl.BlockSpec((1, N_KV_HEADS, VT_ROWS, tm), lambda b, i: (b, 0, 0, i))]
    return pl.pallas_call(
        functools.partial(_qkv_kernel, rope=rope_tabs is not None, want_q=want_q),
        grid=(B, L // tm),
        in_specs=specs,
        out_specs=out_specs,
        out_shape=out_shape,
        compiler_params=_params("parallel", "parallel"),
        name="qkv_proj",
    )(*args)


def _flash_kernel(q_ref, k_ref, vt_ref, o_ref, *, tq, tk):
    q = jnp.concatenate([q_ref[0, :, hd * HEAD_DIM:(hd + 1) * HEAD_DIM] for hd in range(Q_PER_KV)], axis=0)
    c = (HEAD_DIM ** -0.5) * LOG2E
    n_kv = k_ref.shape[1] // tk
    nt = (((1,), (1,)), ((), ()))

    def finish(acc):
        o = (acc[:HEAD_DIM] / acc[HEAD_DIM:HEAD_DIM + 1]).T
        for hd in range(Q_PER_KV):
            o_ref[0, :, hd * HEAD_DIM:(hd + 1) * HEAD_DIM] = o[hd * tq:(hd + 1) * tq].astype(o_ref.dtype)

    acc = jnp.zeros((VT_ROWS, q.shape[0]), F32)
    excess = jnp.zeros((1, q.shape[0]), F32)
    for j in range(n_kv):
        st = jax.lax.dot_general(k_ref[0, j * tk:(j + 1) * tk, :], q, nt, preferred_element_type=F32)
        block_max = jnp.max(st, axis=0, keepdims=True)
        if j == 0:
            shift = block_max
        else:
            excess = jnp.maximum(excess, (block_max - shift) * c)
        p = jnp.exp2((st - shift) * c).astype(BF16)
        acc = acc + _dot(vt_ref[0, 0, :, j * tk:(j + 1) * tk], p)
        if 0 < j < n_kv - 1:
            new_shift = jnp.maximum(shift, block_max)
            acc = acc * jnp.exp2((shift - new_shift) * c)
            shift = new_shift
    finish(acc)

    @pl.when(jnp.max(excess) > FLASH_MAX_EXCESS)
    def _():
        def body(j, carry):
            m, acc = carry
            start = pl.multiple_of(j * tk, tk)
            st = jax.lax.dot_general(k_ref[0, pl.ds(start, tk), :], q, nt, preferred_element_type=F32)
            m_next = jnp.maximum(m, jnp.max(st, axis=0, keepdims=True))
            p = jnp.exp2((st - m_next) * c).astype(BF16)
            acc = acc * jnp.exp2((m - m_next) * c) + _dot(vt_ref[0, 0, :, pl.ds(start, tk)], p)
            return m_next, acc

        init = (jnp.full((1, q.shape[0]), -jnp.inf, F32), jnp.zeros((VT_ROWS, q.shape[0]), F32))
        finish(jax.lax.fori_loop(0, n_kv, body, init)[1])


def _flash_call(q, k, vt, tq=512, tk=768):
    B, L, _ = q.shape
    Lk = k.shape[1]
    return pl.pallas_call(
        functools.partial(_flash_kernel, tq=tq, tk=tk),
        grid=(B, N_KV_HEADS, L // tq),
        in_specs=[
            pl.BlockSpec((1, tq, Q_PER_KV * HEAD_DIM), lambda b, kh, i: (b, i, kh)),
            pl.BlockSpec((1, Lk, HEAD_DIM), lambda b, kh, i: (b, 0, kh)),
            pl.BlockSpec((1, 1, VT_ROWS, Lk), lambda b, kh, i: (b, kh, 0, 0)),
        ],
        out_specs=pl.BlockSpec((1, tq, Q_PER_KV * HEAD_DIM), lambda b, kh, i: (b, i, kh)),
        out_shape=jax.ShapeDtypeStruct(q.shape, BF16),
        compiler_params=_params("parallel", "parallel", "parallel"),
        name="flash_gqa",
    )(q, k, vt)


def _gmlp_kernel(h_ref, mod_ref, g_ref, win_ref, lng_ref, lnb_ref, ws_ref, bst_ref, o_ref, *, tm):
    a = _norm_mod(h_ref[0], g_ref[...], mod_ref[0, 0:1, :], mod_ref[0, 1:2, :]).astype(BF16)
    z = jax.nn.gelu(_dot(a, win_ref[...]), approximate=True)
    u, v = z[:, :GMLP_HALF], z[:, GMLP_HALF:]
    mu = jnp.mean(v, axis=-1, keepdims=True)
    vc = v - mu
    v = vc * jax.lax.rsqrt(jnp.mean(vc * vc, axis=-1, keepdims=True) + EPS) * lng_ref[...] + lnb_ref[...]
    v = v.astype(BF16)
    for ch in range(tm // CHUNK):
        rows = slice(ch * CHUNK, (ch + 1) * CHUNK)
        for grp in range(GMLP_GROUPS):
            cols = slice(grp * GMLP_GW, (grp + 1) * GMLP_GW)
            sv = _dot(ws_ref[grp], v[rows, cols]) + bst_ref[:, grp:grp + 1]
            o_ref[0, rows, cols] = (u[rows, cols] * sv).astype(o_ref.dtype)


def _gmlp_call(h, mod, g, w_in, ln_g, ln_b, ws, bs, tm=256):
    B, L, D = h.shape
    return pl.pallas_call(
        functools.partial(_gmlp_kernel, tm=tm),
        grid=(B, L // tm),
        in_specs=[
            pl.BlockSpec((1, tm, D), lambda b, i: (b, i, 0)),
            _mod_spec(mod),
            _resident((1, D)),
            _resident(w_in.shape),
            _resident((1, GMLP_HALF)),
            _resident((1, GMLP_HALF)),
            _resident(ws.shape),
            _resident((CHUNK, GMLP_GROUPS)),
        ],
        out_specs=pl.BlockSpec((1, tm, GMLP_HALF), lambda b, i: (b, i, 0)),
        out_shape=jax.ShapeDtypeStruct((B, L, GMLP_HALF), BF16),
        compiler_params=_params("parallel", "parallel"),
        name="gmlp_gate",
    )(h, mod, g.reshape(1, D), w_in, ln_g.reshape(1, GMLP_HALF), ln_b.reshape(1, GMLP_HALF), ws, bs.T)


def kernel(x, c, ctx, c_ctx, ada_w, ada_b, norm_g, mlp_w1, mlp_w2, pool_w, pool_scale, attn_w_qkv, attn_w_o,
           attn_q_g, attn_k_g, gm_w_in, gm_ln_g, gm_ln_b, gm_ws, gm_bs, gm_w_out, final_g):
    B, S, D = x.shape
    last_ctx_read = max([i for i in range(DEPTH) if i % N_MIXERS == 1], default=-1)

    pad = (-(B + 1)) % 8
    s_in = jnp.concatenate([c, c_ctx[None, :], jnp.zeros((pad, D), F32)], axis=0)
    mods = _ada_call(s_in, ada_w, ada_b)
    mod_lat = mods[:, :B].reshape(DEPTH, B, 6, D)
    mod_ctx = mods[:, B:B + 1].reshape(DEPTH, 1, 6, D)

    perm = _head_perm()
    h_lat, h_ctx = x, ctx
    for i in range(DEPTH):
        kind, j = i % N_MIXERS, i // N_MIXERS
        ctx_in = i <= last_ctx_read
        ctx_out = i < last_ctx_read
        w1, w2 = mlp_w1[i].astype(BF16), mlp_w2[i].astype(BF16)
        fin = final_g if i == DEPTH - 1 else None
        if kind == 0:
            pw = pool_w[j].astype(BF16)
            h_lat = _pool_call(h_lat, mod_lat[i], norm_g[i, 0], pw, pool_scale[j])
            h_lat = _mlp_call(h_lat, mod_lat[i], norm_g[i, 1], w1, w2, final_g=fin)
            if ctx_out:
                h_ctx = _pool_call(h_ctx, mod_ctx[i], norm_g[i, 0], pw, pool_scale[j])
                h_ctx = _mlp_call(h_ctx, mod_ctx[i], norm_g[i, 1], w1, w2)
        elif kind == 1:
            wqkv = attn_w_qkv[j]
            cols = np.concatenate([hd * HEAD_DIM + perm for hd in range(N_HEADS + N_KV_HEADS)]
                                  + [np.arange(Q_WIDTH + KV_WIDTH, wqkv.shape[1])])
            wqkv = wqkv[:, cols].astype(BF16)
            q_g, k_g = attn_q_g[j][perm], attn_k_g[j][perm]
            q, k_l, vt_l = _qkv_call(h_lat, mod_lat[i], norm_g[i, 0], wqkv, q_g, k_g, _rope_tables(S), True)
            k_c, vt_c = _qkv_call(h_ctx, mod_ctx[i], norm_g[i, 0], wqkv, q_g, k_g, None, False)
            o = _flash_call(q, jnp.concatenate([k_c, k_l], axis=1), jnp.concatenate([vt_c, vt_l], axis=-1))
            if ctx_out:
                raise NotImplementedError("context stream output of an attention layer")
            h_lat = _mlp_call(h_lat, mod_lat[i], norm_g[i, 1], w1, w2,
                              pre=(o, attn_w_o[j].astype(BF16)), final_g=fin)
        else:
            t = _gmlp_call(h_lat, mod_lat[i], norm_g[i, 0], gm_w_in[j].astype(BF16), gm_ln_g[j], gm_ln_b[j],
                           gm_ws[j].astype(BF16), gm_bs[j])
            if ctx_out:
                raise NotImplementedError("context stream output of a gMLP layer")
            h_lat = _mlp_call(h_lat, mod_lat[i], norm_g[i, 1], w1, w2,
                              pre=(t, gm_w_out[j].astype(BF16)), final_g=fin)
    return h_lat
```

```python
import functools

import jax
import jax.numpy as jnp
import numpy as np
from jax.experimental import pallas as pl
from jax.experimental.pallas import tpu as pltpu

D_MODEL = 1024
DEPTH = 4
N_MIXERS = 3
GRID_W = 64
EPS = 1e-6
POOL_WINDOWS = (2, 4, 8, 16)
POOL_GW = D_MODEL // len(POOL_WINDOWS)
POOL_HALO = 16
HEAD_DIM = 128
N_HEADS = D_MODEL // HEAD_DIM
N_KV_HEADS = N_HEADS // 2
Q_PER_KV = N_HEADS // N_KV_HEADS
Q_WIDTH = N_HEADS * HEAD_DIM
KV_WIDTH = N_KV_HEADS * HEAD_DIM
VT_ROWS = HEAD_DIM + 16
ROPE_BASE = 10000.0
CHUNK = 128
GMLP_HALF = 2 * D_MODEL
GMLP_GROUPS = 8
GMLP_GW = GMLP_HALF // GMLP_GROUPS
D_FF = 4 * D_MODEL
LOG2E = 1.4426950408889634
FLASH_MAX_EXCESS = 64.0
CAST_BLOCK_ELEMS = 1024 * 1024

VMEM_LIMIT_BYTES = 56 * 1024 * 1024
BF16 = jnp.bfloat16
F32 = jnp.float32


def _params(*semantics):
    return pltpu.CompilerParams(dimension_semantics=semantics, vmem_limit_bytes=VMEM_LIMIT_BYTES)


def _resident(shape):
    zeros = (0,) * len(shape)
    return pl.BlockSpec(shape, lambda *_: zeros, pipeline_mode=pl.Buffered(1))


def _mod_spec(mod):
    if mod.shape[0] == 1:
        return pl.BlockSpec((1, 6, D_MODEL), lambda b, i: (0, 0, 0))
    return pl.BlockSpec((1, 6, D_MODEL), lambda b, i: (b, 0, 0))


def _rms(x, g):
    return x * jax.lax.rsqrt(jnp.mean(x * x, axis=-1, keepdims=True) + EPS) * g


def _norm_mod(x, g, shift, scale):
    return _rms(x, g) * (1.0 + scale) + shift


def _dot(a, b):
    return jnp.dot(a, b, preferred_element_type=F32)


def _cast_kernel(w_ref, o_ref):
    o_ref[...] = w_ref[0].astype(o_ref.dtype)


def _cast_call(w_stack, layer):
    shape = w_stack.shape[1:]
    C = shape[-1]
    R = int(np.prod(shape[:-1]))
    br = min(R, CAST_BLOCK_ELEMS // C)
    out = pl.pallas_call(
        _cast_kernel,
        grid=(R // br,),
        in_specs=[pl.BlockSpec((1, br, C), lambda r: (layer, r, 0))],
        out_specs=pl.BlockSpec((br, C), lambda r: (r, 0)),
        out_shape=jax.ShapeDtypeStruct((R, C), BF16),
        compiler_params=_params("parallel"),
        name="cast_bf16",
    )(w_stack.reshape(w_stack.shape[0], R, C))
    return out.reshape(shape)


def _ada_kernel(s_ref, w_ref, b_ref, o_ref):
    s = s_ref[...]
    s = s * jax.nn.sigmoid(s)
    o_ref[0] = _dot(s.astype(BF16), w_ref[0].astype(BF16)) + b_ref[0]


def _ada_call(s_in, ada_w, ada_b):
    rows = s_in.shape[0]
    tn = 1536
    return pl.pallas_call(
        _ada_kernel,
        grid=(DEPTH, 6 * D_MODEL // tn),
        in_specs=[
            pl.BlockSpec((rows, D_MODEL), lambda l, j: (0, 0)),
            pl.BlockSpec((1, D_MODEL, tn), lambda l, j: (l, 0, j)),
            pl.BlockSpec((1, 1, tn), lambda l, j: (l, 0, j)),
        ],
        out_specs=pl.BlockSpec((1, rows, tn), lambda l, j: (l, 0, j)),
        out_shape=jax.ShapeDtypeStruct((DEPTH, rows, 6 * D_MODEL), F32),
        compiler_params=_params("arbitrary", "arbitrary"),
        name="ada_mod",
    )(s_in, ada_w, ada_b.reshape(DEPTH, 1, 6 * D_MODEL))


def _mlp_kernel(*refs, has_pre, has_final):
    refs = list(refs)
    h_ref, mod_ref, g_ref = refs[:3]
    refs = refs[3:]
    if has_pre:
        t_ref, wp_ref = refs[:2]
        refs = refs[2:]
    w1_ref, w2_ref = refs[:2]
    refs = refs[2:]
    if has_final:
        fg_ref = refs[0]
        refs = refs[1:]
    o_ref = refs[0]

    x = h_ref[0]
    if has_pre:
        x = x + mod_ref[0, 2:3, :] * _dot(t_ref[0], wp_ref[...])
    m = _norm_mod(x, g_ref[...], mod_ref[0, 3:4, :], mod_ref[0, 4:5, :]).astype(BF16)
    a = jnp.maximum(_dot(m, w1_ref[...]), 0.0)
    a = (a * a).astype(BF16)
    y = x + mod_ref[0, 5:6, :] * _dot(a, w2_ref[...])
    if has_final:
        y = _rms(y, fg_ref[...])
    o_ref[0] = y


def _mlp_call(h, mod, g, w1, w2, pre=None, final_g=None, tm=512):
    B, L, D = h.shape
    tm = min(tm, L)
    tok = lambda width: pl.BlockSpec((1, tm, width), lambda b, i: (b, i, 0))
    args = [h, mod, g.reshape(1, D)]
    specs = [tok(D), _mod_spec(mod), _resident((1, D))]
    if pre is not None:
        t, wp = pre
        args += [t, wp]
        specs += [tok(t.shape[-1]), _resident(wp.shape)]
    args += [w1, w2]
    specs += [_resident(w1.shape), _resident(w2.shape)]
    if final_g is not None:
        args.append(final_g.reshape(1, D))
        specs.append(_resident((1, D)))
    return pl.pallas_call(
        functools.partial(_mlp_kernel, has_pre=pre is not None, has_final=final_g is not None),
        grid=(B, L // tm),
        in_specs=specs,
        out_specs=tok(D),
        out_shape=jax.ShapeDtypeStruct(h.shape, F32),
        compiler_params=_params("parallel", "parallel"),
        name="mlp",
    )(*args)


def _pool_kernel(h_ref, hp_ref, hn_ref, mod_ref, g_ref, w_ref, sc_ref, o_ref, ext_ref, *, seq_len, tm):
    i = pl.program_id(1)
    n = pl.num_programs(1)
    g = g_ref[...]
    shift, scale, gate = mod_ref[0, 0:1, :], mod_ref[0, 1:2, :], mod_ref[0, 2:3, :]
    x = h_ref[0]
    a_main = _norm_mod(x, g, shift, scale)
    a_prev = jnp.where(i > 0, _norm_mod(hp_ref[0], g, shift, scale), 0.0)
    a_next = jnp.where(i < n - 1, _norm_mod(hn_ref[0], g, shift, scale), 0.0)
    ext_ref[0:POOL_HALO, :] = a_prev
    ext_ref[POOL_HALO:POOL_HALO + tm, :] = a_main
    ext_ref[POOL_HALO + tm:, :] = a_next

    pos = i * tm + jax.lax.broadcasted_iota(jnp.int32, (tm, 1), 0)
    ys = []
    for grp, win in enumerate(POOL_WINDOWS):
        cols = slice(grp * POOL_GW, (grp + 1) * POOL_GW)
        lo_off, hi_off = -(win // 2), win - win // 2
        s = ext_ref[POOL_HALO + lo_off:POOL_HALO + lo_off + tm, cols]
        for d in range(lo_off + 1, hi_off):
            s = s + ext_ref[POOL_HALO + d:POOL_HALO + d + tm, cols]
        cnt = (jnp.minimum(pos + hi_off, seq_len) - jnp.maximum(pos + lo_off, 0)).astype(F32)
        p = s / cnt - a_main[:, cols]
        ys.append(_dot(p.astype(BF16), w_ref[grp]))
    y = jnp.concatenate(ys, axis=-1) * sc_ref[...]
    o_ref[0] = x + gate * y


def _pool_call(h, mod, g, w, scale, tm=512):
    B, L, D = h.shape
    tm = min(tm, L)
    r = tm // POOL_HALO
    last = L // POOL_HALO - 1
    return pl.pallas_call(
        functools.partial(_pool_kernel, seq_len=L, tm=tm),
        grid=(B, L // tm),
        in_specs=[
            pl.BlockSpec((1, tm, D), lambda b, i: (b, i, 0)),
            pl.BlockSpec((1, POOL_HALO, D), lambda b, i: (b, jnp.maximum(i * r - 1, 0), 0)),
            pl.BlockSpec((1, POOL_HALO, D), lambda b, i: (b, jnp.minimum((i + 1) * r, last), 0)),
            _mod_spec(mod),
            _resident((1, D)),
            _resident(w.shape),
            _resident((1, D)),
        ],
        out_specs=pl.BlockSpec((1, tm, D), lambda b, i: (b, i, 0)),
        out_shape=jax.ShapeDtypeStruct(h.shape, F32),
        scratch_shapes=[pltpu.VMEM((tm + 2 * POOL_HALO, D), F32)],
        compiler_params=_params("parallel", "parallel"),
        name="pool_mix",
    )(h, h, h, mod, g.reshape(1, D), w, scale.reshape(1, D))


def _head_perm():
    q = HEAD_DIM // 4
    return np.concatenate([np.arange(0, q), np.arange(2 * q, 3 * q), np.arange(q, 2 * q), np.arange(3 * q, 4 * q)])


def _rope_tables(L):
    rows_n = L // GRID_W
    half = HEAD_DIM // 2
    inv = (np.float32(ROPE_BASE) ** (-np.arange(0, half, 2, dtype=np.float32) / np.float32(half))).astype(np.float32)
    ang_r = (np.arange(rows_n, dtype=np.float32)[:, None] * inv[None, :]).astype(np.float64)
    ang_c = (np.arange(GRID_W, dtype=np.float32)[:, None] * inv[None, :]).astype(np.float64)

    def expand(fr, fc):
        r = jnp.repeat(jnp.asarray(fr(ang_r), F32), GRID_W, axis=0)
        c = jnp.tile(jnp.asarray(fc(ang_c), F32), (rows_n, 1))
        return r, c

    cos_r, cos_c = expand(np.cos, np.cos)
    sin_r, sin_c = expand(np.sin, np.sin)
    return (jnp.concatenate([cos_r, cos_c, cos_r, cos_c], axis=-1),
            jnp.concatenate([-sin_r, -sin_c, sin_r, sin_c], axis=-1))


def _qkv_kernel(*refs, rope, want_q):
    refs = list(refs)
    h_ref, mod_ref, g_ref, w_ref, qg_ref, kg_ref = refs[:6]
    refs = refs[6:]
    if rope:
        cos_ref, sin_ref = refs[:2]
        refs = refs[2:]
    if want_q:
        q_ref = refs[0]
        refs = refs[1:]
    k_ref, vt_ref = refs

    a = _norm_mod(h_ref[0], g_ref[...], mod_ref[0, 0:1, :], mod_ref[0, 1:2, :]).astype(BF16)

    def heads(col0, n_heads, gain, out_ref):
        y = _dot(a, w_ref[:, col0:col0 + n_heads * HEAD_DIM])
        for hd in range(n_heads):
            t = _rms(y[:, hd * HEAD_DIM:(hd + 1) * HEAD_DIM], gain)
            if rope:
                t = t * cos_ref[...] + pltpu.roll(t, HEAD_DIM // 2, axis=1) * sin_ref[...]
            out_ref[0, :, hd * HEAD_DIM:(hd + 1) * HEAD_DIM] = t.astype(out_ref.dtype)

    if want_q:
        heads(0, N_HEADS, qg_ref[...], q_ref)
    heads(Q_WIDTH, N_KV_HEADS, kg_ref[...], k_ref)
    v = _dot(a, w_ref[:, Q_WIDTH + KV_WIDTH:])
    for hd in range(N_KV_HEADS):
        vt_ref[0, hd, 0:HEAD_DIM, :] = v[:, hd * HEAD_DIM:(hd + 1) * HEAD_DIM].T.astype(vt_ref.dtype)
        vt_ref[0, hd, HEAD_DIM:, :] = jnp.ones((VT_ROWS - HEAD_DIM, v.shape[0]), vt_ref.dtype)


def _qkv_call(h, mod, g, w, q_g, k_g, rope_tabs, want_q, tm=512):
    B, L, D = h.shape
    tm = min(tm, L)
    tok = lambda width: pl.BlockSpec((1, tm, width), lambda b, i: (b, i, 0))
    args = [h, mod, g.reshape(1, D), w, q_g.reshape(1, HEAD_DIM), k_g.reshape(1, HEAD_DIM)]
    specs = [tok(D), _mod_spec(mod), _resident((1, D)), _resident(w.shape),
             _resident((1, HEAD_DIM)), _resident((1, HEAD_DIM))]
    if rope_tabs is not None:
        args += list(rope_tabs)
        specs += [pl.BlockSpec((tm, HEAD_DIM), lambda b, i: (i, 0))] * 2
    out_shape, out_specs = [], []
    if want_q:
        out_shape.append(jax.ShapeDtypeStruct((B, L, Q_WIDTH), BF16))
        out_specs.append(tok(Q_WIDTH))
    out_shape += [jax.ShapeDtypeStruct((B, L, KV_WIDTH), BF16),
                  jax.ShapeDtypeStruct((B, N_KV_HEADS, VT_ROWS, L), BF16)]
    out_specs += [tok(KV_WIDTH), pl.BlockSpec((1, N_KV_HEADS, VT_ROWS, tm), lambda b, i: (b, 0, 0, i))]
    return pl.pallas_call(
        functools.partial(_qkv_kernel, rope=rope_tabs is not None, want_q=want_q),
        grid=(B, L // tm),
        in_specs=specs,
        out_specs=out_specs,
        out_shape=out_shape,
        compiler_params=_params("parallel", "parallel"),
        name="qkv_proj",
    )(*args)


def _flash_kernel(q_ref, k_ref, vt_ref, o_ref, *, tq, tk):
    q = jnp.concatenate([q_ref[0, :, hd * HEAD_DIM:(hd + 1) * HEAD_DIM] for hd in range(Q_PER_KV)], axis=0)
    c = (HEAD_DIM ** -0.5) * LOG2E
    n_kv = k_ref.shape[1] // tk
    nt = (((1,), (1,)), ((), ()))

    def finish(acc):
        o = (acc[:HEAD_DIM] / acc[HEAD_DIM:HEAD_DIM + 1]).T
        for hd in range(Q_PER_KV):
            o_ref[0, :, hd * HEAD_DIM:(hd + 1) * HEAD_DIM] = o[hd * tq:(hd + 1) * tq].astype(o_ref.dtype)

    acc = jnp.zeros((VT_ROWS, q.shape[0]), F32)
    excess = jnp.zeros((1, q.shape[0]), F32)
    for j in range(n_kv):
        st = jax.lax.dot_general(k_ref[0, j * tk:(j + 1) * tk, :], q, nt, preferred_element_type=F32)
        block_max = jnp.max(st, axis=0, keepdims=True)
        if j == 0:
            shift = block_max
        else:
            excess = jnp.maximum(excess, (block_max - shift) * c)
        p = jnp.exp2((st - shift) * c).astype(BF16)
        acc = acc + _dot(vt_ref[0, 0, :, j * tk:(j + 1) * tk], p)
        if 0 < j < n_kv - 1:
            new_shift = jnp.maximum(shift, block_max)
            acc = acc * jnp.exp2((shift - new_shift) * c)
            shift = new_shift
    finish(acc)

    @pl.when(jnp.max(excess) > FLASH_MAX_EXCESS)
    def _():
        def body(j, carry):
            m, acc = carry
            start = pl.multiple_of(j * tk, tk)
            st = jax.lax.dot_general(k_ref[0, pl.ds(start, tk), :], q, nt, preferred_element_type=F32)
            m_next = jnp.maximum(m, jnp.max(st, axis=0, keepdims=True))
            p = jnp.exp2((st - m_next) * c).astype(BF16)
            acc = acc * jnp.exp2((m - m_next) * c) + _dot(vt_ref[0, 0, :, pl.ds(start, tk)], p)
            return m_next, acc

        init = (jnp.full((1, q.shape[0]), -jnp.inf, F32), jnp.zeros((VT_ROWS, q.shape[0]), F32))
        finish(jax.lax.fori_loop(0, n_kv, body, init)[1])


def _flash_call(q, k, vt, tq=512, tk=768):
    B, L, _ = q.shape
    Lk = k.shape[1]
    return pl.pallas_call(
        functools.partial(_flash_kernel, tq=tq, tk=tk),
        grid=(B, N_KV_HEADS, L // tq),
        in_specs=[
            pl.BlockSpec((1, tq, Q_PER_KV * HEAD_DIM), lambda b, kh, i: (b, i, kh)),
            pl.BlockSpec((1, Lk, HEAD_DIM), lambda b, kh, i: (b, 0, kh)),
            pl.BlockSpec((1, 1, VT_ROWS, Lk), lambda b, kh, i: (b, kh, 0, 0)),
        ],
        out_specs=pl.BlockSpec((1, tq, Q_PER_KV * HEAD_DIM), lambda b, kh, i: (b, i, kh)),
        out_shape=jax.ShapeDtypeStruct(q.shape, BF16),
        compiler_params=_params("parallel", "parallel", "parallel"),
        name="flash_gqa",
    )(q, k, vt)


def _gmlp_kernel(h_ref, mod_ref, g_ref, win_ref, lng_ref, lnb_ref, ws_ref, bst_ref, o_ref, *, tm):
    a = _norm_mod(h_ref[0], g_ref[...], mod_ref[0, 0:1, :], mod_ref[0, 1:2, :]).astype(BF16)
    z = jax.nn.gelu(_dot(a, win_ref[...]), approximate=True)
    u, v = z[:, :GMLP_HALF], z[:, GMLP_HALF:]
    mu = jnp.mean(v, axis=-1, keepdims=True)
    vc = v - mu
    v = vc * jax.lax.rsqrt(jnp.mean(vc * vc, axis=-1, keepdims=True) + EPS) * lng_ref[...] + lnb_ref[...]
    v = v.astype(BF16)
    for ch in range(tm // CHUNK):
        rows = slice(ch * CHUNK, (ch + 1) * CHUNK)
        for grp in range(GMLP_GROUPS):
            cols = slice(grp * GMLP_GW, (grp + 1) * GMLP_GW)
            sv = _dot(ws_ref[grp], v[rows, cols]) + bst_ref[:, grp:grp + 1]
            o_ref[0, rows, cols] = (u[rows, cols] * sv).astype(o_ref.dtype)


def _gmlp_call(h, mod, g, w_in, ln_g, ln_b, ws, bs, tm=256):
    B, L, D = h.shape
    return pl.pallas_call(
        functools.partial(_gmlp_kernel, tm=tm),
        grid=(B, L // tm),
        in_specs=[
            pl.BlockSpec((1, tm, D), lambda b, i: (b, i, 0)),
            _mod_spec(mod),
            _resident((1, D)),
            _resident(w_in.shape),
            _resident((1, GMLP_HALF)),
            _resident((1, GMLP_HALF)),
            _resident(ws.shape),
            _resident((CHUNK, GMLP_GROUPS)),
        ],
        out_specs=pl.BlockSpec((1, tm, GMLP_HALF), lambda b, i: (b, i, 0)),
        out_shape=jax.ShapeDtypeStruct((B, L, GMLP_HALF), BF16),
        compiler_params=_params("parallel", "parallel"),
        name="gmlp_gate",
    )(h, mod, g.reshape(1, D), w_in, ln_g.reshape(1, GMLP_HALF), ln_b.reshape(1, GMLP_HALF), ws, bs.T)


def kernel(x, c, ctx, c_ctx, ada_w, ada_b, norm_g, mlp_w1, mlp_w2, pool_w, pool_scale, attn_w_qkv, attn_w_o,
           attn_q_g, attn_k_g, gm_w_in, gm_ln_g, gm_ln_b, gm_ws, gm_bs, gm_w_out, final_g):
    B, S, D = x.shape
    last_ctx_read = max([i for i in range(DEPTH) if i % N_MIXERS == 1], default=-1)

    pad = (-(B + 1)) % 8
    s_in = jnp.concatenate([c, c_ctx[None, :], jnp.zeros((pad, D), F32)], axis=0)
    mods = _ada_call(s_in, ada_w, ada_b)
    mod_lat = mods[:, :B].reshape(DEPTH, B, 6, D)
    mod_ctx = mods[:, B:B + 1].reshape(DEPTH, 1, 6, D)

    perm = _head_perm()
    h_lat, h_ctx = x, ctx
    for i in range(DEPTH):
        kind, j = i % N_MIXERS, i // N_MIXERS
        ctx_in = i <= last_ctx_read
        ctx_out = i < last_ctx_read
        w1, w2 = _cast_call(mlp_w1, i), _cast_call(mlp_w2, i)
        fin = final_g if i == DEPTH - 1 else None
        if kind == 0:
            pw = _cast_call(pool_w, j)
            h_lat = _pool_call(h_lat, mod_lat[i], norm_g[i, 0], pw, pool_scale[j])
            h_lat = _mlp_call(h_lat, mod_lat[i], norm_g[i, 1], w1, w2, final_g=fin)
            if ctx_out:
                h_ctx = _pool_call(h_ctx, mod_ctx[i], norm_g[i, 0], pw, pool_scale[j])
                h_ctx = _mlp_call(h_ctx, mod_ctx[i], norm_g[i, 1], w1, w2)
        elif kind == 1:
            cols = np.concatenate([hd * HEAD_DIM + perm for hd in range(N_HEADS + N_KV_HEADS)]
                                  + [np.arange(Q_WIDTH + KV_WIDTH, attn_w_qkv.shape[-1])])
            wqkv = _cast_call(attn_w_qkv, j)[:, cols]
            q_g, k_g = attn_q_g[j][perm], attn_k_g[j][perm]
            q, k_l, vt_l = _qkv_call(h_lat, mod_lat[i], norm_g[i, 0], wqkv, q_g, k_g, _rope_tables(S), True)
            k_c, vt_c = _qkv_call(h_ctx, mod_ctx[i], norm_g[i, 0], wqkv, q_g, k_g, None, False)
            o = _flash_call(q, jnp.concatenate([k_c, k_l], axis=1), jnp.concatenate([vt_c, vt_l], axis=-1))
            if ctx_out:
                raise NotImplementedError("context stream output of an attention layer")
            h_lat = _mlp_call(h_lat, mod_lat[i], norm_g[i, 1], w1, w2,
                              pre=(o, _cast_call(attn_w_o, j)), final_g=fin)
        else:
            t = _gmlp_call(h_lat, mod_lat[i], norm_g[i, 0], _cast_call(gm_w_in, j), gm_ln_g[j], gm_ln_b[j],
                           _cast_call(gm_ws, j), gm_bs[j])
            if ctx_out:
                raise NotImplementedError("context stream output of a gMLP layer")
            h_lat = _mlp_call(h_lat, mod_lat[i], norm_g[i, 1], w1, w2,
                              pre=(t, _cast_call(gm_w_out, j)), final_g=fin)
    return h_lat
```

```python
import functools

import jax
import jax.numpy as jnp
import numpy as np
from jax.experimental import pallas as pl
from jax.experimental.pallas import tpu as pltpu

D_MODEL = 1024
DEPTH = 4
N_MIXERS = 3
GRID_W = 64
EPS = 1e-6
POOL_WINDOWS = (2, 4, 8, 16)
POOL_GW = D_MODEL // len(POOL_WINDOWS)
POOL_HALO = 16
HEAD_DIM = 128
N_HEADS = D_MODEL // HEAD_DIM
N_KV_HEADS = N_HEADS // 2
Q_PER_KV = N_HEADS // N_KV_HEADS
Q_WIDTH = N_HEADS * HEAD_DIM
KV_WIDTH = N_KV_HEADS * HEAD_DIM
VT_ROWS = HEAD_DIM + 16
ROPE_BASE = 10000.0
CHUNK = 128
GMLP_HALF = 2 * D_MODEL
GMLP_GROUPS = 8
GMLP_GW = GMLP_HALF // GMLP_GROUPS
D_FF = 4 * D_MODEL
LOG2E = 1.4426950408889634
FLASH_MAX_EXCESS = 64.0
GMLP_SUB_ROWS = 256
QKV_SUB_ROWS = 256
CAST_BLOCK_ELEMS = 1024 * 1024

VMEM_LIMIT_BYTES = 56 * 1024 * 1024
BF16 = jnp.bfloat16
F32 = jnp.float32


def _params(*semantics):
    return pltpu.CompilerParams(dimension_semantics=semantics, vmem_limit_bytes=VMEM_LIMIT_BYTES)


def _resident(shape):
    zeros = (0,) * len(shape)
    return pl.BlockSpec(shape, lambda *_: zeros, pipeline_mode=pl.Buffered(1))


def _mod_spec(mod):
    if mod.shape[0] == 1:
        return pl.BlockSpec((1, 6, D_MODEL), lambda b, i: (0, 0, 0))
    return pl.BlockSpec((1, 6, D_MODEL), lambda b, i: (b, 0, 0))


def _rms(x, g):
    return x * jax.lax.rsqrt(jnp.mean(x * x, axis=-1, keepdims=True) + EPS) * g


def _norm_mod(x, g, shift, scale):
    return _rms(x, g) * (1.0 + scale) + shift


def _dot(a, b):
    return jnp.dot(a, b, preferred_element_type=F32)


def _cast_kernel(w_ref, o_ref):
    o_ref[...] = w_ref[0].astype(o_ref.dtype)


def _cast_call(w_stack, layer):
    shape = w_stack.shape[1:]
    C = shape[-1]
    R = int(np.prod(shape[:-1]))
    br = min(R, CAST_BLOCK_ELEMS // C)
    out = pl.pallas_call(
        _cast_kernel,
        grid=(R // br,),
        in_specs=[pl.BlockSpec((1, br, C), lambda r: (layer, r, 0))],
        out_specs=pl.BlockSpec((br, C), lambda r: (r, 0)),
        out_shape=jax.ShapeDtypeStruct((R, C), BF16),
        compiler_params=_params("parallel"),
        name="cast_bf16",
    )(w_stack.reshape(w_stack.shape[0], R, C))
    return out.reshape(shape)


def _ada_kernel(s_ref, w_ref, b_ref, o_ref):
    s = s_ref[...]
    s = s * jax.nn.sigmoid(s)
    o_ref[0] = _dot(s.astype(BF16), w_ref[0].astype(BF16)) + b_ref[0]


def _ada_call(s_in, ada_w, ada_b):
    rows = s_in.shape[0]
    tn = 1536
    return pl.pallas_call(
        _ada_kernel,
        grid=(DEPTH, 6 * D_MODEL // tn),
        in_specs=[
            pl.BlockSpec((rows, D_MODEL), lambda l, j: (0, 0)),
            pl.BlockSpec((1, D_MODEL, tn), lambda l, j: (l, 0, j)),
            pl.BlockSpec((1, 1, tn), lambda l, j: (l, 0, j)),
        ],
        out_specs=pl.BlockSpec((1, rows, tn), lambda l, j: (l, 0, j)),
        out_shape=jax.ShapeDtypeStruct((DEPTH, rows, 6 * D_MODEL), F32),
        compiler_params=_params("arbitrary", "arbitrary"),
        name="ada_mod",
    )(s_in, ada_w, ada_b.reshape(DEPTH, 1, 6 * D_MODEL))


def _mlp_kernel(*refs, has_pre, has_final):
    refs = list(refs)
    h_ref, mod_ref, g_ref = refs[:3]
    refs = refs[3:]
    if has_pre:
        t_ref, wp_ref = refs[:2]
        refs = refs[2:]
    w1_ref, w2_ref = refs[:2]
    refs = refs[2:]
    if has_final:
        fg_ref = refs[0]
        refs = refs[1:]
    o_ref = refs[0]

    x = h_ref[0]
    if has_pre:
        x = x + mod_ref[0, 2:3, :] * _dot(t_ref[0], wp_ref[...])
    m = _norm_mod(x, g_ref[...], mod_ref[0, 3:4, :], mod_ref[0, 4:5, :]).astype(BF16)
    a = jnp.maximum(_dot(m, w1_ref[...]), 0.0)
    a = (a * a).astype(BF16)
    y = x + mod_ref[0, 5:6, :] * _dot(a, w2_ref[...])
    if has_final:
        y = _rms(y, fg_ref[...])
    o_ref[0] = y


def _mlp_call(h, mod, g, w1, w2, pre=None, final_g=None, tm=512):
    B, L, D = h.shape
    tm = min(tm, L)
    tok = lambda width: pl.BlockSpec((1, tm, width), lambda b, i: (b, i, 0))
    args = [h, mod, g.reshape(1, D)]
    specs = [tok(D), _mod_spec(mod), _resident((1, D))]
    if pre is not None:
        t, wp = pre
        args += [t, wp]
        specs += [tok(t.shape[-1]), _resident(wp.shape)]
    args += [w1, w2]
    specs += [_resident(w1.shape), _resident(w2.shape)]
    if final_g is not None:
        args.append(final_g.reshape(1, D))
        specs.append(_resident((1, D)))
    return pl.pallas_call(
        functools.partial(_mlp_kernel, has_pre=pre is not None, has_final=final_g is not None),
        grid=(B, L // tm),
        in_specs=specs,
        out_specs=tok(D),
        out_shape=jax.ShapeDtypeStruct(h.shape, F32),
        compiler_params=_params("parallel", "parallel"),
        name="mlp",
    )(*args)


def _pool_kernel(h_ref, hp_ref, hn_ref, mod_ref, g_ref, w_ref, sc_ref, o_ref, *, seq_len, tm):
    i = pl.program_id(1)
    n = pl.num_programs(1)
    g = g_ref[...]
    shift, scale, gate = mod_ref[0, 0:1, :], mod_ref[0, 1:2, :], mod_ref[0, 2:3, :]
    x = h_ref[0]
    a_main = _norm_mod(x, g, shift, scale)
    a_prev = jnp.where(i > 0, _norm_mod(hp_ref[0], g, shift, scale), 0.0)
    a_next = jnp.where(i < n - 1, _norm_mod(hn_ref[0], g, shift, scale), 0.0)
    ext = jnp.concatenate([a_prev, a_main, a_next], axis=0)
    rows = ext.shape[0]

    pos = i * tm + jax.lax.broadcasted_iota(jnp.int32, (tm, 1), 0)
    ys = []
    for grp, win in enumerate(POOL_WINDOWS):
        cols = slice(grp * POOL_GW, (grp + 1) * POOL_GW)
        lo_off, hi_off = -(win // 2), win - win // 2
        s = ext[:, cols]
        if hi_off > 1:
            s = pltpu.roll(s, rows - (hi_off - 1), axis=0)
        k = 1
        while k < win:
            s = s + pltpu.roll(s, k, axis=0)
            k *= 2
        s = s[POOL_HALO:POOL_HALO + tm]
        cnt = (jnp.minimum(pos + hi_off, seq_len) - jnp.maximum(pos + lo_off, 0)).astype(F32)
        p = s / cnt - a_main[:, cols]
        ys.append(_dot(p.astype(BF16), w_ref[grp]))
    y = jnp.concatenate(ys, axis=-1) * sc_ref[...]
    o_ref[0] = x + gate * y


def _pool_call(h, mod, g, w, scale, tm=512):
    B, L, D = h.shape
    tm = min(tm, L)
    r = tm // POOL_HALO
    last = L // POOL_HALO - 1
    return pl.pallas_call(
        functools.partial(_pool_kernel, seq_len=L, tm=tm),
        grid=(B, L // tm),
        in_specs=[
            pl.BlockSpec((1, tm, D), lambda b, i: (b, i, 0)),
            pl.BlockSpec((1, POOL_HALO, D), lambda b, i: (b, jnp.maximum(i * r - 1, 0), 0)),
            pl.BlockSpec((1, POOL_HALO, D), lambda b, i: (b, jnp.minimum((i + 1) * r, last), 0)),
            _mod_spec(mod),
            _resident((1, D)),
            _resident(w.shape),
            _resident((1, D)),
        ],
        out_specs=pl.BlockSpec((1, tm, D), lambda b, i: (b, i, 0)),
        out_shape=jax.ShapeDtypeStruct(h.shape, F32),
        compiler_params=_params("parallel", "parallel"),
        name="pool_mix",
    )(h, h, h, mod, g.reshape(1, D), w, scale.reshape(1, D))


def _head_perm():
    q = HEAD_DIM // 4
    return np.concatenate([np.arange(0, q), np.arange(2 * q, 3 * q), np.arange(q, 2 * q), np.arange(3 * q, 4 * q)])


def _rope_tables(L):
    rows_n = L // GRID_W
    half = HEAD_DIM // 2
    inv = (np.float32(ROPE_BASE) ** (-np.arange(0, half, 2, dtype=np.float32) / np.float32(half))).astype(np.float32)
    ang_r = (np.arange(rows_n, dtype=np.float32)[:, None] * inv[None, :]).astype(np.float64)
    ang_c = (np.arange(GRID_W, dtype=np.float32)[:, None] * inv[None, :]).astype(np.float64)

    def expand(fr, fc):
        r = jnp.repeat(jnp.asarray(fr(ang_r), F32), GRID_W, axis=0)
        c = jnp.tile(jnp.asarray(fc(ang_c), F32), (rows_n, 1))
        return r, c

    cos_r, cos_c = expand(np.cos, np.cos)
    sin_r, sin_c = expand(np.sin, np.sin)
    return (jnp.concatenate([cos_r, cos_c, cos_r, cos_c], axis=-1),
            jnp.concatenate([-sin_r, -sin_c, sin_r, sin_c], axis=-1))


def _qkv_kernel(*refs, rope, want_q):
    refs = list(refs)
    h_ref, mod_ref, g_ref, w_ref, qg_ref, kg_ref = refs[:6]
    refs = refs[6:]
    if rope:
        cos_ref, sin_ref = refs[:2]
        refs = refs[2:]
    if want_q:
        q_ref = refs[0]
        refs = refs[1:]
    k_ref, vt_ref = refs

    tm = h_ref.shape[1]
    sub = min(tm, QKV_SUB_ROWS)
    for r0 in range(0, tm, sub):
        rows = slice(r0, r0 + sub)
        a = _norm_mod(h_ref[0, rows, :], g_ref[...], mod_ref[0, 0:1, :], mod_ref[0, 1:2, :]).astype(BF16)

        def heads(col0, n_heads, gain, out_ref):
            y = _dot(a, w_ref[:, col0:col0 + n_heads * HEAD_DIM])
            for hd in range(n_heads):
                t = _rms(y[:, hd * HEAD_DIM:(hd + 1) * HEAD_DIM], gain)
                if rope:
                    t = t * cos_ref[rows, :] + pltpu.roll(t, HEAD_DIM // 2, axis=1) * sin_ref[rows, :]
                out_ref[0, rows, hd * HEAD_DIM:(hd + 1) * HEAD_DIM] = t.astype(out_ref.dtype)

        if want_q:
            heads(0, N_HEADS, qg_ref[...], q_ref)
        heads(Q_WIDTH, N_KV_HEADS, kg_ref[...], k_ref)
        v = _dot(a, w_ref[:, Q_WIDTH + KV_WIDTH:])
        for hd in range(N_KV_HEADS):
            vt_ref[0, hd, 0:HEAD_DIM, rows] = v[:, hd * HEAD_DIM:(hd + 1) * HEAD_DIM].T.astype(vt_ref.dtype)
            vt_ref[0, hd, HEAD_DIM:, rows] = jnp.ones((VT_ROWS - HEAD_DIM, sub), vt_ref.dtype)


def _qkv_call(h, mod, g, w, q_g, k_g, rope_tabs, want_q, tm=1024):
    B, L, D = h.shape
    tm = min(tm, L)
    tok = lambda width: pl.BlockSpec((1, tm, width), lambda b, i: (b, i, 0))
    args = [h, mod, g.reshape(1, D), w, q_g.reshape(1, HEAD_DIM), k_g.reshape(1, HEAD_DIM)]
    specs = [tok(D), _mod_spec(mod), _resident((1, D)), _resident(w.shape),
             _resident((1, HEAD_DIM)), _resident((1, HEAD_DIM))]
    if rope_tabs is not None:
        args += list(rope_tabs)
        specs += [pl.BlockSpec((tm, HEAD_DIM), lambda b, i: (i, 0))] * 2
    out_shape, out_specs = [], []
    if want_q:
        out_shape.append(jax.ShapeDtypeStruct((B, L, Q_WIDTH), BF16))
        out_specs.append(tok(Q_WIDTH))
    out_shape += [jax.ShapeDtypeStruct((B, L, KV_WIDTH), BF16),
                  jax.ShapeDtypeStruct((B, N_KV_HEADS, VT_ROWS, L), BF16)]
    out_specs += [tok(KV_WIDTH), pl.BlockSpec((1, N_KV_HEADS, VT_ROWS, tm), lambda b, i: (b, 0, 0, i))]
    return pl.pallas_call(
        functools.partial(_qkv_kernel, rope=rope_tabs is not None, want_q=want_q),
        grid=(B, L // tm),
        in_specs=specs,
        out_specs=out_specs,
        out_shape=out_shape,
        compiler_params=_params("parallel", "parallel"),
        name="qkv_proj",
    )(*args)


def _flash_kernel(*refs, tq, tk):
    q_ref, o_ref = refs[0], refs[-1]
    sources = [(refs[1 + 2 * s], refs[2 + 2 * s]) for s in range((len(refs) - 2) // 2)]
    q = jnp.concatenate([q_ref[0, :, hd * HEAD_DIM:(hd + 1) * HEAD_DIM] for hd in range(Q_PER_KV)], axis=0)
    c = (HEAD_DIM ** -0.5) * LOG2E
    nt = (((1,), (1,)), ((), ()))

    def block_size(k_ref):
        return min(tk, k_ref.shape[1])

    def finish(acc):
        o = (acc[:HEAD_DIM] / acc[HEAD_DIM:HEAD_DIM + 1]).T
        for hd in range(Q_PER_KV):
            o_ref[0, :, hd * HEAD_DIM:(hd + 1) * HEAD_DIM] = o[hd * tq:(hd + 1) * tq].astype(o_ref.dtype)

    blocks = [(k_ref, vt_ref, start, block_size(k_ref)) for k_ref, vt_ref in sources
              for start in range(0, k_ref.shape[1], block_size(k_ref))]
    acc = jnp.zeros((VT_ROWS, q.shape[0]), F32)
    excess = jnp.zeros((1, q.shape[0]), F32)
    for j, (k_ref, vt_ref, start, size) in enumerate(blocks):
        st = jax.lax.dot_general(k_ref[0, start:start + size, :], q, nt, preferred_element_type=F32)
        block_max = jnp.max(st, axis=0, keepdims=True)
        if j == 0:
            shift = block_max
        else:
            excess = jnp.maximum(excess, (block_max - shift) * c)
        p = jnp.exp2((st - shift) * c).astype(BF16)
        acc = acc + _dot(vt_ref[0, 0, :, start:start + size], p)
        if 0 < j < len(blocks) - 1:
            new_shift = jnp.maximum(shift, block_max)
            acc = acc * jnp.exp2((shift - new_shift) * c)
            shift = new_shift
    finish(acc)

    @pl.when(jnp.max(excess) > FLASH_MAX_EXCESS)
    def _():
        carry = (jnp.full((1, q.shape[0]), -jnp.inf, F32), jnp.zeros((VT_ROWS, q.shape[0]), F32))
        for k_ref, vt_ref in sources:
            size = block_size(k_ref)

            def body(j, carry, k_ref=k_ref, vt_ref=vt_ref, size=size):
                m, acc = carry
                start = pl.multiple_of(j * size, size)
                st = jax.lax.dot_general(k_ref[0, pl.ds(start, size), :], q, nt, preferred_element_type=F32)
                m_next = jnp.maximum(m, jnp.max(st, axis=0, keepdims=True))
                p = jnp.exp2((st - m_next) * c).astype(BF16)
                acc = acc * jnp.exp2((m - m_next) * c) + _dot(vt_ref[0, 0, :, pl.ds(start, size)], p)
                return m_next, acc

            carry = jax.lax.fori_loop(0, k_ref.shape[1] // size, body, carry)
        finish(carry[1])


def _flash_call(q, kv_sources, tq=512, tk=1024):
    B, L, _ = q.shape
    args, specs = [q], [pl.BlockSpec((1, tq, Q_PER_KV * HEAD_DIM), lambda b, kh, i: (b, i, kh))]
    for k, vt in kv_sources:
        Lk = k.shape[1]
        assert Lk % min(tk, Lk) == 0
        args += [k, vt]
        specs += [pl.BlockSpec((1, Lk, HEAD_DIM), lambda b, kh, i: (b, 0, kh)),
                  pl.BlockSpec((1, 1, VT_ROWS, Lk), lambda b, kh, i: (b, kh, 0, 0))]
    return pl.pallas_call(
        functools.partial(_flash_kernel, tq=tq, tk=tk),
        grid=(B, N_KV_HEADS, L // tq),
        in_specs=specs,
        out_specs=pl.BlockSpec((1, tq, Q_PER_KV * HEAD_DIM), lambda b, kh, i: (b, i, kh)),
        out_shape=jax.ShapeDtypeStruct(q.shape, BF16),
        compiler_params=_params("parallel", "parallel", "parallel"),
        name="flash_gqa",
    )(*args)


def _gmlp_kernel(h_ref, mod_ref, g_ref, win_ref, lng_ref, lnb_ref, ws_ref, bst_ref, o_ref, *, tm):
    sub = min(tm, GMLP_SUB_ROWS)
    for r0 in range(0, tm, sub):
        a = _norm_mod(h_ref[0, r0:r0 + sub, :], g_ref[...], mod_ref[0, 0:1, :], mod_ref[0, 1:2, :]).astype(BF16)
        z = jax.nn.gelu(_dot(a, win_ref[...]), approximate=True)
        u, v = z[:, :GMLP_HALF], z[:, GMLP_HALF:]
        mu = jnp.mean(v, axis=-1, keepdims=True)
        vc = v - mu
        v = vc * jax.lax.rsqrt(jnp.mean(vc * vc, axis=-1, keepdims=True) + EPS) * lng_ref[...] + lnb_ref[...]
        v = v.astype(BF16)
        for ch in range(sub // CHUNK):
            rows = slice(ch * CHUNK, (ch + 1) * CHUNK)
            out_rows = slice(r0 + ch * CHUNK, r0 + (ch + 1) * CHUNK)
            for grp in range(GMLP_GROUPS):
                cols = slice(grp * GMLP_GW, (grp + 1) * GMLP_GW)
                sv = _dot(ws_ref[grp], v[rows, cols]) + bst_ref[:, grp:grp + 1]
                o_ref[0, out_rows, cols] = (u[rows, cols] * sv).astype(o_ref.dtype)


def _gmlp_call(h, mod, g, w_in, ln_g, ln_b, ws, bs, tm=512):
    B, L, D = h.shape
    return pl.pallas_call(
        functools.partial(_gmlp_kernel, tm=tm),
        grid=(B, L // tm),
        in_specs=[
            pl.BlockSpec((1, tm, D), lambda b, i: (b, i, 0)),
            _mod_spec(mod),
            _resident((1, D)),
            _resident(w_in.shape),
            _resident((1, GMLP_HALF)),
            _resident((1, GMLP_HALF)),
            _resident(ws.shape),
            _resident((CHUNK, GMLP_GROUPS)),
        ],
        out_specs=pl.BlockSpec((1, tm, GMLP_HALF), lambda b, i: (b, i, 0)),
        out_shape=jax.ShapeDtypeStruct((B, L, GMLP_HALF), BF16),
        compiler_params=_params("parallel", "parallel"),
        name="gmlp_gate",
    )(h, mod, g.reshape(1, D), w_in, ln_g.reshape(1, GMLP_HALF), ln_b.reshape(1, GMLP_HALF), ws, bs.T)


def kernel(x, c, ctx, c_ctx, ada_w, ada_b, norm_g, mlp_w1, mlp_w2, pool_w, pool_scale, attn_w_qkv, attn_w_o,
           attn_q_g, attn_k_g, gm_w_in, gm_ln_g, gm_ln_b, gm_ws, gm_bs, gm_w_out, final_g):
    B, S, D = x.shape
    last_ctx_read = max([i for i in range(DEPTH) if i % N_MIXERS == 1], default=-1)

    pad = (-(B + 1)) % 8
    s_in = jnp.concatenate([c, c_ctx[None, :], jnp.zeros((pad, D), F32)], axis=0)
    mods = _ada_call(s_in, ada_w, ada_b)
    mod_lat = mods[:, :B].reshape(DEPTH, B, 6, D)
    mod_ctx = mods[:, B:B + 1].reshape(DEPTH, 1, 6, D)

    perm = _head_perm()
    h_lat, h_ctx = x, ctx
    for i in range(DEPTH):
        kind, j = i % N_MIXERS, i // N_MIXERS
        ctx_in = i <= last_ctx_read
        ctx_out = i < last_ctx_read
        w1, w2 = _cast_call(mlp_w1, i), _cast_call(mlp_w2, i)
        fin = final_g if i == DEPTH - 1 else None
        if kind == 0:
            pw = _cast_call(pool_w, j)
            h_lat = _pool_call(h_lat, mod_lat[i], norm_g[i, 0], pw, pool_scale[j])
            h_lat = _mlp_call(h_lat, mod_lat[i], norm_g[i, 1], w1, w2, final_g=fin)
            if ctx_out:
                h_ctx = _pool_call(h_ctx, mod_ctx[i], norm_g[i, 0], pw, pool_scale[j])
                h_ctx = _mlp_call(h_ctx, mod_ctx[i], norm_g[i, 1], w1, w2)
        elif kind == 1:
            cols = np.concatenate([hd * HEAD_DIM + perm for hd in range(N_HEADS + N_KV_HEADS)]
                                  + [np.arange(Q_WIDTH + KV_WIDTH, attn_w_qkv.shape[-1])])
            wqkv = _cast_call(attn_w_qkv, j)[:, cols]
            q_g, k_g = attn_q_g[j][perm], attn_k_g[j][perm]
            q, k_l, vt_l = _qkv_call(h_lat, mod_lat[i], norm_g[i, 0], wqkv, q_g, k_g, _rope_tables(S), True)
            k_c, vt_c = _qkv_call(h_ctx, mod_ctx[i], norm_g[i, 0], wqkv, q_g, k_g, None, False)
            o = _flash_call(q, [(k_l, vt_l), (k_c, vt_c)])
            if ctx_out:
                raise NotImplementedError("context stream output of an attention layer")
            h_lat = _mlp_call(h_lat, mod_lat[i], norm_g[i, 1], w1, w2,
                              pre=(o, _cast_call(attn_w_o, j)), final_g=fin)
        else:
            t = _gmlp_call(h_lat, mod_lat[i], norm_g[i, 0], _cast_call(gm_w_in, j), gm_ln_g[j], gm_ln_b[j],
                           _cast_call(gm_ws, j), gm_bs[j])
            if ctx_out:
                raise NotImplementedError("context stream output of a gMLP layer")
            h_lat = _mlp_call(h_lat, mod_lat[i], norm_g[i, 1], w1, w2,
                              pre=(t, _cast_call(gm_w_out, j)), final_g=fin)
    return h_lat
```

```python
import functools

import jax
import jax.numpy as jnp
import numpy as np
from jax.experimental import pallas as pl
from jax.experimental.pallas import tpu as pltpu

D_MODEL = 1024
DEPTH = 4
N_MIXERS = 3
GRID_W = 64
EPS = 1e-6
POOL_WINDOWS = (2, 4, 8, 16)
POOL_GW = D_MODEL // len(POOL_WINDOWS)
POOL_HALO = 16
HEAD_DIM = 128
N_HEADS = D_MODEL // HEAD_DIM
N_KV_HEADS = N_HEADS // 2
Q_PER_KV = N_HEADS // N_KV_HEADS
Q_WIDTH = N_HEADS * HEAD_DIM
KV_WIDTH = N_KV_HEADS * HEAD_DIM
VT_ROWS = HEAD_DIM + 16
ROPE_BASE = 10000.0
CHUNK = 128
GMLP_HALF = 2 * D_MODEL
GMLP_GROUPS = 8
GMLP_GW = GMLP_HALF // GMLP_GROUPS
D_FF = 4 * D_MODEL
LOG2E = 1.4426950408889634
FLASH_MAX_EXCESS = 64.0
GMLP_SUB_ROWS = 256
QKV_SUB_ROWS = 256
CAST_BLOCK_ELEMS = 1024 * 1024

VMEM_LIMIT_BYTES = 56 * 1024 * 1024
BF16 = jnp.bfloat16
F32 = jnp.float32


def _params(*semantics):
    return pltpu.CompilerParams(dimension_semantics=semantics, vmem_limit_bytes=VMEM_LIMIT_BYTES)


def _resident(shape):
    zeros = (0,) * len(shape)
    return pl.BlockSpec(shape, lambda *_: zeros, pipeline_mode=pl.Buffered(1))


def _mod_spec(mod):
    if mod.shape[0] == 1:
        return pl.BlockSpec((1, 6, D_MODEL), lambda b, i: (0, 0, 0))
    return pl.BlockSpec((1, 6, D_MODEL), lambda b, i: (b, 0, 0))


def _rms(x, g):
    return x * jax.lax.rsqrt(jnp.mean(x * x, axis=-1, keepdims=True) + EPS) * g


def _norm_mod(x, g, shift, scale):
    return _rms(x, g) * (1.0 + scale) + shift


def _dot(a, b):
    return jnp.dot(a, b, preferred_element_type=F32)


def _cast_kernel(w_ref, o_ref):
    o_ref[...] = w_ref[0].astype(o_ref.dtype)


def _cast_call(w_stack, layer):
    shape = w_stack.shape[1:]
    C = shape[-1]
    R = int(np.prod(shape[:-1]))
    br = min(R, CAST_BLOCK_ELEMS // C)
    out = pl.pallas_call(
        _cast_kernel,
        grid=(R // br,),
        in_specs=[pl.BlockSpec((1, br, C), lambda r: (layer, r, 0))],
        out_specs=pl.BlockSpec((br, C), lambda r: (r, 0)),
        out_shape=jax.ShapeDtypeStruct((R, C), BF16),
        compiler_params=_params("parallel"),
        name="cast_bf16",
    )(w_stack.reshape(w_stack.shape[0], R, C))
    return out.reshape(shape)


def _ada_kernel(s_ref, w_ref, b_ref, o_ref):
    s = s_ref[...]
    s = s * jax.nn.sigmoid(s)
    o_ref[0] = _dot(s.astype(BF16), w_ref[0].astype(BF16)) + b_ref[0]


def _ada_call(s_in, ada_w, ada_b):
    rows = s_in.shape[0]
    tn = 1536
    return pl.pallas_call(
        _ada_kernel,
        grid=(DEPTH, 6 * D_MODEL // tn),
        in_specs=[
            pl.BlockSpec((rows, D_MODEL), lambda l, j: (0, 0)),
            pl.BlockSpec((1, D_MODEL, tn), lambda l, j: (l, 0, j)),
            pl.BlockSpec((1, 1, tn), lambda l, j: (l, 0, j)),
        ],
        out_specs=pl.BlockSpec((1, rows, tn), lambda l, j: (l, 0, j)),
        out_shape=jax.ShapeDtypeStruct((DEPTH, rows, 6 * D_MODEL), F32),
        compiler_params=_params("arbitrary", "arbitrary"),
        name="ada_mod",
    )(s_in, ada_w, ada_b.reshape(DEPTH, 1, 6 * D_MODEL))


def _pool_window(a_ext, a_main, pos, seq_len, grp):
    n_ext, n_main = a_ext.shape[0], a_main.shape[0]
    win = POOL_WINDOWS[grp]
    cols = slice(grp * POOL_GW, (grp + 1) * POOL_GW)
    lo_off, hi_off = -(win // 2), win - win // 2
    s = a_ext[:, cols]
    if hi_off > 1:
        s = pltpu.roll(s, n_ext - (hi_off - 1), axis=0)
    k = 1
    while k < win:
        s = s + pltpu.roll(s, k, axis=0)
        k *= 2
    s = s[POOL_HALO:POOL_HALO + n_main]
    cnt = (jnp.minimum(pos + hi_off, seq_len) - jnp.maximum(pos + lo_off, 0)).astype(F32)
    return (s / cnt - a_main[:, cols]).astype(BF16)


def _mlp_rows(x, mod_ref, g_ref, w1_ref, w2_ref, fg_ref):
    m = _norm_mod(x, g_ref[...], mod_ref[0, 3:4, :], mod_ref[0, 4:5, :]).astype(BF16)
    a = jnp.maximum(_dot(m, w1_ref[...]), 0.0)
    a = (a * a).astype(BF16)
    y = x + mod_ref[0, 5:6, :] * _dot(a, w2_ref[...])
    return y if fg_ref is None else _rms(y, fg_ref[...])


def _mlp_kernel(*refs, has_proj, has_final):
    refs = list(refs)
    h_ref, mod_ref, g_ref = refs[:3]
    refs = refs[3:]
    if has_proj:
        t_ref, wp_ref = refs[:2]
        refs = refs[2:]
    w1_ref, w2_ref = refs[:2]
    fg_ref = refs[2] if has_final else None
    o_ref = refs[-1]

    x = h_ref[0]
    if has_proj:
        x = x + mod_ref[0, 2:3, :] * _dot(t_ref[0], wp_ref[...])
    o_ref[0] = _mlp_rows(x, mod_ref, g_ref, w1_ref, w2_ref, fg_ref)


def _mlp_call(h, mod, g, w1, w2, proj=None, final_g=None, tm=512):
    B, L, D = h.shape
    tm = min(tm, L)
    tok = lambda width: pl.BlockSpec((1, tm, width), lambda b, i: (b, i, 0))
    args = [h, mod, g.reshape(1, D)]
    specs = [tok(D), _mod_spec(mod), _resident((1, D))]
    if proj is not None:
        t, wp = proj
        args += [t, wp]
        specs += [tok(t.shape[-1]), _resident(wp.shape)]
    args += [w1, w2]
    specs += [_resident(w1.shape), _resident(w2.shape)]
    if final_g is not None:
        args.append(final_g.reshape(1, D))
        specs.append(_resident((1, D)))
    return pl.pallas_call(
        functools.partial(_mlp_kernel, has_proj=proj is not None, has_final=final_g is not None),
        grid=(B, L // tm),
        in_specs=specs,
        out_specs=tok(D),
        out_shape=jax.ShapeDtypeStruct(h.shape, F32),
        compiler_params=_params("parallel", "parallel"),
        name="mlp",
    )(*args)


def _pool_mlp_kernel(*refs, has_final, seq_len, tiles_per_seq, n_tiles):
    refs = list(refs)
    h_ref, hp_ref, hn_ref, moda_ref, g0_ref, pw_ref, psc_ref, modb_ref, g_ref, w1_ref, w2_ref = refs[:11]
    fg_ref = refs[11] if has_final else None
    o_ref, x1_ref = refs[-2:]
    s = pl.program_id(0)
    tm = h_ref.shape[1]

    @pl.when(s == 0)
    def _():
        x1_ref[1] = jnp.zeros(x1_ref.shape[1:], F32)

    i = jnp.minimum(s, n_tiles - 1) % tiles_per_seq

    def stages(write_slot, read_slot):
        xb = x1_ref[read_slot]
        m = _norm_mod(xb, g_ref[...], modb_ref[0, 3:4, :], modb_ref[0, 4:5, :]).astype(BF16)
        norm = functools.partial(_norm_mod, g=g0_ref[...], shift=moda_ref[0, 0:1, :], scale=moda_ref[0, 1:2, :])
        x = h_ref[0]
        a_main = norm(x)
        top = jnp.where(i > 0, norm(hp_ref[0]), 0.0)
        bottom = jnp.where(i < tiles_per_seq - 1, norm(hn_ref[0]), 0.0)
        pos = i * tm + jax.lax.broadcasted_iota(jnp.int32, (tm, 1), 0)
        a_ext = jnp.concatenate([top, a_main, bottom], axis=0)

        n_grp = len(POOL_WINDOWS)
        fc = D_FF // n_grp
        acc, ys = None, []
        for grp in range(n_grp):
            p = _pool_window(a_ext, a_main, pos, seq_len, grp)
            a = jnp.maximum(_dot(m, w1_ref[:, grp * fc:(grp + 1) * fc]), 0.0)
            d = _dot((a * a).astype(BF16), w2_ref[grp * fc:(grp + 1) * fc, :])
            acc = d if acc is None else acc + d
            ys.append(_dot(p, pw_ref[grp]))
        y = xb + modb_ref[0, 5:6, :] * acc
        o_ref[0] = y if fg_ref is None else _rms(y, fg_ref[...])
        x1_ref[write_slot] = x + moda_ref[0, 2:3, :] * (jnp.concatenate(ys, axis=-1) * psc_ref[...])

    pl.when(s % 2 == 0)(functools.partial(stages, 0, 1))
    pl.when(s % 2 == 1)(functools.partial(stages, 1, 0))


def _pool_mlp_call(h, mod, g0, pool_w, pool_scale, g, w1, w2, final_g=None, tm=512):
    B, L, D = h.shape
    tm = min(tm, L)
    tps = L // tm
    n_tiles = B * tps
    r = tm // POOL_HALO
    last = L // POOL_HALO - 1
    shared_mod = mod.shape[0] == 1

    def tile_a(s):
        t = jnp.minimum(s, n_tiles - 1)
        return t // tps, t % tps

    def tile_b(s):
        t = jnp.maximum(s - 1, 0)
        return t // tps, t % tps

    def mod_spec(tile):
        return pl.BlockSpec((1, 6, D), lambda s: (0 if shared_mod else tile(s)[0], 0, 0))

    args = [h, h, h, mod, g0.reshape(1, D), pool_w, pool_scale.reshape(1, D), mod, g.reshape(1, D), w1, w2]
    specs = [
        pl.BlockSpec((1, tm, D), lambda s: (*tile_a(s), 0)),
        pl.BlockSpec((1, POOL_HALO, D), lambda s: (tile_a(s)[0], jnp.maximum(tile_a(s)[1] * r - 1, 0), 0)),
        pl.BlockSpec((1, POOL_HALO, D), lambda s: (tile_a(s)[0], jnp.minimum((tile_a(s)[1] + 1) * r, last), 0)),
        mod_spec(tile_a), _resident((1, D)), _resident(pool_w.shape), _resident((1, D)),
        mod_spec(tile_b), _resident((1, D)), _resident(w1.shape), _resident(w2.shape),
    ]
    if final_g is not None:
        args.append(final_g.reshape(1, D))
        specs.append(_resident((1, D)))
    return pl.pallas_call(
        functools.partial(_pool_mlp_kernel, has_final=final_g is not None, seq_len=L, tiles_per_seq=tps,
                          n_tiles=n_tiles),
        grid=(n_tiles + 1,),
        in_specs=specs,
        out_specs=pl.BlockSpec((1, tm, D), lambda s: (*tile_b(s), 0)),
        out_shape=jax.ShapeDtypeStruct(h.shape, F32),
        scratch_shapes=[pltpu.VMEM((2, tm, D), F32)],
        compiler_params=_params("arbitrary"),
        name="pool_mlp",
    )(*args)


def _head_perm():
    q = HEAD_DIM // 4
    return np.concatenate([np.arange(0, q), np.arange(2 * q, 3 * q), np.arange(q, 2 * q), np.arange(3 * q, 4 * q)])


def _rope_tables(L):
    rows_n = L // GRID_W
    half = HEAD_DIM // 2
    inv = (np.float32(ROPE_BASE) ** (-np.arange(0, half, 2, dtype=np.float32) / np.float32(half))).astype(np.float32)
    ang_r = (np.arange(rows_n, dtype=np.float32)[:, None] * inv[None, :]).astype(np.float64)
    ang_c = (np.arange(GRID_W, dtype=np.float32)[:, None] * inv[None, :]).astype(np.float64)

    def expand(fr, fc):
        r = jnp.repeat(jnp.asarray(fr(ang_r), F32), GRID_W, axis=0)
        c = jnp.tile(jnp.asarray(fc(ang_c), F32), (rows_n, 1))
        return r, c

    cos_r, cos_c = expand(np.cos, np.cos)
    sin_r, sin_c = expand(np.sin, np.sin)
    return (jnp.concatenate([cos_r, cos_c, cos_r, cos_c], axis=-1),
            jnp.concatenate([-sin_r, -sin_c, sin_r, sin_c], axis=-1))


def _qkv_kernel(*refs, rope, want_q):
    refs = list(refs)
    h_ref, mod_ref, g_ref, w_ref, qg_ref, kg_ref = refs[:6]
    refs = refs[6:]
    if rope:
        cos_ref, sin_ref = refs[:2]
        refs = refs[2:]
    if want_q:
        q_ref = refs[0]
        refs = refs[1:]
    k_ref, vt_ref = refs

    tm = h_ref.shape[1]
    sub = min(tm, QKV_SUB_ROWS)
    for r0 in range(0, tm, sub):
        rows = slice(r0, r0 + sub)
        a = _norm_mod(h_ref[0, rows, :], g_ref[...], mod_ref[0, 0:1, :], mod_ref[0, 1:2, :]).astype(BF16)

        def heads(col0, n_heads, gain, out_ref):
            y = _dot(a, w_ref[:, col0:col0 + n_heads * HEAD_DIM])
            for hd in range(n_heads):
                t = _rms(y[:, hd * HEAD_DIM:(hd + 1) * HEAD_DIM], gain)
                if rope:
                    t = t * cos_ref[rows, :] + pltpu.roll(t, HEAD_DIM // 2, axis=1) * sin_ref[rows, :]
                out_ref[0, rows, hd * HEAD_DIM:(hd + 1) * HEAD_DIM] = t.astype(out_ref.dtype)

        if want_q:
            heads(0, N_HEADS, qg_ref[...], q_ref)
        heads(Q_WIDTH, N_KV_HEADS, kg_ref[...], k_ref)
        v = _dot(a, w_ref[:, Q_WIDTH + KV_WIDTH:])
        for hd in range(N_KV_HEADS):
            vt_ref[0, hd, 0:HEAD_DIM, rows] = v[:, hd * HEAD_DIM:(hd + 1) * HEAD_DIM].T.astype(vt_ref.dtype)
            vt_ref[0, hd, HEAD_DIM:, rows] = jnp.ones((VT_ROWS - HEAD_DIM, sub), vt_ref.dtype)


def _qkv_call(h, mod, g, w, q_g, k_g, rope_tabs, want_q, tm=1024):
    B, L, D = h.shape
    tm = min(tm, L)
    tok = lambda width: pl.BlockSpec((1, tm, width), lambda b, i: (b, i, 0))
    args = [h, mod, g.reshape(1, D), w, q_g.reshape(1, HEAD_DIM), k_g.reshape(1, HEAD_DIM)]
    specs = [tok(D), _mod_spec(mod), _resident((1, D)), _resident(w.shape),
             _resident((1, HEAD_DIM)), _resident((1, HEAD_DIM))]
    if rope_tabs is not None:
        args += list(rope_tabs)
        specs += [pl.BlockSpec((tm, HEAD_DIM), lambda b, i: (i, 0))] * 2
    out_shape, out_specs = [], []
    if want_q:
        out_shape.append(jax.ShapeDtypeStruct((B, L, Q_WIDTH), BF16))
        out_specs.append(tok(Q_WIDTH))
    out_shape += [jax.ShapeDtypeStruct((B, L, KV_WIDTH), BF16),
                  jax.ShapeDtypeStruct((B, N_KV_HEADS, VT_ROWS, L), BF16)]
    out_specs += [tok(KV_WIDTH), pl.BlockSpec((1, N_KV_HEADS, VT_ROWS, tm), lambda b, i: (b, 0, 0, i))]
    return pl.pallas_call(
        functools.partial(_qkv_kernel, rope=rope_tabs is not None, want_q=want_q),
        grid=(B, L // tm),
        in_specs=specs,
        out_specs=out_specs,
        out_shape=out_shape,
        compiler_params=_params("parallel", "parallel"),
        name="qkv_proj",
    )(*args)


def _flash_kernel(*refs, tq, tk):
    q_ref, o_ref = refs[0], refs[-1]
    sources = [(refs[1 + 2 * s], refs[2 + 2 * s]) for s in range((len(refs) - 2) // 2)]
    q = jnp.concatenate([q_ref[0, :, hd * HEAD_DIM:(hd + 1) * HEAD_DIM] for hd in range(Q_PER_KV)], axis=0)
    c = (HEAD_DIM ** -0.5) * LOG2E
    nt = (((1,), (1,)), ((), ()))

    def block_size(k_ref):
        return min(tk, k_ref.shape[1])

    def finish(acc):
        o = (acc[:HEAD_DIM] / acc[HEAD_DIM:HEAD_DIM + 1]).T
        for hd in range(Q_PER_KV):
            o_ref[0, :, hd * HEAD_DIM:(hd + 1) * HEAD_DIM] = o[hd * tq:(hd + 1) * tq].astype(o_ref.dtype)

    blocks = [(k_ref, vt_ref, start, block_size(k_ref)) for k_ref, vt_ref in sources
              for start in range(0, k_ref.shape[1], block_size(k_ref))]
    acc = jnp.zeros((VT_ROWS, q.shape[0]), F32)
    excess = jnp.zeros((1, q.shape[0]), F32)
    for j, (k_ref, vt_ref, start, size) in enumerate(blocks):
        st = jax.lax.dot_general(k_ref[0, start:start + size, :], q, nt, preferred_element_type=F32)
        block_max = jnp.max(st, axis=0, keepdims=True)
        if j == 0:
            shift = block_max
        else:
            excess = jnp.maximum(excess, (block_max - shift) * c)
        p = jnp.exp2((st - shift) * c).astype(BF16)
        acc = acc + _dot(vt_ref[0, 0, :, start:start + size], p)
        if 0 < j < len(blocks) - 1:
            new_shift = jnp.maximum(shift, block_max)
            acc = acc * jnp.exp2((shift - new_shift) * c)
            shift = new_shift
    finish(acc)

    @pl.when(jnp.max(excess) > FLASH_MAX_EXCESS)
    def _():
        carry = (jnp.full((1, q.shape[0]), -jnp.inf, F32), jnp.zeros((VT_ROWS, q.shape[0]), F32))
        for k_ref, vt_ref in sources:
            size = block_size(k_ref)

            def body(j, carry, k_ref=k_ref, vt_ref=vt_ref, size=size):
                m, acc = carry
                start = pl.multiple_of(j * size, size)
                st = jax.lax.dot_general(k_ref[0, pl.ds(start, size), :], q, nt, preferred_element_type=F32)
                m_next = jnp.maximum(m, jnp.max(st, axis=0, keepdims=True))
                p = jnp.exp2((st - m_next) * c).astype(BF16)
                acc = acc * jnp.exp2((m - m_next) * c) + _dot(vt_ref[0, 0, :, pl.ds(start, size)], p)
                return m_next, acc

            carry = jax.lax.fori_loop(0, k_ref.shape[1] // size, body, carry)
        finish(carry[1])


def _flash_call(q, kv_sources, tq=512, tk=1024):
    B, L, _ = q.shape
    args, specs = [q], [pl.BlockSpec((1, tq, Q_PER_KV * HEAD_DIM), lambda b, kh, i: (b, i, kh))]
    for k, vt in kv_sources:
        Lk = k.shape[1]
        assert Lk % min(tk, Lk) == 0
        args += [k, vt]
        specs += [pl.BlockSpec((1, Lk, HEAD_DIM), lambda b, kh, i: (b, 0, kh)),
                  pl.BlockSpec((1, 1, VT_ROWS, Lk), lambda b, kh, i: (b, kh, 0, 0))]
    return pl.pallas_call(
        functools.partial(_flash_kernel, tq=tq, tk=tk),
        grid=(B, N_KV_HEADS, L // tq),
        in_specs=specs,
        out_specs=pl.BlockSpec((1, tq, Q_PER_KV * HEAD_DIM), lambda b, kh, i: (b, i, kh)),
        out_shape=jax.ShapeDtypeStruct(q.shape, BF16),
        compiler_params=_params("parallel", "parallel", "parallel"),
        name="flash_gqa",
    )(*args)


def _gmlp_kernel(h_ref, mod_ref, g_ref, win_ref, lng_ref, lnb_ref, ws_ref, bst_ref, o_ref, *, tm):
    sub = min(tm, GMLP_SUB_ROWS)
    for r0 in range(0, tm, sub):
        a = _norm_mod(h_ref[0, r0:r0 + sub, :], g_ref[...], mod_ref[0, 0:1, :], mod_ref[0, 1:2, :]).astype(BF16)
        z = jax.nn.gelu(_dot(a, win_ref[...]), approximate=True)
        u, v = z[:, :GMLP_HALF], z[:, GMLP_HALF:]
        mu = jnp.mean(v, axis=-1, keepdims=True)
        vc = v - mu
        v = vc * jax.lax.rsqrt(jnp.mean(vc * vc, axis=-1, keepdims=True) + EPS) * lng_ref[...] + lnb_ref[...]
        v = v.astype(BF16)
        for ch in range(sub // CHUNK):
            rows = slice(ch * CHUNK, (ch + 1) * CHUNK)
            out_rows = slice(r0 + ch * CHUNK, r0 + (ch + 1) * CHUNK)
            for grp in range(GMLP_GROUPS):
                cols = slice(grp * GMLP_GW, (grp + 1) * GMLP_GW)
                sv = _dot(ws_ref[grp], v[rows, cols]) + bst_ref[:, grp:grp + 1]
                o_ref[0, out_rows, cols] = (u[rows, cols] * sv).astype(o_ref.dtype)


def _gmlp_call(h, mod, g, w_in, ln_g, ln_b, ws, bs, tm=512):
    B, L, D = h.shape
    return pl.pallas_call(
        functools.partial(_gmlp_kernel, tm=tm),
        grid=(B, L // tm),
        in_specs=[
            pl.BlockSpec((1, tm, D), lambda b, i: (b, i, 0)),
            _mod_spec(mod),
            _resident((1, D)),
            _resident(w_in.shape),
            _resident((1, GMLP_HALF)),
            _resident((1, GMLP_HALF)),
            _resident(ws.shape),
            _resident((CHUNK, GMLP_GROUPS)),
        ],
        out_specs=pl.BlockSpec((1, tm, GMLP_HALF), lambda b, i: (b, i, 0)),
        out_shape=jax.ShapeDtypeStruct((B, L, GMLP_HALF), BF16),
        compiler_params=_params("parallel", "parallel"),
        name="gmlp_gate",
    )(h, mod, g.reshape(1, D), w_in, ln_g.reshape(1, GMLP_HALF), ln_b.reshape(1, GMLP_HALF), ws, bs.T)


def kernel(x, c, ctx, c_ctx, ada_w, ada_b, norm_g, mlp_w1, mlp_w2, pool_w, pool_scale, attn_w_qkv, attn_w_o,
           attn_q_g, attn_k_g, gm_w_in, gm_ln_g, gm_ln_b, gm_ws, gm_bs, gm_w_out, final_g):
    B, S, D = x.shape
    last_ctx_read = max([i for i in range(DEPTH) if i % N_MIXERS == 1], default=-1)

    pad = (-(B + 1)) % 8
    s_in = jnp.concatenate([c, c_ctx[None, :], jnp.zeros((pad, D), F32)], axis=0)
    mods = _ada_call(s_in, ada_w, ada_b)
    mod_lat = mods[:, :B].reshape(DEPTH, B, 6, D)
    mod_ctx = mods[:, B:B + 1].reshape(DEPTH, 1, 6, D)

    perm = _head_perm()
    h_lat, h_ctx = x, ctx
    for i in range(DEPTH):
        kind, j = i % N_MIXERS, i // N_MIXERS
        ctx_in = i <= last_ctx_read
        ctx_out = i < last_ctx_read
        w1, w2 = _cast_call(mlp_w1, i), _cast_call(mlp_w2, i)
        fin = final_g if i == DEPTH - 1 else None
        if kind == 0:
            pw = _cast_call(pool_w, j)
            h_lat = _pool_mlp_call(h_lat, mod_lat[i], norm_g[i, 0], pw, pool_scale[j], norm_g[i, 1], w1, w2,
                                   final_g=fin)
            if ctx_out:
                h_ctx = _pool_mlp_call(h_ctx, mod_ctx[i], norm_g[i, 0], pw, pool_scale[j], norm_g[i, 1], w1, w2)
        elif kind == 1:
            cols = np.concatenate([hd * HEAD_DIM + perm for hd in range(N_HEADS + N_KV_HEADS)]
                                  + [np.arange(Q_WIDTH + KV_WIDTH, attn_w_qkv.shape[-1])])
            wqkv = _cast_call(attn_w_qkv, j)[:, cols]
            q_g, k_g = attn_q_g[j][perm], attn_k_g[j][perm]
            q, k_l, vt_l = _qkv_call(h_lat, mod_lat[i], norm_g[i, 0], wqkv, q_g, k_g, _rope_tables(S), True)
            k_c, vt_c = _qkv_call(h_ctx, mod_ctx[i], norm_g[i, 0], wqkv, q_g, k_g, None, False)
            o = _flash_call(q, [(k_l, vt_l), (k_c, vt_c)])
            if ctx_out:
                raise NotImplementedError("context stream output of an attention layer")
            h_lat = _mlp_call(h_lat, mod_lat[i], norm_g[i, 1], w1, w2,
                              proj=(o, _cast_call(attn_w_o, j)), final_g=fin)
        else:
            t = _gmlp_call(h_lat, mod_lat[i], norm_g[i, 0], _cast_call(gm_w_in, j), gm_ln_g[j], gm_ln_b[j],
                           _cast_call(gm_ws, j), gm_bs[j])
            if ctx_out:
                raise NotImplementedError("context stream output of a gMLP layer")
            h_lat = _mlp_call(h_lat, mod_lat[i], norm_g[i, 1], w1, w2,
                              proj=(t, _cast_call(gm_w_out, j)), final_g=fin)
    return h_lat
```

```python
import functools

import jax
import jax.numpy as jnp
import numpy as np
from jax.experimental import pallas as pl
from jax.experimental.pallas import tpu as pltpu

D_MODEL = 1024
DEPTH = 4
N_MIXERS = 3
GRID_W = 64
EPS = 1e-6
POOL_WINDOWS = (2, 4, 8, 16)
POOL_GW = D_MODEL // len(POOL_WINDOWS)
POOL_HALO = 16
HEAD_DIM = 128
N_HEADS = D_MODEL // HEAD_DIM
N_KV_HEADS = N_HEADS // 2
Q_PER_KV = N_HEADS // N_KV_HEADS
Q_WIDTH = N_HEADS * HEAD_DIM
KV_WIDTH = N_KV_HEADS * HEAD_DIM
VT_ROWS = HEAD_DIM + 16
ROPE_BASE = 10000.0
CHUNK = 128
GMLP_HALF = 2 * D_MODEL
GMLP_GROUPS = 8
GMLP_GW = GMLP_HALF // GMLP_GROUPS
D_FF = 4 * D_MODEL
LOG2E = 1.4426950408889634
FLASH_MAX_EXCESS = 64.0
GMLP_SUB_ROWS = 256
QKV_SUB_ROWS = 256
CAST_BLOCK_ELEMS = 1024 * 1024

VMEM_LIMIT_BYTES = 56 * 1024 * 1024
BF16 = jnp.bfloat16
F32 = jnp.float32


def _params(*semantics):
    return pltpu.CompilerParams(dimension_semantics=semantics, vmem_limit_bytes=VMEM_LIMIT_BYTES)


def _resident(shape):
    zeros = (0,) * len(shape)
    return pl.BlockSpec(shape, lambda *_: zeros, pipeline_mode=pl.Buffered(1))


def _mod_spec(mod):
    if mod.shape[0] == 1:
        return pl.BlockSpec((1, 6, D_MODEL), lambda b, i: (0, 0, 0))
    return pl.BlockSpec((1, 6, D_MODEL), lambda b, i: (b, 0, 0))


def _rms(x, g):
    return x * jax.lax.rsqrt(jnp.mean(x * x, axis=-1, keepdims=True) + EPS) * g


def _norm_mod(x, g, shift, scale):
    return _rms(x, g) * (1.0 + scale) + shift


def _dot(a, b):
    return jnp.dot(a, b, preferred_element_type=F32)


def _cast_kernel(w_ref, o_ref):
    o_ref[...] = w_ref[0].astype(o_ref.dtype)


def _cast_call(w_stack, layer):
    shape = w_stack.shape[1:]
    C = shape[-1]
    R = int(np.prod(shape[:-1]))
    br = min(R, CAST_BLOCK_ELEMS // C)
    out = pl.pallas_call(
        _cast_kernel,
        grid=(R // br,),
        in_specs=[pl.BlockSpec((1, br, C), lambda r: (layer, r, 0))],
        out_specs=pl.BlockSpec((br, C), lambda r: (r, 0)),
        out_shape=jax.ShapeDtypeStruct((R, C), BF16),
        compiler_params=_params("parallel"),
        name="cast_bf16",
    )(w_stack.reshape(w_stack.shape[0], R, C))
    return out.reshape(shape)


def _ada_kernel(s_ref, w_ref, b_ref, o_ref):
    s = s_ref[...]
    s = s * jax.nn.sigmoid(s)
    o_ref[0] = _dot(s.astype(BF16), w_ref[0].astype(BF16)) + b_ref[0]


def _ada_call(s_in, ada_w, ada_b):
    rows = s_in.shape[0]
    tn = 1536
    return pl.pallas_call(
        _ada_kernel,
        grid=(DEPTH, 6 * D_MODEL // tn),
        in_specs=[
            pl.BlockSpec((rows, D_MODEL), lambda l, j: (0, 0)),
            pl.BlockSpec((1, D_MODEL, tn), lambda l, j: (l, 0, j)),
            pl.BlockSpec((1, 1, tn), lambda l, j: (l, 0, j)),
        ],
        out_specs=pl.BlockSpec((1, rows, tn), lambda l, j: (l, 0, j)),
        out_shape=jax.ShapeDtypeStruct((DEPTH, rows, 6 * D_MODEL), F32),
        compiler_params=_params("arbitrary", "arbitrary"),
        name="ada_mod",
    )(s_in, ada_w, ada_b.reshape(DEPTH, 1, 6 * D_MODEL))


def _pool_window(a_ext, a_main, pos, seq_len, grp):
    n_ext, n_main = a_ext.shape[0], a_main.shape[0]
    win = POOL_WINDOWS[grp]
    cols = slice(grp * POOL_GW, (grp + 1) * POOL_GW)
    lo_off, hi_off = -(win // 2), win - win // 2
    s = a_ext[:, cols]
    if hi_off > 1:
        s = pltpu.roll(s, n_ext - (hi_off - 1), axis=0)
    k = 1
    while k < win:
        s = s + pltpu.roll(s, k, axis=0)
        k *= 2
    s = s[POOL_HALO:POOL_HALO + n_main]
    cnt = (jnp.minimum(pos + hi_off, seq_len) - jnp.maximum(pos + lo_off, 0)).astype(F32)
    return (s / cnt - a_main[:, cols]).astype(BF16)


def _mlp_rows(x, mod_ref, g_ref, w1_ref, w2_ref, fg_ref):
    m = _norm_mod(x, g_ref[...], mod_ref[0, 3:4, :], mod_ref[0, 4:5, :]).astype(BF16)
    a = jnp.maximum(_dot(m, w1_ref[...]), 0.0)
    a = (a * a).astype(BF16)
    y = x + mod_ref[0, 5:6, :] * _dot(a, w2_ref[...])
    return y if fg_ref is None else _rms(y, fg_ref[...])


def _mlp_kernel(*refs, has_proj, has_final):
    refs = list(refs)
    h_ref, mod_ref, g_ref = refs[:3]
    refs = refs[3:]
    if has_proj:
        t_ref, wp_ref = refs[:2]
        refs = refs[2:]
    w1_ref, w2_ref = refs[:2]
    fg_ref = refs[2] if has_final else None
    o_ref = refs[-1]

    x = h_ref[0]
    if has_proj:
        x = x + mod_ref[0, 2:3, :] * _dot(t_ref[0], wp_ref[...])
    o_ref[0] = _mlp_rows(x, mod_ref, g_ref, w1_ref, w2_ref, fg_ref)


def _mlp_call(h, mod, g, w1, w2, proj=None, final_g=None, tm=512):
    B, L, D = h.shape
    tm = min(tm, L)
    tok = lambda width: pl.BlockSpec((1, tm, width), lambda b, i: (b, i, 0))
    args = [h, mod, g.reshape(1, D)]
    specs = [tok(D), _mod_spec(mod), _resident((1, D))]
    if proj is not None:
        t, wp = proj
        args += [t, wp]
        specs += [tok(t.shape[-1]), _resident(wp.shape)]
    args += [w1, w2]
    specs += [_resident(w1.shape), _resident(w2.shape)]
    if final_g is not None:
        args.append(final_g.reshape(1, D))
        specs.append(_resident((1, D)))
    return pl.pallas_call(
        functools.partial(_mlp_kernel, has_proj=proj is not None, has_final=final_g is not None),
        grid=(B, L // tm),
        in_specs=specs,
        out_specs=tok(D),
        out_shape=jax.ShapeDtypeStruct(h.shape, F32),
        compiler_params=_params("parallel", "parallel"),
        name="mlp",
    )(*args)


def _pool_mlp_kernel(*refs, has_final, seq_len, tiles_per_seq, n_tiles):
    refs = list(refs)
    h_ref, hp_ref, hn_ref, moda_ref, g0_ref, pw_ref, psc_ref, modb_ref, g_ref, w1_ref, w2_ref = refs[:11]
    fg_ref = refs[11] if has_final else None
    o_ref, x1_ref = refs[-2:]
    s = pl.program_id(0)
    tm = h_ref.shape[1]

    @pl.when(s == 0)
    def _():
        x1_ref[1] = jnp.zeros(x1_ref.shape[1:], F32)

    i = jnp.minimum(s, n_tiles - 1) % tiles_per_seq

    def stages(write_slot, read_slot):
        xb = x1_ref[read_slot]
        m = _norm_mod(xb, g_ref[...], modb_ref[0, 3:4, :], modb_ref[0, 4:5, :]).astype(BF16)
        norm = functools.partial(_norm_mod, g=g0_ref[...], shift=moda_ref[0, 0:1, :], scale=moda_ref[0, 1:2, :])
        x = h_ref[0]
        a_main = norm(x)
        top = jnp.where(i > 0, norm(hp_ref[0]), 0.0)
        bottom = jnp.where(i < tiles_per_seq - 1, norm(hn_ref[0]), 0.0)
        pos = i * tm + jax.lax.broadcasted_iota(jnp.int32, (tm, 1), 0)
        a_ext = jnp.concatenate([top, a_main, bottom], axis=0)

        n_grp = len(POOL_WINDOWS)
        fc = D_FF // n_grp
        acc, ys = None, []
        for grp in range(n_grp):
            p = _pool_window(a_ext, a_main, pos, seq_len, grp)
            a = jnp.maximum(_dot(m, w1_ref[:, grp * fc:(grp + 1) * fc]), 0.0)
            d = _dot((a * a).astype(BF16), w2_ref[grp * fc:(grp + 1) * fc, :])
            acc = d if acc is None else acc + d
            ys.append(_dot(p, pw_ref[grp]))
        y = xb + modb_ref[0, 5:6, :] * acc
        o_ref[0] = y if fg_ref is None else _rms(y, fg_ref[...])
        x1_ref[write_slot] = x + moda_ref[0, 2:3, :] * (jnp.concatenate(ys, axis=-1) * psc_ref[...])

    pl.when(s % 2 == 0)(functools.partial(stages, 0, 1))
    pl.when(s % 2 == 1)(functools.partial(stages, 1, 0))


def _pool_mlp_call(h, mod, g0, pool_w, pool_scale, g, w1, w2, final_g=None, tm=512):
    B, L, D = h.shape
    tm = min(tm, L)
    tps = L // tm
    n_tiles = B * tps
    r = tm // POOL_HALO
    last = L // POOL_HALO - 1
    shared_mod = mod.shape[0] == 1

    def tile_a(s):
        t = jnp.minimum(s, n_tiles - 1)
        return t // tps, t % tps

    def tile_b(s):
        t = jnp.maximum(s - 1, 0)
        return t // tps, t % tps

    def mod_spec(tile):
        return pl.BlockSpec((1, 6, D), lambda s: (0 if shared_mod else tile(s)[0], 0, 0))

    args = [h, h, h, mod, g0.reshape(1, D), pool_w, pool_scale.reshape(1, D), mod, g.reshape(1, D), w1, w2]
    specs = [
        pl.BlockSpec((1, tm, D), lambda s: (*tile_a(s), 0)),
        pl.BlockSpec((1, POOL_HALO, D), lambda s: (tile_a(s)[0], jnp.maximum(tile_a(s)[1] * r - 1, 0), 0)),
        pl.BlockSpec((1, POOL_HALO, D), lambda s: (tile_a(s)[0], jnp.minimum((tile_a(s)[1] + 1) * r, last), 0)),
        mod_spec(tile_a), _resident((1, D)), _resident(pool_w.shape), _resident((1, D)),
        mod_spec(tile_b), _resident((1, D)), _resident(w1.shape), _resident(w2.shape),
    ]
    if final_g is not None:
        args.append(final_g.reshape(1, D))
        specs.append(_resident((1, D)))
    return pl.pallas_call(
        functools.partial(_pool_mlp_kernel, has_final=final_g is not None, seq_len=L, tiles_per_seq=tps,
                          n_tiles=n_tiles),
        grid=(n_tiles + 1,),
        in_specs=specs,
        out_specs=pl.BlockSpec((1, tm, D), lambda s: (*tile_b(s), 0)),
        out_shape=jax.ShapeDtypeStruct(h.shape, F32),
        scratch_shapes=[pltpu.VMEM((2, tm, D), F32)],
        compiler_params=_params("arbitrary"),
        name="pool_mlp",
    )(*args)


def _head_perm():
    q = HEAD_DIM // 4
    return np.concatenate([np.arange(0, q), np.arange(2 * q, 3 * q), np.arange(q, 2 * q), np.arange(3 * q, 4 * q)])


def _rope_tables(L):
    rows_n = L // GRID_W
    half = HEAD_DIM // 2
    inv = (np.float32(ROPE_BASE) ** (-np.arange(0, half, 2, dtype=np.float32) / np.float32(half))).astype(np.float32)
    ang_r = (np.arange(rows_n, dtype=np.float32)[:, None] * inv[None, :]).astype(np.float64)
    ang_c = (np.arange(GRID_W, dtype=np.float32)[:, None] * inv[None, :]).astype(np.float64)

    def expand(fr, fc):
        r = jnp.repeat(jnp.asarray(fr(ang_r), F32), GRID_W, axis=0)
        c = jnp.tile(jnp.asarray(fc(ang_c), F32), (rows_n, 1))
        return r, c

    cos_r, cos_c = expand(np.cos, np.cos)
    sin_r, sin_c = expand(np.sin, np.sin)
    return (jnp.concatenate([cos_r, cos_c, cos_r, cos_c], axis=-1),
            jnp.concatenate([-sin_r, -sin_c, sin_r, sin_c], axis=-1))


def _qkv_kernel(*refs, rope, want_q):
    refs = list(refs)
    h_ref, mod_ref, g_ref, w_ref, qg_ref, kg_ref = refs[:6]
    refs = refs[6:]
    if rope:
        cos_ref, sin_ref = refs[:2]
        refs = refs[2:]
    if want_q:
        q_ref = refs[0]
        refs = refs[1:]
    k_ref, vt_ref = refs

    tm = h_ref.shape[1]
    sub = min(tm, QKV_SUB_ROWS)
    chains = []
    for r0 in range(0, tm, sub):
        rows = slice(r0, r0 + sub)
        a = _norm_mod(h_ref[0, rows, :], g_ref[...], mod_ref[0, 0:1, :], mod_ref[0, 1:2, :]).astype(BF16)
        yq = _dot(a, w_ref[:, :Q_WIDTH]) if want_q else None
        yk = _dot(a, w_ref[:, Q_WIDTH:Q_WIDTH + KV_WIDTH])
        yv = _dot(a, w_ref[:, Q_WIDTH + KV_WIDTH:])
        chains.append((rows, yq, yk, yv))

    for rows, yq, yk, yv in chains:
        def heads(y, gain, out_ref):
            for hd in range(y.shape[1] // HEAD_DIM):
                t = _rms(y[:, hd * HEAD_DIM:(hd + 1) * HEAD_DIM], gain)
                if rope:
                    t = t * cos_ref[rows, :] + pltpu.roll(t, HEAD_DIM // 2, axis=1) * sin_ref[rows, :]
                out_ref[0, rows, hd * HEAD_DIM:(hd + 1) * HEAD_DIM] = t.astype(out_ref.dtype)

        if want_q:
            heads(yq, qg_ref[...], q_ref)
        heads(yk, kg_ref[...], k_ref)
        for hd in range(N_KV_HEADS):
            vt_ref[0, hd, 0:HEAD_DIM, rows] = yv[:, hd * HEAD_DIM:(hd + 1) * HEAD_DIM].T.astype(vt_ref.dtype)
            vt_ref[0, hd, HEAD_DIM:, rows] = jnp.ones((VT_ROWS - HEAD_DIM, sub), vt_ref.dtype)


def _qkv_call(h, mod, g, w, q_g, k_g, rope_tabs, want_q, tm=1024):
    B, L, D = h.shape
    tm = min(tm, L)
    tok = lambda width: pl.BlockSpec((1, tm, width), lambda b, i: (b, i, 0))
    args = [h, mod, g.reshape(1, D), w, q_g.reshape(1, HEAD_DIM), k_g.reshape(1, HEAD_DIM)]
    specs = [tok(D), _mod_spec(mod), _resident((1, D)), _resident(w.shape),
             _resident((1, HEAD_DIM)), _resident((1, HEAD_DIM))]
    if rope_tabs is not None:
        args += list(rope_tabs)
        specs += [pl.BlockSpec((tm, HEAD_DIM), lambda b, i: (i, 0))] * 2
    out_shape, out_specs = [], []
    if want_q:
        out_shape.append(jax.ShapeDtypeStruct((B, L, Q_WIDTH), BF16))
        out_specs.append(tok(Q_WIDTH))
    out_shape += [jax.ShapeDtypeStruct((B, L, KV_WIDTH), BF16),
                  jax.ShapeDtypeStruct((B, N_KV_HEADS, VT_ROWS, L), BF16)]
    out_specs += [tok(KV_WIDTH), pl.BlockSpec((1, N_KV_HEADS, VT_ROWS, tm), lambda b, i: (b, 0, 0, i))]
    return pl.pallas_call(
        functools.partial(_qkv_kernel, rope=rope_tabs is not None, want_q=want_q),
        grid=(B, L // tm),
        in_specs=specs,
        out_specs=out_specs,
        out_shape=out_shape,
        compiler_params=_params("parallel", "parallel"),
        name="qkv_proj",
    )(*args)


def _flash_kernel(*refs, tq, tk):
    q_ref, o_ref = refs[0], refs[-1]
    sources = [(refs[1 + 2 * s], refs[2 + 2 * s]) for s in range((len(refs) - 2) // 2)]
    q = jnp.concatenate([q_ref[0, :, hd * HEAD_DIM:(hd + 1) * HEAD_DIM] for hd in range(Q_PER_KV)], axis=0)
    c = (HEAD_DIM ** -0.5) * LOG2E
    nt = (((1,), (1,)), ((), ()))

    def block_size(k_ref):
        return min(tk, k_ref.shape[1])

    def finish(acc):
        o = (acc[:HEAD_DIM] / acc[HEAD_DIM:HEAD_DIM + 1]).T
        for hd in range(Q_PER_KV):
            o_ref[0, :, hd * HEAD_DIM:(hd + 1) * HEAD_DIM] = o[hd * tq:(hd + 1) * tq].astype(o_ref.dtype)

    blocks = [(k_ref, vt_ref, start, block_size(k_ref)) for k_ref, vt_ref in sources
              for start in range(0, k_ref.shape[1], block_size(k_ref))]
    acc = jnp.zeros((VT_ROWS, q.shape[0]), F32)
    excess = jnp.zeros((1, q.shape[0]), F32)
    for j, (k_ref, vt_ref, start, size) in enumerate(blocks):
        st = jax.lax.dot_general(k_ref[0, start:start + size, :], q, nt, preferred_element_type=F32)
        block_max = jnp.max(st, axis=0, keepdims=True)
        if j == 0:
            shift = block_max
        else:
            excess = jnp.maximum(excess, (block_max - shift) * c)
        p = jnp.exp2((st - shift) * c).astype(BF16)
        acc = acc + _dot(vt_ref[0, 0, :, start:start + size], p)
        if 0 < j < len(blocks) - 1:
            new_shift = jnp.maximum(shift, block_max)
            acc = acc * jnp.exp2((shift - new_shift) * c)
            shift = new_shift
    finish(acc)

    @pl.when(jnp.max(excess) > FLASH_MAX_EXCESS)
    def _():
        carry = (jnp.full((1, q.shape[0]), -jnp.inf, F32), jnp.zeros((VT_ROWS, q.shape[0]), F32))
        for k_ref, vt_ref in sources:
            size = block_size(k_ref)

            def body(j, carry, k_ref=k_ref, vt_ref=vt_ref, size=size):
                m, acc = carry
                start = pl.multiple_of(j * size, size)
                st = jax.lax.dot_general(k_ref[0, pl.ds(start, size), :], q, nt, preferred_element_type=F32)
                m_next = jnp.maximum(m, jnp.max(st, axis=0, keepdims=True))
                p = jnp.exp2((st - m_next) * c).astype(BF16)
                acc = acc * jnp.exp2((m - m_next) * c) + _dot(vt_ref[0, 0, :, pl.ds(start, size)], p)
                return m_next, acc

            carry = jax.lax.fori_loop(0, k_ref.shape[1] // size, body, carry)
        finish(carry[1])


def _flash_call(q, kv_sources, tq=512, tk=1024):
    B, L, _ = q.shape
    args, specs = [q], [pl.BlockSpec((1, tq, Q_PER_KV * HEAD_DIM), lambda b, kh, i: (b, i, kh))]
    for k, vt in kv_sources:
        Lk = k.shape[1]
        assert Lk % min(tk, Lk) == 0
        args += [k, vt]
        specs += [pl.BlockSpec((1, Lk, HEAD_DIM), lambda b, kh, i: (b, 0, kh)),
                  pl.BlockSpec((1, 1, VT_ROWS, Lk), lambda b, kh, i: (b, kh, 0, 0))]
    return pl.pallas_call(
        functools.partial(_flash_kernel, tq=tq, tk=tk),
        grid=(B, N_KV_HEADS, L // tq),
        in_specs=specs,
        out_specs=pl.BlockSpec((1, tq, Q_PER_KV * HEAD_DIM), lambda b, kh, i: (b, i, kh)),
        out_shape=jax.ShapeDtypeStruct(q.shape, BF16),
        compiler_params=_params("parallel", "parallel", "parallel"),
        name="flash_gqa",
    )(*args)


def _gmlp_kernel(h_ref, mod_ref, g_ref, win_ref, lng_ref, lnb_ref, ws_ref, bst_ref, o_ref, *, tm):
    sub = min(tm, GMLP_SUB_ROWS)
    starts = range(0, tm, sub)
    a = [_norm_mod(h_ref[0, r0:r0 + sub, :], g_ref[...], mod_ref[0, 0:1, :], mod_ref[0, 1:2, :]).astype(BF16)
         for r0 in starts]
    zv = [_dot(a_c, win_ref[:, GMLP_HALF:]) for a_c in a]
    zu = [_dot(a_c, win_ref[:, :GMLP_HALF]) for a_c in a]
    for c, r0 in enumerate(starts):
        v = jax.nn.gelu(zv[c], approximate=True)
        mu = jnp.mean(v, axis=-1, keepdims=True)
        vc = v - mu
        v = vc * jax.lax.rsqrt(jnp.mean(vc * vc, axis=-1, keepdims=True) + EPS) * lng_ref[...] + lnb_ref[...]
        v = v.astype(BF16)
        u = jax.nn.gelu(zu[c], approximate=True)
        for ch in range(sub // CHUNK):
            rows = slice(ch * CHUNK, (ch + 1) * CHUNK)
            out_rows = slice(r0 + ch * CHUNK, r0 + (ch + 1) * CHUNK)
            for grp in range(GMLP_GROUPS):
                cols = slice(grp * GMLP_GW, (grp + 1) * GMLP_GW)
                sv = _dot(ws_ref[grp], v[rows, cols]) + bst_ref[:, grp:grp + 1]
                o_ref[0, out_rows, cols] = (u[rows, cols] * sv).astype(o_ref.dtype)


def _gmlp_call(h, mod, g, w_in, ln_g, ln_b, ws, bs, tm=512):
    B, L, D = h.shape
    return pl.pallas_call(
        functools.partial(_gmlp_kernel, tm=tm),
        grid=(B, L // tm),
        in_specs=[
            pl.BlockSpec((1, tm, D), lambda b, i: (b, i, 0)),
            _mod_spec(mod),
            _resident((1, D)),
            _resident(w_in.shape),
            _resident((1, GMLP_HALF)),
            _resident((1, GMLP_HALF)),
            _resident(ws.shape),
            _resident((CHUNK, GMLP_GROUPS)),
        ],
        out_specs=pl.BlockSpec((1, tm, GMLP_HALF), lambda b, i: (b, i, 0)),
        out_shape=jax.ShapeDtypeStruct((B, L, GMLP_HALF), BF16),
        compiler_params=_params("parallel", "parallel"),
        name="gmlp_gate",
    )(h, mod, g.reshape(1, D), w_in, ln_g.reshape(1, GMLP_HALF), ln_b.reshape(1, GMLP_HALF), ws, bs.T)


def kernel(x, c, ctx, c_ctx, ada_w, ada_b, norm_g, mlp_w1, mlp_w2, pool_w, pool_scale, attn_w_qkv, attn_w_o,
           attn_q_g, attn_k_g, gm_w_in, gm_ln_g, gm_ln_b, gm_ws, gm_bs, gm_w_out, final_g):
    B, S, D = x.shape
    last_ctx_read = max([i for i in range(DEPTH) if i % N_MIXERS == 1], default=-1)

    pad = (-(B + 1)) % 8
    s_in = jnp.concatenate([c, c_ctx[None, :], jnp.zeros((pad, D), F32)], axis=0)
    mods = _ada_call(s_in, ada_w, ada_b)
    mod_lat = mods[:, :B].reshape(DEPTH, B, 6, D)
    mod_ctx = mods[:, B:B + 1].reshape(DEPTH, 1, 6, D)

    perm = _head_perm()
    h_lat, h_ctx = x, ctx
    for i in range(DEPTH):
        kind, j = i % N_MIXERS, i // N_MIXERS
        ctx_in = i <= last_ctx_read
        ctx_out = i < last_ctx_read
        w1, w2 = _cast_call(mlp_w1, i), _cast_call(mlp_w2, i)
        fin = final_g if i == DEPTH - 1 else None
        if kind == 0:
            pw = _cast_call(pool_w, j)
            h_lat = _pool_mlp_call(h_lat, mod_lat[i], norm_g[i, 0], pw, pool_scale[j], norm_g[i, 1], w1, w2,
                                   final_g=fin)
            if ctx_out:
                h_ctx = _pool_mlp_call(h_ctx, mod_ctx[i], norm_g[i, 0], pw, pool_scale[j], norm_g[i, 1], w1, w2)
        elif kind == 1:
            cols = np.concatenate([hd * HEAD_DIM + perm for hd in range(N_HEADS + N_KV_HEADS)]
                                  + [np.arange(Q_WIDTH + KV_WIDTH, attn_w_qkv.shape[-1])])
            wqkv = _cast_call(attn_w_qkv, j)[:, cols]
            q_g, k_g = attn_q_g[j][perm], attn_k_g[j][perm]
            q, k_l, vt_l = _qkv_call(h_lat, mod_lat[i], norm_g[i, 0], wqkv, q_g, k_g, _rope_tables(S), True)
            k_c, vt_c = _qkv_call(h_ctx, mod_ctx[i], norm_g[i, 0], wqkv, q_g, k_g, None, False)
            o = _flash_call(q, [(k_l, vt_l), (k_c, vt_c)])
            if ctx_out:
                raise NotImplementedError("context stream output of an attention layer")
            h_lat = _mlp_call(h_lat, mod_lat[i], norm_g[i, 1], w1, w2,
                              proj=(o, _cast_call(attn_w_o, j)), final_g=fin)
        else:
            t = _gmlp_call(h_lat, mod_lat[i], norm_g[i, 0], _cast_call(gm_w_in, j), gm_ln_g[j], gm_ln_b[j],
                           _cast_call(gm_ws, j), gm_bs[j])
            if ctx_out:
                raise NotImplementedError("context stream output of a gMLP layer")
            h_lat = _mlp_call(h_lat, mod_lat[i], norm_g[i, 1], w1, w2,
                              proj=(t, _cast_call(gm_w_out, j)), final_g=fin)
    return h_lat
```

```python
import functools

import jax
import jax.numpy as jnp
import numpy as np
from jax.experimental import pallas as pl
from jax.experimental.pallas import tpu as pltpu

D_MODEL = 1024
DEPTH = 4
N_MIXERS = 3
GRID_W = 64
EPS = 1e-6
POOL_WINDOWS = (2, 4, 8, 16)
POOL_GW = D_MODEL // len(POOL_WINDOWS)
POOL_HALO = 16
HEAD_DIM = 128
N_HEADS = D_MODEL // HEAD_DIM
N_KV_HEADS = N_HEADS // 2
Q_PER_KV = N_HEADS // N_KV_HEADS
Q_WIDTH = N_HEADS * HEAD_DIM
KV_WIDTH = N_KV_HEADS * HEAD_DIM
VT_ROWS = HEAD_DIM + 16
ROPE_BASE = 10000.0
CHUNK = 128
GMLP_HALF = 2 * D_MODEL
GMLP_GROUPS = 8
GMLP_GW = GMLP_HALF // GMLP_GROUPS
D_FF = 4 * D_MODEL
LOG2E = 1.4426950408889634
FLASH_MAX_EXCESS = 64.0
GMLP_SUB_ROWS = 256
QKV_SUB_ROWS = 256
CAST_BLOCK_ELEMS = 1024 * 1024

VMEM_LIMIT_BYTES = 56 * 1024 * 1024
BF16 = jnp.bfloat16
F32 = jnp.float32


def _params(*semantics):
    return pltpu.CompilerParams(dimension_semantics=semantics, vmem_limit_bytes=VMEM_LIMIT_BYTES)


def _resident(shape):
    zeros = (0,) * len(shape)
    return pl.BlockSpec(shape, lambda *_: zeros, pipeline_mode=pl.Buffered(1))


def _mod_spec(mod):
    if mod.shape[0] == 1:
        return pl.BlockSpec((1, 6, D_MODEL), lambda b, i: (0, 0, 0))
    return pl.BlockSpec((1, 6, D_MODEL), lambda b, i: (b, 0, 0))


def _rms(x, g):
    return x * jax.lax.rsqrt(jnp.mean(x * x, axis=-1, keepdims=True) + EPS) * g


def _norm_mod(x, g, shift, scale):
    return _rms(x, g) * (1.0 + scale) + shift


def _dot(a, b):
    return jnp.dot(a, b, preferred_element_type=F32)


def _cast_kernel(w_ref, o_ref):
    o_ref[...] = w_ref[0].astype(o_ref.dtype)


def _cast_call(w_stack, layer):
    shape = w_stack.shape[1:]
    C = shape[-1]
    R = int(np.prod(shape[:-1]))
    br = min(R, CAST_BLOCK_ELEMS // C)
    out = pl.pallas_call(
        _cast_kernel,
        grid=(R // br,),
        in_specs=[pl.BlockSpec((1, br, C), lambda r: (layer, r, 0))],
        out_specs=pl.BlockSpec((br, C), lambda r: (r, 0)),
        out_shape=jax.ShapeDtypeStruct((R, C), BF16),
        compiler_params=_params("parallel"),
        name="cast_bf16",
    )(w_stack.reshape(w_stack.shape[0], R, C))
    return out.reshape(shape)


def _side_casts(casts, n_steps, step_of):
    args, in_specs, out_shapes, out_specs, shapes = [], [], [], [], []
    for w_stack, layer in casts:
        shape = w_stack.shape[1:]
        C = shape[-1]
        R = int(np.prod(shape[:-1]))
        assert R % (n_steps * 16) == 0
        br = R // n_steps
        args.append(w_stack.reshape(w_stack.shape[0], R, C))
        in_specs.append(pl.BlockSpec((1, br, C), lambda *idx, layer=layer: (layer, step_of(*idx), 0)))
        out_shapes.append(jax.ShapeDtypeStruct((R, C), BF16))
        out_specs.append(pl.BlockSpec((br, C), lambda *idx: (step_of(*idx), 0)))
        shapes.append(shape)
    return args, in_specs, out_shapes, out_specs, shapes


def _run_side_casts(in_refs, out_refs):
    for w_ref, o_ref in zip(in_refs, out_refs):
        o_ref[...] = w_ref[0].astype(o_ref.dtype)


def _ada_kernel(s_ref, w_ref, b_ref, o_ref):
    s = s_ref[...]
    s = s * jax.nn.sigmoid(s)
    o_ref[0] = _dot(s.astype(BF16), w_ref[0].astype(BF16)) + b_ref[0]


def _ada_call(s_in, ada_w, ada_b):
    rows = s_in.shape[0]
    tn = 1536
    return pl.pallas_call(
        _ada_kernel,
        grid=(DEPTH, 6 * D_MODEL // tn),
        in_specs=[
            pl.BlockSpec((rows, D_MODEL), lambda l, j: (0, 0)),
            pl.BlockSpec((1, D_MODEL, tn), lambda l, j: (l, 0, j)),
            pl.BlockSpec((1, 1, tn), lambda l, j: (l, 0, j)),
        ],
        out_specs=pl.BlockSpec((1, rows, tn), lambda l, j: (l, 0, j)),
        out_shape=jax.ShapeDtypeStruct((DEPTH, rows, 6 * D_MODEL), F32),
        compiler_params=_params("arbitrary", "arbitrary"),
        name="ada_mod",
    )(s_in, ada_w, ada_b.reshape(DEPTH, 1, 6 * D_MODEL))


def _pool_window(a_ext, a_main, pos, seq_len, grp):
    n_ext, n_main = a_ext.shape[0], a_main.shape[0]
    win = POOL_WINDOWS[grp]
    cols = slice(grp * POOL_GW, (grp + 1) * POOL_GW)
    lo_off, hi_off = -(win // 2), win - win // 2
    s = a_ext[:, cols]
    if hi_off > 1:
        s = pltpu.roll(s, n_ext - (hi_off - 1), axis=0)
    k = 1
    while k < win:
        s = s + pltpu.roll(s, k, axis=0)
        k *= 2
    s = s[POOL_HALO:POOL_HALO + n_main]
    cnt = (jnp.minimum(pos + hi_off, seq_len) - jnp.maximum(pos + lo_off, 0)).astype(F32)
    return (s / cnt - a_main[:, cols]).astype(BF16)


def _mlp_rows(x, mod_ref, g_ref, w1_ref, w2_ref, fg_ref):
    m = _norm_mod(x, g_ref[...], mod_ref[0, 3:4, :], mod_ref[0, 4:5, :]).astype(BF16)
    a = jnp.maximum(_dot(m, w1_ref[...]), 0.0)
    a = (a * a).astype(BF16)
    y = x + mod_ref[0, 5:6, :] * _dot(a, w2_ref[...])
    return y if fg_ref is None else _rms(y, fg_ref[...])


def _mlp_kernel(*refs, has_proj, has_final, n_cast):
    refs = list(refs)
    h_ref, mod_ref, g_ref = refs[:3]
    refs = refs[3:]
    if has_proj:
        t_ref, wp_ref = refs[:2]
        refs = refs[2:]
    w1_ref, w2_ref = refs[:2]
    refs = refs[2:]
    fg_ref = None
    if has_final:
        fg_ref = refs[0]
        refs = refs[1:]
    cast_in, o_ref, cast_out = refs[:n_cast], refs[n_cast], refs[n_cast + 1:]

    x = h_ref[0]
    if has_proj:
        x = x + mod_ref[0, 2:3, :] * _dot(t_ref[0], wp_ref[...])
    o_ref[0] = _mlp_rows(x, mod_ref, g_ref, w1_ref, w2_ref, fg_ref)
    _run_side_casts(cast_in, cast_out)


def _mlp_call(h, mod, g, w1, w2, proj=None, final_g=None, casts=(), tm=512):
    B, L, D = h.shape
    tm = min(tm, L)
    n = L // tm
    tok = lambda width: pl.BlockSpec((1, tm, width), lambda b, i: (b, i, 0))
    args = [h, mod, g.reshape(1, D)]
    specs = [tok(D), _mod_spec(mod), _resident((1, D))]
    if proj is not None:
        t, wp = proj
        args += [t, wp]
        specs += [tok(t.shape[-1]), _resident(wp.shape)]
    args += [w1, w2]
    specs += [_resident(w1.shape), _resident(w2.shape)]
    if final_g is not None:
        args.append(final_g.reshape(1, D))
        specs.append(_resident((1, D)))
    c_args, c_in, c_shapes, c_out, c_orig = _side_casts(casts, B * n, lambda b, i: b * n + i)
    outs = pl.pallas_call(
        functools.partial(_mlp_kernel, has_proj=proj is not None, has_final=final_g is not None, n_cast=len(casts)),
        grid=(B, n),
        in_specs=specs + c_in,
        out_specs=[tok(D)] + c_out,
        out_shape=[jax.ShapeDtypeStruct(h.shape, F32)] + c_shapes,
        compiler_params=_params("parallel", "parallel"),
        name="mlp",
    )(*args, *c_args)
    return outs[0], [w.reshape(s) for w, s in zip(outs[1:], c_orig)]


def _pool_mlp_kernel(*refs, has_final, seq_len, tiles_per_seq, n_tiles, n_cast):
    refs = list(refs)
    h_ref, hp_ref, hn_ref, moda_ref, g0_ref, pw_ref, psc_ref, modb_ref, g_ref, w1_ref, w2_ref = refs[:11]
    refs = refs[11:]
    fg_ref = None
    if has_final:
        fg_ref = refs[0]
        refs = refs[1:]
    cast_in, o_ref, cast_out, x1_ref = refs[:n_cast], refs[n_cast], refs[n_cast + 1:-1], refs[-1]
    s = pl.program_id(0)
    tm = h_ref.shape[1]

    @pl.when(s == 0)
    def _():
        x1_ref[1] = jnp.zeros(x1_ref.shape[1:], F32)

    i = jnp.minimum(s, n_tiles - 1) % tiles_per_seq

    def stages(write_slot, read_slot):
        xb = x1_ref[read_slot]
        m = _norm_mod(xb, g_ref[...], modb_ref[0, 3:4, :], modb_ref[0, 4:5, :]).astype(BF16)
        norm = functools.partial(_norm_mod, g=g0_ref[...], shift=moda_ref[0, 0:1, :], scale=moda_ref[0, 1:2, :])
        x = h_ref[0]
        a_main = norm(x)
        top = jnp.where(i > 0, norm(hp_ref[0]), 0.0)
        bottom = jnp.where(i < tiles_per_seq - 1, norm(hn_ref[0]), 0.0)
        pos = i * tm + jax.lax.broadcasted_iota(jnp.int32, (tm, 1), 0)
        a_ext = jnp.concatenate([top, a_main, bottom], axis=0)

        n_grp = len(POOL_WINDOWS)
        fc = D_FF // n_grp
        acc, ys = None, []
        for grp in range(n_grp):
            p = _pool_window(a_ext, a_main, pos, seq_len, grp)
            a = jnp.maximum(_dot(m, w1_ref[:, grp * fc:(grp + 1) * fc]), 0.0)
            d = _dot((a * a).astype(BF16), w2_ref[grp * fc:(grp + 1) * fc, :])
            acc = d if acc is None else acc + d
            ys.append(_dot(p, pw_ref[grp]))
        y = xb + modb_ref[0, 5:6, :] * acc
        o_ref[0] = y if fg_ref is None else _rms(y, fg_ref[...])
        x1_ref[write_slot] = x + moda_ref[0, 2:3, :] * (jnp.concatenate(ys, axis=-1) * psc_ref[...])

    pl.when(s % 2 == 0)(functools.partial(stages, 0, 1))
    pl.when(s % 2 == 1)(functools.partial(stages, 1, 0))
    _run_side_casts(cast_in, cast_out)


def _pool_mlp_call(h, mod, g0, pool_w, pool_scale, g, w1, w2, final_g=None, casts=(), tm=512):
    B, L, D = h.shape
    tm = min(tm, L)
    tps = L // tm
    n_tiles = B * tps
    r = tm // POOL_HALO
    last = L // POOL_HALO - 1
    shared_mod = mod.shape[0] == 1

    def tile_a(s):
        t = jnp.minimum(s, n_tiles - 1)
        return t // tps, t % tps

    def tile_b(s):
        t = jnp.maximum(s - 1, 0)
        return t // tps, t % tps

    def mod_spec(tile):
        return pl.BlockSpec((1, 6, D), lambda s: (0 if shared_mod else tile(s)[0], 0, 0))

    args = [h, h, h, mod, g0.reshape(1, D), pool_w, pool_scale.reshape(1, D), mod, g.reshape(1, D), w1, w2]
    specs = [
        pl.BlockSpec((1, tm, D), lambda s: (*tile_a(s), 0)),
        pl.BlockSpec((1, POOL_HALO, D), lambda s: (tile_a(s)[0], jnp.maximum(tile_a(s)[1] * r - 1, 0), 0)),
        pl.BlockSpec((1, POOL_HALO, D), lambda s: (tile_a(s)[0], jnp.minimum((tile_a(s)[1] + 1) * r, last), 0)),
        mod_spec(tile_a), _resident((1, D)), _resident(pool_w.shape), _resident((1, D)),
        mod_spec(tile_b), _resident((1, D)), _resident(w1.shape), _resident(w2.shape),
    ]
    if final_g is not None:
        args.append(final_g.reshape(1, D))
        specs.append(_resident((1, D)))
    c_args, c_in, c_shapes, c_out, c_orig = _side_casts(casts, n_tiles, lambda s: jnp.minimum(s, n_tiles - 1))
    outs = pl.pallas_call(
        functools.partial(_pool_mlp_kernel, has_final=final_g is not None, seq_len=L, tiles_per_seq=tps,
                          n_tiles=n_tiles, n_cast=len(casts)),
        grid=(n_tiles + 1,),
        in_specs=specs + c_in,
        out_specs=[pl.BlockSpec((1, tm, D), lambda s: (*tile_b(s), 0))] + c_out,
        out_shape=[jax.ShapeDtypeStruct(h.shape, F32)] + c_shapes,
        scratch_shapes=[pltpu.VMEM((2, tm, D), F32)],
        compiler_params=_params("arbitrary"),
        name="pool_mlp",
    )(*args, *c_args)
    return outs[0], [w.reshape(s) for w, s in zip(outs[1:], c_orig)]


def _head_perm():
    q = HEAD_DIM // 4
    return np.concatenate([np.arange(0, q), np.arange(2 * q, 3 * q), np.arange(q, 2 * q), np.arange(3 * q, 4 * q)])


def _rope_tables(L):
    rows_n = L // GRID_W
    half = HEAD_DIM // 2
    inv = (np.float32(ROPE_BASE) ** (-np.arange(0, half, 2, dtype=np.float32) / np.float32(half))).astype(np.float32)
    ang_r = (np.arange(rows_n, dtype=np.float32)[:, None] * inv[None, :]).astype(np.float64)
    ang_c = (np.arange(GRID_W, dtype=np.float32)[:, None] * inv[None, :]).astype(np.float64)

    def expand(fr, fc):
        r = jnp.repeat(jnp.asarray(fr(ang_r), F32), GRID_W, axis=0)
        c = jnp.tile(jnp.asarray(fc(ang_c), F32), (rows_n, 1))
        return r, c

    cos_r, cos_c = expand(np.cos, np.cos)
    sin_r, sin_c = expand(np.sin, np.sin)
    return (jnp.concatenate([cos_r, cos_c, cos_r, cos_c], axis=-1),
            jnp.concatenate([-sin_r, -sin_c, sin_r, sin_c], axis=-1))


def _qkv_kernel(*refs, rope, want_q):
    refs = list(refs)
    h_ref, mod_ref, g_ref, w_ref, qg_ref, kg_ref = refs[:6]
    refs = refs[6:]
    if rope:
        cos_ref, sin_ref = refs[:2]
        refs = refs[2:]
    if want_q:
        q_ref = refs[0]
        refs = refs[1:]
    k_ref, vt_ref = refs

    tm = h_ref.shape[1]
    sub = min(tm, QKV_SUB_ROWS)
    chains = []
    for r0 in range(0, tm, sub):
        rows = slice(r0, r0 + sub)
        a = _norm_mod(h_ref[0, rows, :], g_ref[...], mod_ref[0, 0:1, :], mod_ref[0, 1:2, :]).astype(BF16)
        yq = _dot(a, w_ref[:, :Q_WIDTH]) if want_q else None
        yk = _dot(a, w_ref[:, Q_WIDTH:Q_WIDTH + KV_WIDTH])
        yv = _dot(a, w_ref[:, Q_WIDTH + KV_WIDTH:])
        chains.append((rows, yq, yk, yv))

    for rows, yq, yk, yv in chains:
        def heads(y, gain, out_ref):
            for hd in range(y.shape[1] // HEAD_DIM):
                t = _rms(y[:, hd * HEAD_DIM:(hd + 1) * HEAD_DIM], gain)
                if rope:
                    t = t * cos_ref[rows, :] + pltpu.roll(t, HEAD_DIM // 2, axis=1) * sin_ref[rows, :]
                out_ref[0, rows, hd * HEAD_DIM:(hd + 1) * HEAD_DIM] = t.astype(out_ref.dtype)

        if want_q:
            heads(yq, qg_ref[...], q_ref)
        heads(yk, kg_ref[...], k_ref)
        for hd in range(N_KV_HEADS):
            vt_ref[0, hd, 0:HEAD_DIM, rows] = yv[:, hd * HEAD_DIM:(hd + 1) * HEAD_DIM].T.astype(vt_ref.dtype)
            vt_ref[0, hd, HEAD_DIM:, rows] = jnp.ones((VT_ROWS - HEAD_DIM, sub), vt_ref.dtype)


def _qkv_call(h, mod, g, w, q_g, k_g, rope_tabs, want_q, tm=1024):
    B, L, D = h.shape
    tm = min(tm, L)
    tok = lambda width: pl.BlockSpec((1, tm, width), lambda b, i: (b, i, 0))
    args = [h, mod, g.reshape(1, D), w, q_g.reshape(1, HEAD_DIM), k_g.reshape(1, HEAD_DIM)]
    specs = [tok(D), _mod_spec(mod), _resident((1, D)), _resident(w.shape),
             _resident((1, HEAD_DIM)), _resident((1, HEAD_DIM))]
    if rope_tabs is not None:
        args += list(rope_tabs)
        specs += [pl.BlockSpec((tm, HEAD_DIM), lambda b, i: (i, 0))] * 2
    out_shape, out_specs = [], []
    if want_q:
        out_shape.append(jax.ShapeDtypeStruct((B, L, Q_WIDTH), BF16))
        out_specs.append(tok(Q_WIDTH))
    out_shape += [jax.ShapeDtypeStruct((B, L, KV_WIDTH), BF16),
                  jax.ShapeDtypeStruct((B, N_KV_HEADS, VT_ROWS, L), BF16)]
    out_specs += [tok(KV_WIDTH), pl.BlockSpec((1, N_KV_HEADS, VT_ROWS, tm), lambda b, i: (b, 0, 0, i))]
    return pl.pallas_call(
        functools.partial(_qkv_kernel, rope=rope_tabs is not None, want_q=want_q),
        grid=(B, L // tm),
        in_specs=specs,
        out_specs=out_specs,
        out_shape=out_shape,
        compiler_params=_params("parallel", "parallel"),
        name="qkv_proj",
    )(*args)


def _flash_kernel(*refs, tq, tk):
    q_ref, o_ref = refs[0], refs[-1]
    sources = [(refs[1 + 2 * s], refs[2 + 2 * s]) for s in range((len(refs) - 2) // 2)]
    q = jnp.concatenate([q_ref[0, :, hd * HEAD_DIM:(hd + 1) * HEAD_DIM] for hd in range(Q_PER_KV)], axis=0)
    c = (HEAD_DIM ** -0.5) * LOG2E
    nt = (((1,), (1,)), ((), ()))

    def block_size(k_ref):
        return min(tk, k_ref.shape[1])

    def finish(acc):
        o = (acc[:HEAD_DIM] / acc[HEAD_DIM:HEAD_DIM + 1]).T
        for hd in range(Q_PER_KV):
            o_ref[0, :, hd * HEAD_DIM:(hd + 1) * HEAD_DIM] = o[hd * tq:(hd + 1) * tq].astype(o_ref.dtype)

    blocks = [(k_ref, vt_ref, start, block_size(k_ref)) for k_ref, vt_ref in sources
              for start in range(0, k_ref.shape[1], block_size(k_ref))]
    acc = jnp.zeros((VT_ROWS, q.shape[0]), F32)
    excess = jnp.zeros((1, q.shape[0]), F32)
    for j, (k_ref, vt_ref, start, size) in enumerate(blocks):
        st = jax.lax.dot_general(k_ref[0, start:start + size, :], q, nt, preferred_element_type=F32)
        block_max = jnp.max(st, axis=0, keepdims=True)
        if j == 0:
            shift = block_max
        else:
            excess = jnp.maximum(excess, (block_max - shift) * c)
        p = jnp.exp2((st - shift) * c).astype(BF16)
        acc = acc + _dot(vt_ref[0, 0, :, start:start + size], p)
        if 0 < j < len(blocks) - 1:
            new_shift = jnp.maximum(shift, block_max)
            acc = acc * jnp.exp2((shift - new_shift) * c)
            shift = new_shift
    finish(acc)

    @pl.when(jnp.max(excess) > FLASH_MAX_EXCESS)
    def _():
        carry = (jnp.full((1, q.shape[0]), -jnp.inf, F32), jnp.zeros((VT_ROWS, q.shape[0]), F32))
        for k_ref, vt_ref in sources:
            size = block_size(k_ref)

            def body(j, carry, k_ref=k_ref, vt_ref=vt_ref, size=size):
                m, acc = carry
                start = pl.multiple_of(j * size, size)
                st = jax.lax.dot_general(k_ref[0, pl.ds(start, size), :], q, nt, preferred_element_type=F32)
                m_next = jnp.maximum(m, jnp.max(st, axis=0, keepdims=True))
                p = jnp.exp2((st - m_next) * c).astype(BF16)
                acc = acc * jnp.exp2((m - m_next) * c) + _dot(vt_ref[0, 0, :, pl.ds(start, size)], p)
                return m_next, acc

            carry = jax.lax.fori_loop(0, k_ref.shape[1] // size, body, carry)
        finish(carry[1])


def _flash_call(q, kv_sources, tq=512, tk=1024):
    B, L, _ = q.shape
    args, specs = [q], [pl.BlockSpec((1, tq, Q_PER_KV * HEAD_DIM), lambda b, kh, i: (b, i, kh))]
    for k, vt in kv_sources:
        Lk = k.shape[1]
        assert Lk % min(tk, Lk) == 0
        args += [k, vt]
        specs += [pl.BlockSpec((1, Lk, HEAD_DIM), lambda b, kh, i: (b, 0, kh)),
                  pl.BlockSpec((1, 1, VT_ROWS, Lk), lambda b, kh, i: (b, kh, 0, 0))]
    return pl.pallas_call(
        functools.partial(_flash_kernel, tq=tq, tk=tk),
        grid=(B, N_KV_HEADS, L // tq),
        in_specs=specs,
        out_specs=pl.BlockSpec((1, tq, Q_PER_KV * HEAD_DIM), lambda b, kh, i: (b, i, kh)),
        out_shape=jax.ShapeDtypeStruct(q.shape, BF16),
        compiler_params=_params("parallel", "parallel", "parallel"),
        name="flash_gqa",
    )(*args)


def _gmlp_kernel(h_ref, mod_ref, g_ref, win_ref, lng_ref, lnb_ref, ws_ref, bst_ref, o_ref, *, tm):
    sub = min(tm, GMLP_SUB_ROWS)
    starts = range(0, tm, sub)
    a = [_norm_mod(h_ref[0, r0:r0 + sub, :], g_ref[...], mod_ref[0, 0:1, :], mod_ref[0, 1:2, :]).astype(BF16)
         for r0 in starts]
    zv = [_dot(a_c, win_ref[:, GMLP_HALF:]) for a_c in a]
    zu = [_dot(a_c, win_ref[:, :GMLP_HALF]) for a_c in a]
    for c, r0 in enumerate(starts):
        v = jax.nn.gelu(zv[c], approximate=True)
        mu = jnp.mean(v, axis=-1, keepdims=True)
        vc = v - mu
        v = vc * jax.lax.rsqrt(jnp.mean(vc * vc, axis=-1, keepdims=True) + EPS) * lng_ref[...] + lnb_ref[...]
        v = v.astype(BF16)
        u = jax.nn.gelu(zu[c], approximate=True)
        for ch in range(sub // CHUNK):
            rows = slice(ch * CHUNK, (ch + 1) * CHUNK)
            out_rows = slice(r0 + ch * CHUNK, r0 + (ch + 1) * CHUNK)
            for grp in range(GMLP_GROUPS):
                cols = slice(grp * GMLP_GW, (grp + 1) * GMLP_GW)
                sv = _dot(ws_ref[grp], v[rows, cols]) + bst_ref[:, grp:grp + 1]
                o_ref[0, out_rows, cols] = (u[rows, cols] * sv).astype(o_ref.dtype)


def _gmlp_call(h, mod, g, w_in, ln_g, ln_b, ws, bs, tm=1024):
    B, L, D = h.shape
    return pl.pallas_call(
        functools.partial(_gmlp_kernel, tm=tm),
        grid=(B, L // tm),
        in_specs=[
            pl.BlockSpec((1, tm, D), lambda b, i: (b, i, 0)),
            _mod_spec(mod),
            _resident((1, D)),
            _resident(w_in.shape),
            _resident((1, GMLP_HALF)),
            _resident((1, GMLP_HALF)),
            _resident(ws.shape),
            _resident((CHUNK, GMLP_GROUPS)),
        ],
        out_specs=pl.BlockSpec((1, tm, GMLP_HALF), lambda b, i: (b, i, 0)),
        out_shape=jax.ShapeDtypeStruct((B, L, GMLP_HALF), BF16),
        compiler_params=_params("parallel", "parallel"),
        name="gmlp_gate",
    )(h, mod, g.reshape(1, D), w_in, ln_g.reshape(1, GMLP_HALF), ln_b.reshape(1, GMLP_HALF), ws, bs.T)


def kernel(x, c, ctx, c_ctx, ada_w, ada_b, norm_g, mlp_w1, mlp_w2, pool_w, pool_scale, attn_w_qkv, attn_w_o,
           attn_q_g, attn_k_g, gm_w_in, gm_ln_g, gm_ln_b, gm_ws, gm_bs, gm_w_out, final_g):
    B, S, D = x.shape
    last_ctx_read = max([i for i in range(DEPTH) if i % N_MIXERS == 1], default=-1)

    pad = (-(B + 1)) % 8
    s_in = jnp.concatenate([c, c_ctx[None, :], jnp.zeros((pad, D), F32)], axis=0)
    mods = _ada_call(s_in, ada_w, ada_b)
    mod_lat = mods[:, :B].reshape(DEPTH, B, 6, D)
    mod_ctx = mods[:, B:B + 1].reshape(DEPTH, 1, 6, D)

    def layer_weight_stacks(i):
        kind, j = i % N_MIXERS, i // N_MIXERS
        mixer = {0: [(pool_w, j)], 1: [(attn_w_qkv, j), (attn_w_o, j)],
                 2: [(gm_w_in, j), (gm_ws, j), (gm_w_out, j)]}[kind]
        return mixer + [(mlp_w1, i), (mlp_w2, i)]

    perm = _head_perm()
    h_lat, h_ctx = x, ctx
    weights = [_cast_call(w, idx) for w, idx in layer_weight_stacks(0)]
    for i in range(DEPTH):
        kind, j = i % N_MIXERS, i // N_MIXERS
        ctx_in = i <= last_ctx_read
        ctx_out = i < last_ctx_read
        *mixer_w, w1, w2 = weights
        nxt = layer_weight_stacks(i + 1) if i + 1 < DEPTH else ()
        fin = final_g if i == DEPTH - 1 else None
        if kind == 0:
            pw, = mixer_w
            if ctx_out:
                h_ctx, _ = _pool_mlp_call(h_ctx, mod_ctx[i], norm_g[i, 0], pw, pool_scale[j], norm_g[i, 1], w1, w2)
            h_lat, weights = _pool_mlp_call(h_lat, mod_lat[i], norm_g[i, 0], pw, pool_scale[j], norm_g[i, 1], w1, w2,
                                            final_g=fin, casts=nxt)
        elif kind == 1:
            wqkv, wo = mixer_w
            cols = np.concatenate([hd * HEAD_DIM + perm for hd in range(N_HEADS + N_KV_HEADS)]
                                  + [np.arange(Q_WIDTH + KV_WIDTH, attn_w_qkv.shape[-1])])
            wqkv = wqkv[:, cols]
            q_g, k_g = attn_q_g[j][perm], attn_k_g[j][perm]
            q, k_l, vt_l = _qkv_call(h_lat, mod_lat[i], norm_g[i, 0], wqkv, q_g, k_g, _rope_tables(S), True)
            k_c, vt_c = _qkv_call(h_ctx, mod_ctx[i], norm_g[i, 0], wqkv, q_g, k_g, None, False)
            o = _flash_call(q, [(k_l, vt_l), (k_c, vt_c)])
            if ctx_out:
                raise NotImplementedError("context stream output of an attention layer")
            h_lat, weights = _mlp_call(h_lat, mod_lat[i], norm_g[i, 1], w1, w2, proj=(o, wo), final_g=fin, casts=nxt)
        else:
            w_in, ws, w_out = mixer_w
            t = _gmlp_call(h_lat, mod_lat[i], norm_g[i, 0], w_in, gm_ln_g[j], gm_ln_b[j], ws, gm_bs[j])
            if ctx_out:
                raise NotImplementedError("context stream output of a gMLP layer")
            h_lat, weights = _mlp_call(h_lat, mod_lat[i], norm_g[i, 1], w1, w2, proj=(t, w_out), final_g=fin,
                                       casts=nxt)
    return h_lat
```

```python
import functools

import jax
import jax.numpy as jnp
import numpy as np
from jax.experimental import pallas as pl
from jax.experimental.pallas import tpu as pltpu

D_MODEL = 1024
DEPTH = 4
N_MIXERS = 3
GRID_W = 64
EPS = 1e-6
POOL_WINDOWS = (2, 4, 8, 16)
POOL_GW = D_MODEL // len(POOL_WINDOWS)
POOL_HALO = 16
HEAD_DIM = 128
N_HEADS = D_MODEL // HEAD_DIM
N_KV_HEADS = N_HEADS // 2
Q_PER_KV = N_HEADS // N_KV_HEADS
Q_WIDTH = N_HEADS * HEAD_DIM
KV_WIDTH = N_KV_HEADS * HEAD_DIM
VT_ROWS = HEAD_DIM + 16
ROPE_BASE = 10000.0
CHUNK = 128
GMLP_HALF = 2 * D_MODEL
GMLP_GROUPS = 8
GMLP_GW = GMLP_HALF // GMLP_GROUPS
D_FF = 4 * D_MODEL
LOG2E = 1.4426950408889634
FLASH_MAX_EXCESS = 64.0
GMLP_SUB_ROWS = 256
QKV_SUB_ROWS = 256
CAST_BLOCK_ELEMS = 1024 * 1024

VMEM_LIMIT_BYTES = 56 * 1024 * 1024
BF16 = jnp.bfloat16
F32 = jnp.float32


def _params(*semantics):
    return pltpu.CompilerParams(dimension_semantics=semantics, vmem_limit_bytes=VMEM_LIMIT_BYTES)


def _resident(shape):
    zeros = (0,) * len(shape)
    return pl.BlockSpec(shape, lambda *_: zeros, pipeline_mode=pl.Buffered(1))


def _mod_spec(mod):
    if mod.shape[0] == 1:
        return pl.BlockSpec((1, 6, D_MODEL), lambda b, i: (0, 0, 0))
    return pl.BlockSpec((1, 6, D_MODEL), lambda b, i: (b, 0, 0))


def _rms(x, g):
    return x * jax.lax.rsqrt(jnp.mean(x * x, axis=-1, keepdims=True) + EPS) * g


def _norm_mod(x, g, shift, scale):
    return _rms(x, g) * (1.0 + scale) + shift


def _dot(a, b):
    return jnp.dot(a, b, preferred_element_type=F32)


def _cast_block(w, permute_heads):
    if permute_heads:
        heads = [_permute_head_lanes(w[:, hd * HEAD_DIM:(hd + 1) * HEAD_DIM]) for hd in range(permute_heads)]
        w = jnp.concatenate(heads + [w[:, permute_heads * HEAD_DIM:]], axis=-1)
    return w.astype(BF16)


def _cast_kernel(w_ref, o_ref, *, permute_heads):
    o_ref[...] = _cast_block(w_ref[0], permute_heads)


def _cast_call(w_stack, layer, permute_heads=0):
    shape = w_stack.shape[1:]
    C = shape[-1]
    R = int(np.prod(shape[:-1]))
    br = min(R, CAST_BLOCK_ELEMS // C)
    out = pl.pallas_call(
        functools.partial(_cast_kernel, permute_heads=permute_heads),
        grid=(R // br,),
        in_specs=[pl.BlockSpec((1, br, C), lambda r: (layer, r, 0))],
        out_specs=pl.BlockSpec((br, C), lambda r: (r, 0)),
        out_shape=jax.ShapeDtypeStruct((R, C), BF16),
        compiler_params=_params("parallel"),
        name="cast_bf16",
    )(w_stack.reshape(w_stack.shape[0], R, C))
    return out.reshape(shape)


def _side_casts(casts, n_steps, step_of):
    args, in_specs, out_shapes, out_specs, shapes = [], [], [], [], []
    for w_stack, layer, _ in casts:
        shape = w_stack.shape[1:]
        C = shape[-1]
        R = int(np.prod(shape[:-1]))
        assert R % (n_steps * 16) == 0
        br = R // n_steps
        args.append(w_stack.reshape(w_stack.shape[0], R, C))
        in_specs.append(pl.BlockSpec((1, br, C), lambda *idx, layer=layer: (layer, step_of(*idx), 0)))
        out_shapes.append(jax.ShapeDtypeStruct((R, C), BF16))
        out_specs.append(pl.BlockSpec((br, C), lambda *idx: (step_of(*idx), 0)))
        shapes.append(shape)
    return args, in_specs, out_shapes, out_specs, shapes


def _run_side_casts(in_refs, out_refs, permute_heads):
    for w_ref, o_ref, heads in zip(in_refs, out_refs, permute_heads):
        o_ref[...] = _cast_block(w_ref[0], heads)


def _ada_kernel(s_ref, w_ref, b_ref, o_ref):
    s = s_ref[...]
    s = s * jax.nn.sigmoid(s)
    o_ref[0] = _dot(s.astype(BF16), w_ref[0].astype(BF16)) + b_ref[0]


def _ada_call(s_in, ada_w, ada_b):
    rows = s_in.shape[0]
    tn = 1536
    return pl.pallas_call(
        _ada_kernel,
        grid=(DEPTH, 6 * D_MODEL // tn),
        in_specs=[
            pl.BlockSpec((rows, D_MODEL), lambda l, j: (0, 0)),
            pl.BlockSpec((1, D_MODEL, tn), lambda l, j: (l, 0, j)),
            pl.BlockSpec((1, 1, tn), lambda l, j: (l, 0, j)),
        ],
        out_specs=pl.BlockSpec((1, rows, tn), lambda l, j: (l, 0, j)),
        out_shape=jax.ShapeDtypeStruct((DEPTH, rows, 6 * D_MODEL), F32),
        compiler_params=_params("arbitrary", "arbitrary"),
        name="ada_mod",
    )(s_in, ada_w, ada_b.reshape(DEPTH, 1, 6 * D_MODEL))


def _pool_window(a_ext, a_main, pos, seq_len, grp):
    n_ext, n_main = a_ext.shape[0], a_main.shape[0]
    win = POOL_WINDOWS[grp]
    cols = slice(grp * POOL_GW, (grp + 1) * POOL_GW)
    lo_off, hi_off = -(win // 2), win - win // 2
    s = a_ext[:, cols]
    if hi_off > 1:
        s = pltpu.roll(s, n_ext - (hi_off - 1), axis=0)
    k = 1
    while k < win:
        s = s + pltpu.roll(s, k, axis=0)
        k *= 2
    s = s[POOL_HALO:POOL_HALO + n_main]
    cnt = (jnp.minimum(pos + hi_off, seq_len) - jnp.maximum(pos + lo_off, 0)).astype(F32)
    return (s / cnt - a_main[:, cols]).astype(BF16)


def _mlp_rows(x, mod_ref, g_ref, w1_ref, w2_ref, fg_ref):
    m = _norm_mod(x, g_ref[...], mod_ref[0, 3:4, :], mod_ref[0, 4:5, :]).astype(BF16)
    a = jnp.maximum(_dot(m, w1_ref[...]), 0.0)
    a = (a * a).astype(BF16)
    y = x + mod_ref[0, 5:6, :] * _dot(a, w2_ref[...])
    return y if fg_ref is None else _rms(y, fg_ref[...])


def _mlp_kernel(*refs, has_proj, has_final, cast_heads):
    refs = list(refs)
    h_ref, mod_ref, g_ref = refs[:3]
    refs = refs[3:]
    if has_proj:
        t_ref, wp_ref = refs[:2]
        refs = refs[2:]
    w1_ref, w2_ref = refs[:2]
    refs = refs[2:]
    fg_ref = None
    if has_final:
        fg_ref = refs[0]
        refs = refs[1:]
    n_cast = len(cast_heads)
    cast_in, o_ref, cast_out = refs[:n_cast], refs[n_cast], refs[n_cast + 1:]

    x = h_ref[0]
    if has_proj:
        x = x + mod_ref[0, 2:3, :] * _dot(t_ref[0], wp_ref[...])
    o_ref[0] = _mlp_rows(x, mod_ref, g_ref, w1_ref, w2_ref, fg_ref)
    _run_side_casts(cast_in, cast_out, cast_heads)


def _mlp_call(h, mod, g, w1, w2, proj=None, final_g=None, casts=(), tm=512):
    B, L, D = h.shape
    tm = min(tm, L)
    n = L // tm
    tok = lambda width: pl.BlockSpec((1, tm, width), lambda b, i: (b, i, 0))
    args = [h, mod, g.reshape(1, D)]
    specs = [tok(D), _mod_spec(mod), _resident((1, D))]
    if proj is not None:
        t, wp = proj
        args += [t, wp]
        specs += [tok(t.shape[-1]), _resident(wp.shape)]
    args += [w1, w2]
    specs += [_resident(w1.shape), _resident(w2.shape)]
    if final_g is not None:
        args.append(final_g.reshape(1, D))
        specs.append(_resident((1, D)))
    c_args, c_in, c_shapes, c_out, c_orig = _side_casts(casts, B * n, lambda b, i: b * n + i)
    outs = pl.pallas_call(
        functools.partial(_mlp_kernel, has_proj=proj is not None, has_final=final_g is not None,
                          cast_heads=tuple(c[2] for c in casts)),
        grid=(B, n),
        in_specs=specs + c_in,
        out_specs=[tok(D)] + c_out,
        out_shape=[jax.ShapeDtypeStruct(h.shape, F32)] + c_shapes,
        compiler_params=_params("parallel", "parallel"),
        name="mlp",
    )(*args, *c_args)
    return outs[0], [w.reshape(s) for w, s in zip(outs[1:], c_orig)]


def _pool_mlp_kernel(*refs, has_final, seq_len, tiles_per_seq, n_tiles, cast_heads):
    refs = list(refs)
    h_ref, hp_ref, hn_ref, moda_ref, g0_ref, pw_ref, psc_ref, modb_ref, g_ref, w1_ref, w2_ref = refs[:11]
    refs = refs[11:]
    fg_ref = None
    if has_final:
        fg_ref = refs[0]
        refs = refs[1:]
    n_cast = len(cast_heads)
    cast_in, o_ref, cast_out, x1_ref = refs[:n_cast], refs[n_cast], refs[n_cast + 1:-1], refs[-1]
    s = pl.program_id(0)
    tm = h_ref.shape[1]

    @pl.when(s == 0)
    def _():
        x1_ref[1] = jnp.zeros(x1_ref.shape[1:], F32)

    i = jnp.minimum(s, n_tiles - 1) % tiles_per_seq

    def stages(write_slot, read_slot):
        xb = x1_ref[read_slot]
        m = _norm_mod(xb, g_ref[...], modb_ref[0, 3:4, :], modb_ref[0, 4:5, :]).astype(BF16)
        norm = functools.partial(_norm_mod, g=g0_ref[...], shift=moda_ref[0, 0:1, :], scale=moda_ref[0, 1:2, :])
        x = h_ref[0]
        a_main = norm(x)
        top = jnp.where(i > 0, norm(hp_ref[0]), 0.0)
        bottom = jnp.where(i < tiles_per_seq - 1, norm(hn_ref[0]), 0.0)
        pos = i * tm + jax.lax.broadcasted_iota(jnp.int32, (tm, 1), 0)
        a_ext = jnp.concatenate([top, a_main, bottom], axis=0)

        n_grp = len(POOL_WINDOWS)
        fc = D_FF // n_grp
        acc, ys = None, []
        for grp in range(n_grp):
            p = _pool_window(a_ext, a_main, pos, seq_len, grp)
            a = jnp.maximum(_dot(m, w1_ref[:, grp * fc:(grp + 1) * fc]), 0.0)
            d = _dot((a * a).astype(BF16), w2_ref[grp * fc:(grp + 1) * fc, :])
            acc = d if acc is None else acc + d
            ys.append(_dot(p, pw_ref[grp]))
        y = xb + modb_ref[0, 5:6, :] * acc
        o_ref[0] = y if fg_ref is None else _rms(y, fg_ref[...])
        x1_ref[write_slot] = x + moda_ref[0, 2:3, :] * (jnp.concatenate(ys, axis=-1) * psc_ref[...])

    pl.when(s % 2 == 0)(functools.partial(stages, 0, 1))
    pl.when(s % 2 == 1)(functools.partial(stages, 1, 0))
    _run_side_casts(cast_in, cast_out, cast_heads)


def _pool_mlp_call(h, mod, g0, pool_w, pool_scale, g, w1, w2, final_g=None, casts=(), tm=512):
    B, L, D = h.shape
    tm = min(tm, L)
    tps = L // tm
    n_tiles = B * tps
    r = tm // POOL_HALO
    last = L // POOL_HALO - 1
    shared_mod = mod.shape[0] == 1

    def tile_a(s):
        t = jnp.minimum(s, n_tiles - 1)
        return t // tps, t % tps

    def tile_b(s):
        t = jnp.maximum(s - 1, 0)
        return t // tps, t % tps

    def mod_spec(tile):
        return pl.BlockSpec((1, 6, D), lambda s: (0 if shared_mod else tile(s)[0], 0, 0))

    args = [h, h, h, mod, g0.reshape(1, D), pool_w, pool_scale.reshape(1, D), mod, g.reshape(1, D), w1, w2]
    specs = [
        pl.BlockSpec((1, tm, D), lambda s: (*tile_a(s), 0)),
        pl.BlockSpec((1, POOL_HALO, D), lambda s: (tile_a(s)[0], jnp.maximum(tile_a(s)[1] * r - 1, 0), 0)),
        pl.BlockSpec((1, POOL_HALO, D), lambda s: (tile_a(s)[0], jnp.minimum((tile_a(s)[1] + 1) * r, last), 0)),
        mod_spec(tile_a), _resident((1, D)), _resident(pool_w.shape), _resident((1, D)),
        mod_spec(tile_b), _resident((1, D)), _resident(w1.shape), _resident(w2.shape),
    ]
    if final_g is not None:
        args.append(final_g.reshape(1, D))
        specs.append(_resident((1, D)))
    c_args, c_in, c_shapes, c_out, c_orig = _side_casts(casts, n_tiles, lambda s: jnp.minimum(s, n_tiles - 1))
    outs = pl.pallas_call(
        functools.partial(_pool_mlp_kernel, has_final=final_g is not None, seq_len=L, tiles_per_seq=tps,
                          n_tiles=n_tiles, cast_heads=tuple(c[2] for c in casts)),
        grid=(n_tiles + 1,),
        in_specs=specs + c_in,
        out_specs=[pl.BlockSpec((1, tm, D), lambda s: (*tile_b(s), 0))] + c_out,
        out_shape=[jax.ShapeDtypeStruct(h.shape, F32)] + c_shapes,
        scratch_shapes=[pltpu.VMEM((2, tm, D), F32)],
        compiler_params=_params("arbitrary"),
        name="pool_mlp",
    )(*args, *c_args)
    return outs[0], [w.reshape(s) for w, s in zip(outs[1:], c_orig)]


def _permute_head_lanes(t):
    q = HEAD_DIM // 4
    lane = jax.lax.broadcasted_iota(jnp.int32, t.shape, 1)
    from_right = pltpu.roll(t, HEAD_DIM - q, axis=1)
    from_left = pltpu.roll(t, q, axis=1)
    return jnp.where(lane < q, t, jnp.where(lane < 2 * q, from_right, jnp.where(lane < 3 * q, from_left, t)))


def _rope_tables(L):
    rows_n = L // GRID_W
    q = HEAD_DIM // 4
    half = HEAD_DIM // 2
    inv = (np.float32(ROPE_BASE) ** (-np.arange(0, half, 2, dtype=np.float32) / np.float32(half))).astype(np.float32)
    ang_r = (np.arange(rows_n, dtype=np.float32)[:, None] * inv[None, :]).astype(np.float64)
    ang_c = (np.arange(GRID_W, dtype=np.float32)[:, None] * inv[None, :]).astype(np.float64)

    def table(fn, signs):
        by_row = np.zeros((rows_n, HEAD_DIM), np.float32)
        by_col = np.zeros((GRID_W, HEAD_DIM), np.float32)
        for quarter, sign in enumerate(signs):
            tab, ang = (by_row, ang_r) if quarter % 2 == 0 else (by_col, ang_c)
            tab[:, quarter * q:(quarter + 1) * q] = sign * fn(ang)
        return (jnp.broadcast_to(jnp.asarray(by_row)[:, None, :], (rows_n, GRID_W, HEAD_DIM))
                + jnp.asarray(by_col)[None, :, :]).reshape(L, HEAD_DIM)

    return table(np.cos, (1.0, 1.0, 1.0, 1.0)), table(np.sin, (-1.0, -1.0, 1.0, 1.0))


def _qkv_kernel(*refs, rope, want_q):
    refs = list(refs)
    h_ref, mod_ref, g_ref, w_ref, qg_ref, kg_ref = refs[:6]
    refs = refs[6:]
    if rope:
        cos_ref, sin_ref = refs[:2]
        refs = refs[2:]
    if want_q:
        q_ref = refs[0]
        refs = refs[1:]
    k_ref, vt_ref = refs

    tm = h_ref.shape[1]
    sub = min(tm, QKV_SUB_ROWS)
    chains = []
    for r0 in range(0, tm, sub):
        rows = slice(r0, r0 + sub)
        a = _norm_mod(h_ref[0, rows, :], g_ref[...], mod_ref[0, 0:1, :], mod_ref[0, 1:2, :]).astype(BF16)
        yq = _dot(a, w_ref[:, :Q_WIDTH]) if want_q else None
        yk = _dot(a, w_ref[:, Q_WIDTH:Q_WIDTH + KV_WIDTH])
        yv = _dot(a, w_ref[:, Q_WIDTH + KV_WIDTH:])
        chains.append((rows, yq, yk, yv))

    for rows, yq, yk, yv in chains:
        def heads(y, gain, out_ref):
            for hd in range(y.shape[1] // HEAD_DIM):
                t = _rms(y[:, hd * HEAD_DIM:(hd + 1) * HEAD_DIM], gain)
                if rope:
                    t = t * cos_ref[rows, :] + pltpu.roll(t, HEAD_DIM // 2, axis=1) * sin_ref[rows, :]
                out_ref[0, rows, hd * HEAD_DIM:(hd + 1) * HEAD_DIM] = t.astype(out_ref.dtype)

        if want_q:
            heads(yq, qg_ref[...], q_ref)
        heads(yk, kg_ref[...], k_ref)
        for hd in range(N_KV_HEADS):
            vt_ref[0, hd, 0:HEAD_DIM, rows] = yv[:, hd * HEAD_DIM:(hd + 1) * HEAD_DIM].T.astype(vt_ref.dtype)
            vt_ref[0, hd, HEAD_DIM:, rows] = jnp.ones((VT_ROWS - HEAD_DIM, sub), vt_ref.dtype)


def _qkv_call(h, mod, g, w, q_g, k_g, rope_tabs, want_q, tm=1024):
    B, L, D = h.shape
    tm = min(tm, L)
    tok = lambda width: pl.BlockSpec((1, tm, width), lambda b, i: (b, i, 0))
    args = [h, mod, g.reshape(1, D), w, q_g.reshape(1, HEAD_DIM), k_g.reshape(1, HEAD_DIM)]
    specs = [tok(D), _mod_spec(mod), _resident((1, D)), _resident(w.shape),
             _resident((1, HEAD_DIM)), _resident((1, HEAD_DIM))]
    if rope_tabs is not None:
        args += list(rope_tabs)
        specs += [pl.BlockSpec((tm, HEAD_DIM), lambda b, i: (i, 0))] * 2
    out_shape, out_specs = [], []
    if want_q:
        out_shape.append(jax.ShapeDtypeStruct((B, L, Q_WIDTH), BF16))
        out_specs.append(tok(Q_WIDTH))
    out_shape += [jax.ShapeDtypeStruct((B, L, KV_WIDTH), BF16),
                  jax.ShapeDtypeStruct((B, N_KV_HEADS, VT_ROWS, L), BF16)]
    out_specs += [tok(KV_WIDTH), pl.BlockSpec((1, N_KV_HEADS, VT_ROWS, tm), lambda b, i: (b, 0, 0, i))]
    return pl.pallas_call(
        functools.partial(_qkv_kernel, rope=rope_tabs is not None, want_q=want_q),
        grid=(B, L // tm),
        in_specs=specs,
        out_specs=out_specs,
        out_shape=out_shape,
        compiler_params=_params("parallel", "parallel"),
        name="qkv_proj",
    )(*args)


def _flash_kernel(*refs, tq, tk):
    q_ref, o_ref = refs[0], refs[-1]
    sources = [(refs[1 + 2 * s], refs[2 + 2 * s]) for s in range((len(refs) - 2) // 2)]
    q = jnp.concatenate([q_ref[0, :, hd * HEAD_DIM:(hd + 1) * HEAD_DIM] for hd in range(Q_PER_KV)], axis=0)
    c = (HEAD_DIM ** -0.5) * LOG2E
    nt = (((1,), (1,)), ((), ()))

    def block_size(k_ref):
        return min(tk, k_ref.shape[1])

    def finish(acc):
        o = (acc[:HEAD_DIM] / acc[HEAD_DIM:HEAD_DIM + 1]).T
        for hd in range(Q_PER_KV):
            o_ref[0, :, hd * HEAD_DIM:(hd + 1) * HEAD_DIM] = o[hd * tq:(hd + 1) * tq].astype(o_ref.dtype)

    blocks = [(k_ref, vt_ref, start, block_size(k_ref)) for k_ref, vt_ref in sources
              for start in range(0, k_ref.shape[1], block_size(k_ref))]
    acc = jnp.zeros((VT_ROWS, q.shape[0]), F32)
    excess = jnp.zeros((1, q.shape[0]), F32)
    for j, (k_ref, vt_ref, start, size) in enumerate(blocks):
        st = jax.lax.dot_general(k_ref[0, start:start + size, :], q, nt, preferred_element_type=F32)
        block_max = jnp.max(st, axis=0, keepdims=True)
        if j == 0:
            shift = block_max
        else:
            excess = jnp.maximum(excess, (block_max - shift) * c)
        p = jnp.exp2((st - shift) * c).astype(BF16)
        acc = acc + _dot(vt_ref[0, 0, :, start:start + size], p)
        if 0 < j < len(blocks) - 1:
            new_shift = jnp.maximum(shift, block_max)
            acc = acc * jnp.exp2((shift - new_shift) * c)
            shift = new_shift
    finish(acc)

    @pl.when(jnp.max(excess) > FLASH_MAX_EXCESS)
    def _():
        carry = (jnp.full((1, q.shape[0]), -jnp.inf, F32), jnp.zeros((VT_ROWS, q.shape[0]), F32))
        for k_ref, vt_ref in sources:
            size = block_size(k_ref)

            def body(j, carry, k_ref=k_ref, vt_ref=vt_ref, size=size):
                m, acc = carry
                start = pl.multiple_of(j * size, size)
                st = jax.lax.dot_general(k_ref[0, pl.ds(start, size), :], q, nt, preferred_element_type=F32)
                m_next = jnp.maximum(m, jnp.max(st, axis=0, keepdims=True))
                p = jnp.exp2((st - m_next) * c).astype(BF16)
                acc = acc * jnp.exp2((m - m_next) * c) + _dot(vt_ref[0, 0, :, pl.ds(start, size)], p)
                return m_next, acc

            carry = jax.lax.fori_loop(0, k_ref.shape[1] // size, body, carry)
        finish(carry[1])


def _flash_call(q, kv_sources, tq=512, tk=1024):
    B, L, _ = q.shape
    args, specs = [q], [pl.BlockSpec((1, tq, Q_PER_KV * HEAD_DIM), lambda b, kh, i: (b, i, kh))]
    for k, vt in kv_sources:
        Lk = k.shape[1]
        assert Lk % min(tk, Lk) == 0
        args += [k, vt]
        specs += [pl.BlockSpec((1, Lk, HEAD_DIM), lambda b, kh, i: (b, 0, kh)),
                  pl.BlockSpec((1, 1, VT_ROWS, Lk), lambda b, kh, i: (b, kh, 0, 0))]
    return pl.pallas_call(
        functools.partial(_flash_kernel, tq=tq, tk=tk),
        grid=(B, N_KV_HEADS, L // tq),
        in_specs=specs,
        out_specs=pl.BlockSpec((1, tq, Q_PER_KV * HEAD_DIM), lambda b, kh, i: (b, i, kh)),
        out_shape=jax.ShapeDtypeStruct(q.shape, BF16),
        compiler_params=_params("parallel", "parallel", "parallel"),
        name="flash_gqa",
    )(*args)


def _gmlp_kernel(h_ref, mod_ref, g_ref, win_ref, lng_ref, lnb_ref, ws_ref, bst_ref, o_ref, *, tm):
    sub = min(tm, GMLP_SUB_ROWS)
    starts = range(0, tm, sub)
    a = [_norm_mod(h_ref[0, r0:r0 + sub, :], g_ref[...], mod_ref[0, 0:1, :], mod_ref[0, 1:2, :]).astype(BF16)
         for r0 in starts]
    zv = [_dot(a_c, win_ref[:, GMLP_HALF:]) for a_c in a]
    zu = [_dot(a_c, win_ref[:, :GMLP_HALF]) for a_c in a]
    for c, r0 in enumerate(starts):
        v = jax.nn.gelu(zv[c], approximate=True)
        mu = jnp.mean(v, axis=-1, keepdims=True)
        vc = v - mu
        v = vc * jax.lax.rsqrt(jnp.mean(vc * vc, axis=-1, keepdims=True) + EPS) * lng_ref[...] + lnb_ref[...]
        v = v.astype(BF16)
        u = jax.nn.gelu(zu[c], approximate=True)
        for ch in range(sub // CHUNK):
            rows = slice(ch * CHUNK, (ch + 1) * CHUNK)
            out_rows = slice(r0 + ch * CHUNK, r0 + (ch + 1) * CHUNK)
            for grp in range(GMLP_GROUPS):
                cols = slice(grp * GMLP_GW, (grp + 1) * GMLP_GW)
                sv = _dot(ws_ref[grp], v[rows, cols]) + bst_ref[:, grp:grp + 1]
                o_ref[0, out_rows, cols] = (u[rows, cols] * sv).astype(o_ref.dtype)


def _gmlp_call(h, mod, g, w_in, ln_g, ln_b, ws, bs, tm=1024):
    B, L, D = h.shape
    return pl.pallas_call(
        functools.partial(_gmlp_kernel, tm=tm),
        grid=(B, L // tm),
        in_specs=[
            pl.BlockSpec((1, tm, D), lambda b, i: (b, i, 0)),
            _mod_spec(mod),
            _resident((1, D)),
            _resident(w_in.shape),
            _resident((1, GMLP_HALF)),
            _resident((1, GMLP_HALF)),
            _resident(ws.shape),
            _resident((CHUNK, GMLP_GROUPS)),
        ],
        out_specs=pl.BlockSpec((1, tm, GMLP_HALF), lambda b, i: (b, i, 0)),
        out_shape=jax.ShapeDtypeStruct((B, L, GMLP_HALF), BF16),
        compiler_params=_params("parallel", "parallel"),
        name="gmlp_gate",
    )(h, mod, g.reshape(1, D), w_in, ln_g.reshape(1, GMLP_HALF), ln_b.reshape(1, GMLP_HALF), ws, bs.T)


def kernel(x, c, ctx, c_ctx, ada_w, ada_b, norm_g, mlp_w1, mlp_w2, pool_w, pool_scale, attn_w_qkv, attn_w_o,
           attn_q_g, attn_k_g, gm_w_in, gm_ln_g, gm_ln_b, gm_ws, gm_bs, gm_w_out, final_g):
    B, S, D = x.shape
    last_ctx_read = max([i for i in range(DEPTH) if i % N_MIXERS == 1], default=-1)

    pad = (-(B + 1)) % 8
    s_in = jnp.concatenate([c, c_ctx[None, :], jnp.zeros((pad, D), F32)], axis=0)
    mods = _ada_call(s_in, ada_w, ada_b)
    mod_lat = mods[:, :B].reshape(DEPTH, B, 6, D)
    mod_ctx = mods[:, B:B + 1].reshape(DEPTH, 1, 6, D)

    def layer_weight_stacks(i):
        kind, j = i % N_MIXERS, i // N_MIXERS
        mixer = {0: [(pool_w, j, 0)], 1: [(attn_w_qkv, j, N_HEADS + N_KV_HEADS), (attn_w_o, j, 0)],
                 2: [(gm_w_in, j, 0), (gm_ws, j, 0), (gm_w_out, j, 0)]}[kind]
        return mixer + [(mlp_w1, i, 0), (mlp_w2, i, 0)]

    h_lat, h_ctx = x, ctx
    weights = [_cast_call(*entry) for entry in layer_weight_stacks(0)]
    for i in range(DEPTH):
        kind, j = i % N_MIXERS, i // N_MIXERS
        ctx_in = i <= last_ctx_read
        ctx_out = i < last_ctx_read
        *mixer_w, w1, w2 = weights
        nxt = layer_weight_stacks(i + 1) if i + 1 < DEPTH else ()
        fin = final_g if i == DEPTH - 1 else None
        if kind == 0:
            pw, = mixer_w
            if ctx_out:
                h_ctx, _ = _pool_mlp_call(h_ctx, mod_ctx[i], norm_g[i, 0], pw, pool_scale[j], norm_g[i, 1], w1, w2)
            h_lat, weights = _pool_mlp_call(h_lat, mod_lat[i], norm_g[i, 0], pw, pool_scale[j], norm_g[i, 1], w1, w2,
                                            final_g=fin, casts=nxt)
        elif kind == 1:
            wqkv, wo = mixer_w
            q_g, k_g = (g_[j].reshape(2, 2, HEAD_DIM // 4).transpose(1, 0, 2).reshape(HEAD_DIM)
                        for g_ in (attn_q_g, attn_k_g))
            q, k_l, vt_l = _qkv_call(h_lat, mod_lat[i], norm_g[i, 0], wqkv, q_g, k_g, _rope_tables(S), True)
            k_c, vt_c = _qkv_call(h_ctx, mod_ctx[i], norm_g[i, 0], wqkv, q_g, k_g, None, False)
            o = _flash_call(q, [(k_l, vt_l), (k_c, vt_c)])
            if ctx_out:
                raise NotImplementedError("context stream output of an attention layer")
            h_lat, weights = _mlp_call(h_lat, mod_lat[i], norm_g[i, 1], w1, w2, proj=(o, wo), final_g=fin, casts=nxt)
        else:
            w_in, ws, w_out = mixer_w
            t = _gmlp_call(h_lat, mod_lat[i], norm_g[i, 0], w_in, gm_ln_g[j], gm_ln_b[j], ws, gm_bs[j])
            if ctx_out:
                raise NotImplementedError("context stream output of a gMLP layer")
            h_lat, weights = _mlp_call(h_lat, mod_lat[i], norm_g[i, 1], w1, w2, proj=(t, w_out), final_g=fin,
                                       casts=nxt)
    return h_lat
```

```python
import functools

import jax
import jax.numpy as jnp
import numpy as np
from jax.experimental import pallas as pl
from jax.experimental.pallas import tpu as pltpu

D_MODEL = 1024
DEPTH = 4
N_MIXERS = 3
GRID_W = 64
EPS = 1e-6
POOL_WINDOWS = (2, 4, 8, 16)
POOL_GW = D_MODEL // len(POOL_WINDOWS)
POOL_HALO = 16
HEAD_DIM = 128
N_HEADS = D_MODEL // HEAD_DIM
N_KV_HEADS = N_HEADS // 2
Q_PER_KV = N_HEADS // N_KV_HEADS
Q_WIDTH = N_HEADS * HEAD_DIM
KV_WIDTH = N_KV_HEADS * HEAD_DIM
VT_ROWS = HEAD_DIM + 16
ROPE_BASE = 10000.0
CHUNK = 128
GMLP_HALF = 2 * D_MODEL
GMLP_GROUPS = 8
GMLP_GW = GMLP_HALF // GMLP_GROUPS
D_FF = 4 * D_MODEL
LOG2E = 1.4426950408889634
FLASH_MAX_EXCESS = 64.0
GMLP_SUB_ROWS = 256
QKV_SUB_ROWS = 256
CAST_BLOCK_ELEMS = 1024 * 1024

VMEM_LIMIT_BYTES = 56 * 1024 * 1024
BF16 = jnp.bfloat16
F32 = jnp.float32


def _params(*semantics):
    return pltpu.CompilerParams(dimension_semantics=semantics, vmem_limit_bytes=VMEM_LIMIT_BYTES)


def _resident(shape):
    zeros = (0,) * len(shape)
    return pl.BlockSpec(shape, lambda *_: zeros, pipeline_mode=pl.Buffered(1))


def _mod_spec(mod):
    if mod.shape[0] == 1:
        return pl.BlockSpec((1, 6, D_MODEL), lambda b, i: (0, 0, 0))
    return pl.BlockSpec((1, 6, D_MODEL), lambda b, i: (b, 0, 0))


def _rms(x, g):
    return x * jax.lax.rsqrt(jnp.mean(x * x, axis=-1, keepdims=True) + EPS) * g


def _norm_mod(x, g, shift, scale):
    return _rms(x, g) * (1.0 + scale) + shift


def _dot(a, b):
    return jnp.dot(a, b, preferred_element_type=F32)


def _cast_block(w, permute_heads):
    if permute_heads:
        heads = [_permute_head_lanes(w[:, hd * HEAD_DIM:(hd + 1) * HEAD_DIM]) for hd in range(permute_heads)]
        w = jnp.concatenate(heads + [w[:, permute_heads * HEAD_DIM:]], axis=-1)
    return w.astype(BF16)


def _cast_kernel(w_ref, o_ref, *, permute_heads):
    o_ref[...] = _cast_block(w_ref[0], permute_heads)


def _cast_call(w_stack, layer, permute_heads=0):
    shape = w_stack.shape[1:]
    C = shape[-1]
    R = int(np.prod(shape[:-1]))
    br = min(R, CAST_BLOCK_ELEMS // C)
    out = pl.pallas_call(
        functools.partial(_cast_kernel, permute_heads=permute_heads),
        grid=(R // br,),
        in_specs=[pl.BlockSpec((1, br, C), lambda r: (layer, r, 0))],
        out_specs=pl.BlockSpec((br, C), lambda r: (r, 0)),
        out_shape=jax.ShapeDtypeStruct((R, C), BF16),
        compiler_params=_params("parallel"),
        name="cast_bf16",
    )(w_stack.reshape(w_stack.shape[0], R, C))
    return out.reshape(shape)


def _side_casts(casts, n_steps, step_of):
    args, in_specs, out_shapes, out_specs, shapes = [], [], [], [], []
    for w_stack, layer, _ in casts:
        shape = w_stack.shape[1:]
        C = shape[-1]
        R = int(np.prod(shape[:-1]))
        assert R % (n_steps * 16) == 0
        br = R // n_steps
        args.append(w_stack.reshape(w_stack.shape[0], R, C))
        in_specs.append(pl.BlockSpec((1, br, C), lambda *idx, layer=layer: (layer, step_of(*idx), 0)))
        out_shapes.append(jax.ShapeDtypeStruct((R, C), BF16))
        out_specs.append(pl.BlockSpec((br, C), lambda *idx: (step_of(*idx), 0)))
        shapes.append(shape)
    return args, in_specs, out_shapes, out_specs, shapes


def _run_side_casts(in_refs, out_refs, permute_heads):
    for w_ref, o_ref, heads in zip(in_refs, out_refs, permute_heads):
        o_ref[...] = _cast_block(w_ref[0], heads)


def _ada_kernel(s_ref, w_ref, b_ref, o_ref):
    s = s_ref[...]
    s = s * jax.nn.sigmoid(s)
    o_ref[0] = _dot(s.astype(BF16), w_ref[0].astype(BF16)) + b_ref[0]


def _ada_call(s_in, ada_w, ada_b):
    rows = s_in.shape[0]
    tn = 1536
    return pl.pallas_call(
        _ada_kernel,
        grid=(DEPTH, 6 * D_MODEL // tn),
        in_specs=[
            pl.BlockSpec((rows, D_MODEL), lambda l, j: (0, 0)),
            pl.BlockSpec((1, D_MODEL, tn), lambda l, j: (l, 0, j)),
            pl.BlockSpec((1, 1, tn), lambda l, j: (l, 0, j)),
        ],
        out_specs=pl.BlockSpec((1, rows, tn), lambda l, j: (l, 0, j)),
        out_shape=jax.ShapeDtypeStruct((DEPTH, rows, 6 * D_MODEL), F32),
        compiler_params=_params("arbitrary", "arbitrary"),
        name="ada_mod",
    )(s_in, ada_w, ada_b.reshape(DEPTH, 1, 6 * D_MODEL))


def _pool_window(a_ext, a_main, pos, seq_len, grp):
    n_ext, n_main = a_ext.shape[0], a_main.shape[0]
    win = POOL_WINDOWS[grp]
    cols = slice(grp * POOL_GW, (grp + 1) * POOL_GW)
    lo_off, hi_off = -(win // 2), win - win // 2
    s = a_ext[:, cols]
    if hi_off > 1:
        s = pltpu.roll(s, n_ext - (hi_off - 1), axis=0)
    k = 1
    while k < win:
        s = s + pltpu.roll(s, k, axis=0)
        k *= 2
    s = s[POOL_HALO:POOL_HALO + n_main]
    cnt = (jnp.minimum(pos + hi_off, seq_len) - jnp.maximum(pos + lo_off, 0)).astype(F32)
    return (s / cnt - a_main[:, cols]).astype(BF16)


def _mlp_rows(x, mod_ref, g_ref, w1_ref, w2_ref, fg_ref):
    m = _norm_mod(x, g_ref[...], mod_ref[0, 3:4, :], mod_ref[0, 4:5, :]).astype(BF16)
    a = jnp.maximum(_dot(m, w1_ref[...]), 0.0)
    a = (a * a).astype(BF16)
    y = x + mod_ref[0, 5:6, :] * _dot(a, w2_ref[...])
    return y if fg_ref is None else _rms(y, fg_ref[...])


def _mlp_kernel(*refs, has_proj, has_final, cast_heads):
    refs = list(refs)
    h_ref, mod_ref, g_ref = refs[:3]
    refs = refs[3:]
    if has_proj:
        t_ref, wp_ref = refs[:2]
        refs = refs[2:]
    w1_ref, w2_ref = refs[:2]
    refs = refs[2:]
    fg_ref = None
    if has_final:
        fg_ref = refs[0]
        refs = refs[1:]
    n_cast = len(cast_heads)
    cast_in, o_ref, cast_out = refs[:n_cast], refs[n_cast], refs[n_cast + 1:]

    x = h_ref[0]
    if has_proj:
        x = x + mod_ref[0, 2:3, :] * _dot(t_ref[0], wp_ref[...])
    o_ref[0] = _mlp_rows(x, mod_ref, g_ref, w1_ref, w2_ref, fg_ref)
    _run_side_casts(cast_in, cast_out, cast_heads)


def _mlp_call(h, mod, g, w1, w2, proj=None, final_g=None, casts=(), tm=512):
    B, L, D = h.shape
    tm = min(tm, L)
    n = L // tm
    tok = lambda width: pl.BlockSpec((1, tm, width), lambda b, i: (b, i, 0))
    args = [h, mod, g.reshape(1, D)]
    specs = [tok(D), _mod_spec(mod), _resident((1, D))]
    if proj is not None:
        t, wp = proj
        args += [t, wp]
        specs += [tok(t.shape[-1]), _resident(wp.shape)]
    args += [w1, w2]
    specs += [_resident(w1.shape), _resident(w2.shape)]
    if final_g is not None:
        args.append(final_g.reshape(1, D))
        specs.append(_resident((1, D)))
    c_args, c_in, c_shapes, c_out, c_orig = _side_casts(casts, B * n, lambda b, i: b * n + i)
    outs = pl.pallas_call(
        functools.partial(_mlp_kernel, has_proj=proj is not None, has_final=final_g is not None,
                          cast_heads=tuple(c[2] for c in casts)),
        grid=(B, n),
        in_specs=specs + c_in,
        out_specs=[tok(D)] + c_out,
        out_shape=[jax.ShapeDtypeStruct(h.shape, F32)] + c_shapes,
        compiler_params=_params("parallel", "parallel"),
        name="mlp",
    )(*args, *c_args)
    return outs[0], [w.reshape(s) for w, s in zip(outs[1:], c_orig)]


def _pool_mlp_kernel(*refs, has_final, seq_len, tiles_per_seq, n_tiles, cast_heads):
    refs = list(refs)
    h_ref, hp_ref, hn_ref, moda_ref, g0_ref, pw_ref, psc_ref, modb_ref, g_ref, w1_ref, w2_ref = refs[:11]
    refs = refs[11:]
    fg_ref = None
    if has_final:
        fg_ref = refs[0]
        refs = refs[1:]
    n_cast = len(cast_heads)
    cast_in, o_ref, cast_out, x1_ref = refs[:n_cast], refs[n_cast], refs[n_cast + 1:-1], refs[-1]
    s = pl.program_id(0)
    tm = h_ref.shape[1]

    @pl.when(s == 0)
    def _():
        x1_ref[1] = jnp.zeros(x1_ref.shape[1:], F32)

    i = jnp.minimum(s, n_tiles - 1) % tiles_per_seq

    def stages(write_slot, read_slot):
        xb = x1_ref[read_slot]
        m = _norm_mod(xb, g_ref[...], modb_ref[0, 3:4, :], modb_ref[0, 4:5, :]).astype(BF16)
        norm = functools.partial(_norm_mod, g=g0_ref[...], shift=moda_ref[0, 0:1, :], scale=moda_ref[0, 1:2, :])
        x = h_ref[0]
        a_main = norm(x)
        top = jnp.where(i > 0, norm(hp_ref[0]), 0.0)
        bottom = jnp.where(i < tiles_per_seq - 1, norm(hn_ref[0]), 0.0)
        pos = i * tm + jax.lax.broadcasted_iota(jnp.int32, (tm, 1), 0)
        a_ext = jnp.concatenate([top, a_main, bottom], axis=0)

        n_grp = len(POOL_WINDOWS)
        fc = D_FF // n_grp
        acc, ys = None, []
        for grp in range(n_grp):
            p = _pool_window(a_ext, a_main, pos, seq_len, grp)
            a = jnp.maximum(_dot(m, w1_ref[:, grp * fc:(grp + 1) * fc]), 0.0)
            d = _dot((a * a).astype(BF16), w2_ref[grp * fc:(grp + 1) * fc, :])
            acc = d if acc is None else acc + d
            ys.append(_dot(p, pw_ref[grp]))
        y = xb + modb_ref[0, 5:6, :] * acc
        o_ref[0] = y if fg_ref is None else _rms(y, fg_ref[...])
        x1_ref[write_slot] = x + moda_ref[0, 2:3, :] * (jnp.concatenate(ys, axis=-1) * psc_ref[...])
        _run_side_casts(cast_in, cast_out, cast_heads)

    pl.when(s % 2 == 0)(functools.partial(stages, 0, 1))
    pl.when(s % 2 == 1)(functools.partial(stages, 1, 0))


def _pool_mlp_call(h, mod, g0, pool_w, pool_scale, g, w1, w2, final_g=None, casts=(), tm=512):
    B, L, D = h.shape
    tm = min(tm, L)
    tps = L // tm
    n_tiles = B * tps
    r = tm // POOL_HALO
    last = L // POOL_HALO - 1
    shared_mod = mod.shape[0] == 1

    def tile_a(s):
        t = jnp.minimum(s, n_tiles - 1)
        return t // tps, t % tps

    def tile_b(s):
        t = jnp.maximum(s - 1, 0)
        return t // tps, t % tps

    def mod_spec(tile):
        return pl.BlockSpec((1, 6, D), lambda s: (0 if shared_mod else tile(s)[0], 0, 0))

    args = [h, h, h, mod, g0.reshape(1, D), pool_w, pool_scale.reshape(1, D), mod, g.reshape(1, D), w1, w2]
    specs = [
        pl.BlockSpec((1, tm, D), lambda s: (*tile_a(s), 0)),
        pl.BlockSpec((1, POOL_HALO, D), lambda s: (tile_a(s)[0], jnp.maximum(tile_a(s)[1] * r - 1, 0), 0)),
        pl.BlockSpec((1, POOL_HALO, D), lambda s: (tile_a(s)[0], jnp.minimum((tile_a(s)[1] + 1) * r, last), 0)),
        mod_spec(tile_a), _resident((1, D)), _resident(pool_w.shape), _resident((1, D)),
        mod_spec(tile_b), _resident((1, D)), _resident(w1.shape), _resident(w2.shape),
    ]
    if final_g is not None:
        args.append(final_g.reshape(1, D))
        specs.append(_resident((1, D)))
    c_args, c_in, c_shapes, c_out, c_orig = _side_casts(casts, n_tiles, lambda s: jnp.minimum(s, n_tiles - 1))
    outs = pl.pallas_call(
        functools.partial(_pool_mlp_kernel, has_final=final_g is not None, seq_len=L, tiles_per_seq=tps,
                          n_tiles=n_tiles, cast_heads=tuple(c[2] for c in casts)),
        grid=(n_tiles + 1,),
        in_specs=specs + c_in,
        out_specs=[pl.BlockSpec((1, tm, D), lambda s: (*tile_b(s), 0))] + c_out,
        out_shape=[jax.ShapeDtypeStruct(h.shape, F32)] + c_shapes,
        scratch_shapes=[pltpu.VMEM((2, tm, D), F32)],
        compiler_params=_params("arbitrary"),
        name="pool_mlp",
    )(*args, *c_args)
    return outs[0], [w.reshape(s) for w, s in zip(outs[1:], c_orig)]


def _permute_head_lanes(t):
    q = HEAD_DIM // 4
    lane = jax.lax.broadcasted_iota(jnp.int32, t.shape, 1)
    from_right = pltpu.roll(t, HEAD_DIM - q, axis=1)
    from_left = pltpu.roll(t, q, axis=1)
    return jnp.where(lane < q, t, jnp.where(lane < 2 * q, from_right, jnp.where(lane < 3 * q, from_left, t)))


def _rope_tables(L):
    rows_n = L // GRID_W
    q = HEAD_DIM // 4
    half = HEAD_DIM // 2
    inv = (np.float32(ROPE_BASE) ** (-np.arange(0, half, 2, dtype=np.float32) / np.float32(half))).astype(np.float32)
    ang_r = (np.arange(rows_n, dtype=np.float32)[:, None] * inv[None, :]).astype(np.float64)
    ang_c = (np.arange(GRID_W, dtype=np.float32)[:, None] * inv[None, :]).astype(np.float64)

    def table(fn, signs):
        by_row = np.zeros((rows_n, HEAD_DIM), np.float32)
        by_col = np.zeros((GRID_W, HEAD_DIM), np.float32)
        for quarter, sign in enumerate(signs):
            tab, ang = (by_row, ang_r) if quarter % 2 == 0 else (by_col, ang_c)
            tab[:, quarter * q:(quarter + 1) * q] = sign * fn(ang)
        return (jnp.broadcast_to(jnp.asarray(by_row)[:, None, :], (rows_n, GRID_W, HEAD_DIM))
                + jnp.asarray(by_col)[None, :, :]).reshape(L, HEAD_DIM)

    return table(np.cos, (1.0, 1.0, 1.0, 1.0)), table(np.sin, (-1.0, -1.0, 1.0, 1.0))


def _qkv_kernel(*refs, rope, want_q):
    refs = list(refs)
    h_ref, mod_ref, g_ref, w_ref, qg_ref, kg_ref = refs[:6]
    refs = refs[6:]
    if rope:
        cos_ref, sin_ref = refs[:2]
        refs = refs[2:]
    if want_q:
        q_ref = refs[0]
        refs = refs[1:]
    k_ref, vt_ref = refs

    tm = h_ref.shape[1]
    sub = min(tm, QKV_SUB_ROWS)
    chains = []
    for r0 in range(0, tm, sub):
        rows = slice(r0, r0 + sub)
        a = _norm_mod(h_ref[0, rows, :], g_ref[...], mod_ref[0, 0:1, :], mod_ref[0, 1:2, :]).astype(BF16)
        yq = _dot(a, w_ref[:, :Q_WIDTH]) if want_q else None
        yk = _dot(a, w_ref[:, Q_WIDTH:Q_WIDTH + KV_WIDTH])
        yv = _dot(a, w_ref[:, Q_WIDTH + KV_WIDTH:])
        chains.append((rows, yq, yk, yv))

    for rows, yq, yk, yv in chains:
        def heads(y, gain, out_ref):
            for hd in range(y.shape[1] // HEAD_DIM):
                t = _rms(y[:, hd * HEAD_DIM:(hd + 1) * HEAD_DIM], gain)
                if rope:
                    t = t * cos_ref[rows, :] + pltpu.roll(t, HEAD_DIM // 2, axis=1) * sin_ref[rows, :]
                out_ref[0, rows, hd * HEAD_DIM:(hd + 1) * HEAD_DIM] = t.astype(out_ref.dtype)

        if want_q:
            heads(yq, qg_ref[...], q_ref)
        heads(yk, kg_ref[...], k_ref)
        for hd in range(N_KV_HEADS):
            vt_ref[0, hd, 0:HEAD_DIM, rows] = yv[:, hd * HEAD_DIM:(hd + 1) * HEAD_DIM].T.astype(vt_ref.dtype)
            vt_ref[0, hd, HEAD_DIM:, rows] = jnp.ones((VT_ROWS - HEAD_DIM, sub), vt_ref.dtype)


def _qkv_call(h, mod, g, w, q_g, k_g, rope_tabs, want_q, tm=1024):
    B, L, D = h.shape
    tm = min(tm, L)
    tok = lambda width: pl.BlockSpec((1, tm, width), lambda b, i: (b, i, 0))
    args = [h, mod, g.reshape(1, D), w, q_g.reshape(1, HEAD_DIM), k_g.reshape(1, HEAD_DIM)]
    specs = [tok(D), _mod_spec(mod), _resident((1, D)), _resident(w.shape),
             _resident((1, HEAD_DIM)), _resident((1, HEAD_DIM))]
    if rope_tabs is not None:
        args += list(rope_tabs)
        specs += [pl.BlockSpec((tm, HEAD_DIM), lambda b, i: (i, 0))] * 2
    out_shape, out_specs = [], []
    if want_q:
        out_shape.append(jax.ShapeDtypeStruct((B, L, Q_WIDTH), BF16))
        out_specs.append(tok(Q_WIDTH))
    out_shape += [jax.ShapeDtypeStruct((B, L, KV_WIDTH), BF16),
                  jax.ShapeDtypeStruct((B, N_KV_HEADS, VT_ROWS, L), BF16)]
    out_specs += [tok(KV_WIDTH), pl.BlockSpec((1, N_KV_HEADS, VT_ROWS, tm), lambda b, i: (b, 0, 0, i))]
    return pl.pallas_call(
        functools.partial(_qkv_kernel, rope=rope_tabs is not None, want_q=want_q),
        grid=(B, L // tm),
        in_specs=specs,
        out_specs=out_specs,
        out_shape=out_shape,
        compiler_params=_params("parallel", "parallel"),
        name="qkv_proj",
    )(*args)


def _flash_kernel(*refs, tq, tk):
    q_ref, o_ref = refs[0], refs[-1]
    sources = [(refs[1 + 2 * s], refs[2 + 2 * s]) for s in range((len(refs) - 2) // 2)]
    q = jnp.concatenate([q_ref[0, :, hd * HEAD_DIM:(hd + 1) * HEAD_DIM] for hd in range(Q_PER_KV)], axis=0)
    c = (HEAD_DIM ** -0.5) * LOG2E
    nt = (((1,), (1,)), ((), ()))

    def block_size(k_ref):
        return min(tk, k_ref.shape[1])

    def finish(acc):
        o = (acc[:HEAD_DIM] / acc[HEAD_DIM:HEAD_DIM + 1]).T
        for hd in range(Q_PER_KV):
            o_ref[0, :, hd * HEAD_DIM:(hd + 1) * HEAD_DIM] = o[hd * tq:(hd + 1) * tq].astype(o_ref.dtype)

    blocks = [(k_ref, vt_ref, start, block_size(k_ref)) for k_ref, vt_ref in sources
              for start in range(0, k_ref.shape[1], block_size(k_ref))]
    acc = jnp.zeros((VT_ROWS, q.shape[0]), F32)
    excess = jnp.zeros((1, q.shape[0]), F32)
    for j, (k_ref, vt_ref, start, size) in enumerate(blocks):
        st = jax.lax.dot_general(k_ref[0, start:start + size, :], q, nt, preferred_element_type=F32)
        block_max = jnp.max(st, axis=0, keepdims=True)
        if j == 0:
            shift = block_max
        else:
            excess = jnp.maximum(excess, (block_max - shift) * c)
        p = jnp.exp2((st - shift) * c).astype(BF16)
        acc = acc + _dot(vt_ref[0, 0, :, start:start + size], p)
        if 0 < j < len(blocks) - 1:
            new_shift = jnp.maximum(shift, block_max)
            acc = acc * jnp.exp2((shift - new_shift) * c)
            shift = new_shift
    finish(acc)

    @pl.when(jnp.max(excess) > FLASH_MAX_EXCESS)
    def _():
        carry = (jnp.full((1, q.shape[0]), -jnp.inf, F32), jnp.zeros((VT_ROWS, q.shape[0]), F32))
        for k_ref, vt_ref in sources:
            size = block_size(k_ref)

            def body(j, carry, k_ref=k_ref, vt_ref=vt_ref, size=size):
                m, acc = carry
                start = pl.multiple_of(j * size, size)
                st = jax.lax.dot_general(k_ref[0, pl.ds(start, size), :], q, nt, preferred_element_type=F32)
                m_next = jnp.maximum(m, jnp.max(st, axis=0, keepdims=True))
                p = jnp.exp2((st - m_next) * c).astype(BF16)
                acc = acc * jnp.exp2((m - m_next) * c) + _dot(vt_ref[0, 0, :, pl.ds(start, size)], p)
                return m_next, acc

            carry = jax.lax.fori_loop(0, k_ref.shape[1] // size, body, carry)
        finish(carry[1])


def _flash_call(q, kv_sources, tq=512, tk=1024):
    B, L, _ = q.shape
    args, specs = [q], [pl.BlockSpec((1, tq, Q_PER_KV * HEAD_DIM), lambda b, kh, i: (b, i, kh))]
    for k, vt in kv_sources:
        Lk = k.shape[1]
        assert Lk % min(tk, Lk) == 0
        args += [k, vt]
        specs += [pl.BlockSpec((1, Lk, HEAD_DIM), lambda b, kh, i: (b, 0, kh)),
                  pl.BlockSpec((1, 1, VT_ROWS, Lk), lambda b, kh, i: (b, kh, 0, 0))]
    return pl.pallas_call(
        functools.partial(_flash_kernel, tq=tq, tk=tk),
        grid=(B, N_KV_HEADS, L // tq),
        in_specs=specs,
        out_specs=pl.BlockSpec((1, tq, Q_PER_KV * HEAD_DIM), lambda b, kh, i: (b, i, kh)),
        out_shape=jax.ShapeDtypeStruct(q.shape, BF16),
        compiler_params=_params("parallel", "parallel", "parallel"),
        name="flash_gqa",
    )(*args)


def _gmlp_kernel(h_ref, mod_ref, g_ref, win_ref, lng_ref, lnb_ref, ws_ref, bst_ref, o_ref, *, tm):
    sub = min(tm, GMLP_SUB_ROWS)
    starts = range(0, tm, sub)
    a = [_norm_mod(h_ref[0, r0:r0 + sub, :], g_ref[...], mod_ref[0, 0:1, :], mod_ref[0, 1:2, :]).astype(BF16)
         for r0 in starts]
    zv = [_dot(a_c, win_ref[:, GMLP_HALF:]) for a_c in a]
    zu = [_dot(a_c, win_ref[:, :GMLP_HALF]) for a_c in a]
    for c, r0 in enumerate(starts):
        v = jax.nn.gelu(zv[c], approximate=True)
        mu = jnp.mean(v, axis=-1, keepdims=True)
        vc = v - mu
        v = vc * jax.lax.rsqrt(jnp.mean(vc * vc, axis=-1, keepdims=True) + EPS) * lng_ref[...] + lnb_ref[...]
        v = v.astype(BF16)
        u = jax.nn.gelu(zu[c], approximate=True)
        for ch in range(sub // CHUNK):
            rows = slice(ch * CHUNK, (ch + 1) * CHUNK)
            out_rows = slice(r0 + ch * CHUNK, r0 + (ch + 1) * CHUNK)
            for grp in range(GMLP_GROUPS):
                cols = slice(grp * GMLP_GW, (grp + 1) * GMLP_GW)
                sv = _dot(ws_ref[grp], v[rows, cols]) + bst_ref[:, grp:grp + 1]
                o_ref[0, out_rows, cols] = (u[rows, cols] * sv).astype(o_ref.dtype)


def _gmlp_call(h, mod, g, w_in, ln_g, ln_b, ws, bs, tm=1024):
    B, L, D = h.shape
    return pl.pallas_call(
        functools.partial(_gmlp_kernel, tm=tm),
        grid=(B, L // tm),
        in_specs=[
            pl.BlockSpec((1, tm, D), lambda b, i: (b, i, 0)),
            _mod_spec(mod),
            _resident((1, D)),
            _resident(w_in.shape),
            _resident((1, GMLP_HALF)),
            _resident((1, GMLP_HALF)),
            _resident(ws.shape),
            _resident((CHUNK, GMLP_GROUPS)),
        ],
        out_specs=pl.BlockSpec((1, tm, GMLP_HALF), lambda b, i: (b, i, 0)),
        out_shape=jax.ShapeDtypeStruct((B, L, GMLP_HALF), BF16),
        compiler_params=_params("parallel", "parallel"),
        name="gmlp_gate",
    )(h, mod, g.reshape(1, D), w_in, ln_g.reshape(1, GMLP_HALF), ln_b.reshape(1, GMLP_HALF), ws, bs.T)


def kernel(x, c, ctx, c_ctx, ada_w, ada_b, norm_g, mlp_w1, mlp_w2, pool_w, pool_scale, attn_w_qkv, attn_w_o,
           attn_q_g, attn_k_g, gm_w_in, gm_ln_g, gm_ln_b, gm_ws, gm_bs, gm_w_out, final_g):
    B, S, D = x.shape
    last_ctx_read = max([i for i in range(DEPTH) if i % N_MIXERS == 1], default=-1)

    pad = (-(B + 1)) % 8
    s_in = jnp.concatenate([c, c_ctx[None, :], jnp.zeros((pad, D), F32)], axis=0)
    mods = _ada_call(s_in, ada_w, ada_b)
    mod_lat = mods[:, :B].reshape(DEPTH, B, 6, D)
    mod_ctx = mods[:, B:B + 1].reshape(DEPTH, 1, 6, D)

    def layer_weight_stacks(i):
        kind, j = i % N_MIXERS, i // N_MIXERS
        mixer = {0: [(pool_w, j, 0)], 1: [(attn_w_qkv, j, N_HEADS + N_KV_HEADS), (attn_w_o, j, 0)],
                 2: [(gm_w_in, j, 0), (gm_ws, j, 0), (gm_w_out, j, 0)]}[kind]
        return mixer + [(mlp_w1, i, 0), (mlp_w2, i, 0)]

    h_lat, h_ctx = x, ctx
    weights = [_cast_call(*entry) for entry in layer_weight_stacks(0)]
    for i in range(DEPTH):
        kind, j = i % N_MIXERS, i // N_MIXERS
        ctx_in = i <= last_ctx_read
        ctx_out = i < last_ctx_read
        *mixer_w, w1, w2 = weights
        nxt = layer_weight_stacks(i + 1) if i + 1 < DEPTH else ()
        fin = final_g if i == DEPTH - 1 else None
        if kind == 0:
            pw, = mixer_w
            if ctx_out:
                h_ctx, _ = _pool_mlp_call(h_ctx, mod_ctx[i], norm_g[i, 0], pw, pool_scale[j], norm_g[i, 1], w1, w2)
            h_lat, weights = _pool_mlp_call(h_lat, mod_lat[i], norm_g[i, 0], pw, pool_scale[j], norm_g[i, 1], w1, w2,
                                            final_g=fin, casts=nxt)
        elif kind == 1:
            wqkv, wo = mixer_w
            q_g, k_g = (g_[j].reshape(2, 2, HEAD_DIM // 4).transpose(1, 0, 2).reshape(HEAD_DIM)
                        for g_ in (attn_q_g, attn_k_g))
            q, k_l, vt_l = _qkv_call(h_lat, mod_lat[i], norm_g[i, 0], wqkv, q_g, k_g, _rope_tables(S), True)
            k_c, vt_c = _qkv_call(h_ctx, mod_ctx[i], norm_g[i, 0], wqkv, q_g, k_g, None, False)
            o = _flash_call(q, [(k_l, vt_l), (k_c, vt_c)])
            if ctx_out:
                raise NotImplementedError("context stream output of an attention layer")
            h_lat, weights = _mlp_call(h_lat, mod_lat[i], norm_g[i, 1], w1, w2, proj=(o, wo), final_g=fin, casts=nxt)
        else:
            w_in, ws, w_out = mixer_w
            t = _gmlp_call(h_lat, mod_lat[i], norm_g[i, 0], w_in, gm_ln_g[j], gm_ln_b[j], ws, gm_bs[j])
            if ctx_out:
                raise NotImplementedError("context stream output of a gMLP layer")
            h_lat, weights = _mlp_call(h_lat, mod_lat[i], norm_g[i, 1], w1, w2, proj=(t, w_out), final_g=fin,
                                       casts=nxt)
    return h_lat
```

```python
import functools

import jax
import jax.numpy as jnp
import numpy as np
from jax.experimental import pallas as pl
from jax.experimental.pallas import tpu as pltpu

D_MODEL = 1024
DEPTH = 4
N_MIXERS = 3
GRID_W = 64
EPS = 1e-6
POOL_WINDOWS = (2, 4, 8, 16)
POOL_GW = D_MODEL // len(POOL_WINDOWS)
POOL_HALO = 16
HEAD_DIM = 128
N_HEADS = D_MODEL // HEAD_DIM
N_KV_HEADS = N_HEADS // 2
Q_PER_KV = N_HEADS // N_KV_HEADS
Q_WIDTH = N_HEADS * HEAD_DIM
KV_WIDTH = N_KV_HEADS * HEAD_DIM
VT_ROWS = HEAD_DIM + 16
ROPE_BASE = 10000.0
CHUNK = 128
GMLP_HALF = 2 * D_MODEL
GMLP_GROUPS = 8
GMLP_GW = GMLP_HALF // GMLP_GROUPS
D_FF = 4 * D_MODEL
LOG2E = 1.4426950408889634
FLASH_MAX_EXCESS = 64.0
GMLP_SUB_ROWS = 256
QKV_SUB_ROWS = 256
CAST_BLOCK_ELEMS = 1024 * 1024

VMEM_LIMIT_BYTES = 56 * 1024 * 1024
BF16 = jnp.bfloat16
F32 = jnp.float32


def _params(*semantics):
    return pltpu.CompilerParams(dimension_semantics=semantics, vmem_limit_bytes=VMEM_LIMIT_BYTES)


def _resident(shape):
    zeros = (0,) * len(shape)
    return pl.BlockSpec(shape, lambda *_: zeros, pipeline_mode=pl.Buffered(1))


def _mod_spec(mod):
    if mod.shape[0] == 1:
        return pl.BlockSpec((1, 6, D_MODEL), lambda b, i: (0, 0, 0))
    return pl.BlockSpec((1, 6, D_MODEL), lambda b, i: (b, 0, 0))


def _rms(x, g):
    return x * jax.lax.rsqrt(jnp.mean(x * x, axis=-1, keepdims=True) + EPS) * g


def _norm_mod(x, g, shift, scale):
    return _rms(x, g) * (1.0 + scale) + shift


def _dot(a, b):
    return jnp.dot(a, b, preferred_element_type=F32)


def _cast_block(w, permute_heads):
    if permute_heads:
        heads = [_permute_head_lanes(w[:, hd * HEAD_DIM:(hd + 1) * HEAD_DIM]) for hd in range(permute_heads)]
        w = jnp.concatenate(heads + [w[:, permute_heads * HEAD_DIM:]], axis=-1)
    return w.astype(BF16)


def _cast_kernel(w_ref, o_ref, *, permute_heads):
    o_ref[...] = _cast_block(w_ref[0], permute_heads)


def _cast_call(w_stack, layer, permute_heads=0):
    shape = w_stack.shape[1:]
    C = shape[-1]
    R = int(np.prod(shape[:-1]))
    br = min(R, CAST_BLOCK_ELEMS // C)
    out = pl.pallas_call(
        functools.partial(_cast_kernel, permute_heads=permute_heads),
        grid=(R // br,),
        in_specs=[pl.BlockSpec((1, br, C), lambda r: (layer, r, 0))],
        out_specs=pl.BlockSpec((br, C), lambda r: (r, 0)),
        out_shape=jax.ShapeDtypeStruct((R, C), BF16),
        compiler_params=_params("parallel"),
        name="cast_bf16",
    )(w_stack.reshape(w_stack.shape[0], R, C))
    return out.reshape(shape)


def _side_casts(casts, n_steps, step_of):
    args, in_specs, out_shapes, out_specs, shapes = [], [], [], [], []
    for w_stack, layer, _ in casts:
        shape = w_stack.shape[1:]
        C = shape[-1]
        R = int(np.prod(shape[:-1]))
        assert R % (n_steps * 16) == 0
        br = R // n_steps
        args.append(w_stack.reshape(w_stack.shape[0], R, C))
        in_specs.append(pl.BlockSpec((1, br, C), lambda *idx, layer=layer: (layer, step_of(*idx), 0)))
        out_shapes.append(jax.ShapeDtypeStruct((R, C), BF16))
        out_specs.append(pl.BlockSpec((br, C), lambda *idx: (step_of(*idx), 0)))
        shapes.append(shape)
    return args, in_specs, out_shapes, out_specs, shapes


def _run_side_casts(in_refs, out_refs, permute_heads):
    for w_ref, o_ref, heads in zip(in_refs, out_refs, permute_heads):
        o_ref[...] = _cast_block(w_ref[0], heads)


def _ada_kernel(s_ref, w_ref, b_ref, o_ref):
    s = s_ref[...]
    s = s * jax.nn.sigmoid(s)
    o_ref[0] = _dot(s.astype(BF16), w_ref[0].astype(BF16)) + b_ref[0]


def _ada_call(s_in, ada_w, ada_b):
    rows = s_in.shape[0]
    tn = 1536
    return pl.pallas_call(
        _ada_kernel,
        grid=(DEPTH, 6 * D_MODEL // tn),
        in_specs=[
            pl.BlockSpec((rows, D_MODEL), lambda l, j: (0, 0)),
            pl.BlockSpec((1, D_MODEL, tn), lambda l, j: (l, 0, j)),
            pl.BlockSpec((1, 1, tn), lambda l, j: (l, 0, j)),
        ],
        out_specs=pl.BlockSpec((1, rows, tn), lambda l, j: (l, 0, j)),
        out_shape=jax.ShapeDtypeStruct((DEPTH, rows, 6 * D_MODEL), F32),
        compiler_params=_params("arbitrary", "arbitrary"),
        name="ada_mod",
    )(s_in, ada_w, ada_b.reshape(DEPTH, 1, 6 * D_MODEL))


def _pool_window(a_ext, a_main, pos, seq_len, grp):
    n_ext, n_main = a_ext.shape[0], a_main.shape[0]
    win = POOL_WINDOWS[grp]
    cols = slice(grp * POOL_GW, (grp + 1) * POOL_GW)
    lo_off, hi_off = -(win // 2), win - win // 2
    s = a_ext[:, cols]
    if hi_off > 1:
        s = pltpu.roll(s, n_ext - (hi_off - 1), axis=0)
    k = 1
    while k < win:
        s = s + pltpu.roll(s, k, axis=0)
        k *= 2
    s = s[POOL_HALO:POOL_HALO + n_main]
    cnt = (jnp.minimum(pos + hi_off, seq_len) - jnp.maximum(pos + lo_off, 0)).astype(F32)
    return (s / cnt - a_main[:, cols]).astype(BF16)


def _mlp_rows(x, mod_ref, g_ref, w1_ref, w2_ref, fg_ref):
    m = _norm_mod(x, g_ref[...], mod_ref[0, 3:4, :], mod_ref[0, 4:5, :]).astype(BF16)
    a = jnp.maximum(_dot(m, w1_ref[...]), 0.0)
    a = (a * a).astype(BF16)
    y = x + mod_ref[0, 5:6, :] * _dot(a, w2_ref[...])
    return y if fg_ref is None else _rms(y, fg_ref[...])


def _mlp_kernel(*refs, has_proj, has_final, cast_heads):
    refs = list(refs)
    h_ref, mod_ref, g_ref = refs[:3]
    refs = refs[3:]
    if has_proj:
        t_ref, wp_ref = refs[:2]
        refs = refs[2:]
    w1_ref, w2_ref = refs[:2]
    refs = refs[2:]
    fg_ref = None
    if has_final:
        fg_ref = refs[0]
        refs = refs[1:]
    n_cast = len(cast_heads)
    cast_in, o_ref, cast_out = refs[:n_cast], refs[n_cast], refs[n_cast + 1:]

    x = h_ref[0]
    if has_proj:
        x = x + mod_ref[0, 2:3, :] * _dot(t_ref[0], wp_ref[...])
    o_ref[0] = _mlp_rows(x, mod_ref, g_ref, w1_ref, w2_ref, fg_ref)
    _run_side_casts(cast_in, cast_out, cast_heads)


def _mlp_call(h, mod, g, w1, w2, proj=None, final_g=None, casts=(), tm=512):
    B, L, D = h.shape
    tm = min(tm, L)
    n = L // tm
    tok = lambda width: pl.BlockSpec((1, tm, width), lambda b, i: (b, i, 0))
    args = [h, mod, g.reshape(1, D)]
    specs = [tok(D), _mod_spec(mod), _resident((1, D))]
    if proj is not None:
        t, wp = proj
        args += [t, wp]
        specs += [tok(t.shape[-1]), _resident(wp.shape)]
    args += [w1, w2]
    specs += [_resident(w1.shape), _resident(w2.shape)]
    if final_g is not None:
        args.append(final_g.reshape(1, D))
        specs.append(_resident((1, D)))
    c_args, c_in, c_shapes, c_out, c_orig = _side_casts(casts, B * n, lambda b, i: b * n + i)
    outs = pl.pallas_call(
        functools.partial(_mlp_kernel, has_proj=proj is not None, has_final=final_g is not None,
                          cast_heads=tuple(c[2] for c in casts)),
        grid=(B, n),
        in_specs=specs + c_in,
        out_specs=[tok(D)] + c_out,
        out_shape=[jax.ShapeDtypeStruct(h.shape, F32)] + c_shapes,
        compiler_params=_params("parallel", "parallel"),
        name="mlp",
    )(*args, *c_args)
    return outs[0], [w.reshape(s) for w, s in zip(outs[1:], c_orig)]


def _pool_mlp_kernel(*refs, has_final, seq_len, tiles_per_seq, n_tiles, cast_heads):
    refs = list(refs)
    h_ref, hp_ref, hn_ref, moda_ref, g0_ref, pw_ref, psc_ref, modb_ref, g_ref, w1_ref, w2_ref = refs[:11]
    refs = refs[11:]
    fg_ref = None
    if has_final:
        fg_ref = refs[0]
        refs = refs[1:]
    n_cast = len(cast_heads)
    cast_in, o_ref, cast_out, x1_ref = refs[:n_cast], refs[n_cast], refs[n_cast + 1:-1], refs[-1]
    s = pl.program_id(0)
    tm = h_ref.shape[1]

    @pl.when(s == 0)
    def _():
        x1_ref[1] = jnp.zeros(x1_ref.shape[1:], F32)

    i = jnp.minimum(s, n_tiles - 1) % tiles_per_seq

    def stages(write_slot, read_slot):
        xb = x1_ref[read_slot]
        m = _norm_mod(xb, g_ref[...], modb_ref[0, 3:4, :], modb_ref[0, 4:5, :]).astype(BF16)
        norm = functools.partial(_norm_mod, g=g0_ref[...], shift=moda_ref[0, 0:1, :], scale=moda_ref[0, 1:2, :])
        x = h_ref[0]
        a_main = norm(x)
        top = jnp.where(i > 0, norm(hp_ref[0]), 0.0)
        bottom = jnp.where(i < tiles_per_seq - 1, norm(hn_ref[0]), 0.0)
        pos = i * tm + jax.lax.broadcasted_iota(jnp.int32, (tm, 1), 0)
        a_ext = jnp.concatenate([top, a_main, bottom], axis=0)

        n_grp = len(POOL_WINDOWS)
        fc = D_FF // n_grp
        acc, ys = None, []
        for grp in range(n_grp):
            p = _pool_window(a_ext, a_main, pos, seq_len, grp)
            a = jnp.maximum(_dot(m, w1_ref[:, grp * fc:(grp + 1) * fc]), 0.0)
            d = _dot((a * a).astype(BF16), w2_ref[grp * fc:(grp + 1) * fc, :])
            acc = d if acc is None else acc + d
            ys.append(_dot(p, pw_ref[grp]))
        y = xb + modb_ref[0, 5:6, :] * acc
        o_ref[0] = y if fg_ref is None else _rms(y, fg_ref[...])
        x1_ref[write_slot] = x + moda_ref[0, 2:3, :] * (jnp.concatenate(ys, axis=-1) * psc_ref[...])
        _run_side_casts(cast_in, cast_out, cast_heads)

    pl.when(s % 2 == 0)(functools.partial(stages, 0, 1))
    pl.when(s % 2 == 1)(functools.partial(stages, 1, 0))


def _pool_mlp_call(h, mod, g0, pool_w, pool_scale, g, w1, w2, final_g=None, casts=(), tm=512):
    B, L, D = h.shape
    tm = min(tm, L)
    tps = L // tm
    n_tiles = B * tps
    r = tm // POOL_HALO
    last = L // POOL_HALO - 1
    shared_mod = mod.shape[0] == 1

    def tile_a(s):
        t = jnp.minimum(s, n_tiles - 1)
        return t // tps, t % tps

    def tile_b(s):
        t = jnp.maximum(s - 1, 0)
        return t // tps, t % tps

    def mod_spec(tile):
        return pl.BlockSpec((1, 6, D), lambda s: (0 if shared_mod else tile(s)[0], 0, 0))

    args = [h, h, h, mod, g0.reshape(1, D), pool_w, pool_scale.reshape(1, D), mod, g.reshape(1, D), w1, w2]
    specs = [
        pl.BlockSpec((1, tm, D), lambda s: (*tile_a(s), 0)),
        pl.BlockSpec((1, POOL_HALO, D), lambda s: (tile_a(s)[0], jnp.maximum(tile_a(s)[1] * r - 1, 0), 0)),
        pl.BlockSpec((1, POOL_HALO, D), lambda s: (tile_a(s)[0], jnp.minimum((tile_a(s)[1] + 1) * r, last), 0)),
        mod_spec(tile_a), _resident((1, D)), _resident(pool_w.shape), _resident((1, D)),
        mod_spec(tile_b), _resident((1, D)), _resident(w1.shape), _resident(w2.shape),
    ]
    if final_g is not None:
        args.append(final_g.reshape(1, D))
        specs.append(_resident((1, D)))
    c_args, c_in, c_shapes, c_out, c_orig = _side_casts(casts, n_tiles, lambda s: jnp.minimum(s, n_tiles - 1))
    outs = pl.pallas_call(
        functools.partial(_pool_mlp_kernel, has_final=final_g is not None, seq_len=L, tiles_per_seq=tps,
                          n_tiles=n_tiles, cast_heads=tuple(c[2] for c in casts)),
        grid=(n_tiles + 1,),
        in_specs=specs + c_in,
        out_specs=[pl.BlockSpec((1, tm, D), lambda s: (*tile_b(s), 0))] + c_out,
        out_shape=[jax.ShapeDtypeStruct(h.shape, F32)] + c_shapes,
        scratch_shapes=[pltpu.VMEM((2, tm, D), F32)],
        compiler_params=_params("arbitrary"),
        name="pool_mlp",
    )(*args, *c_args)
    return outs[0], [w.reshape(s) for w, s in zip(outs[1:], c_orig)]


def _permute_head_lanes(t):
    q = HEAD_DIM // 4
    lane = jax.lax.broadcasted_iota(jnp.int32, t.shape, 1)
    from_right = pltpu.roll(t, HEAD_DIM - q, axis=1)
    from_left = pltpu.roll(t, q, axis=1)
    return jnp.where(lane < q, t, jnp.where(lane < 2 * q, from_right, jnp.where(lane < 3 * q, from_left, t)))


def _rope_tables(L):
    rows_n = L // GRID_W
    q = HEAD_DIM // 4
    half = HEAD_DIM // 2
    inv = (np.float32(ROPE_BASE) ** (-np.arange(0, half, 2, dtype=np.float32) / np.float32(half))).astype(np.float32)
    ang_r = (np.arange(rows_n, dtype=np.float32)[:, None] * inv[None, :]).astype(np.float64)
    ang_c = (np.arange(GRID_W, dtype=np.float32)[:, None] * inv[None, :]).astype(np.float64)

    def table(fn, signs):
        by_row = np.zeros((rows_n, HEAD_DIM), np.float32)
        by_col = np.zeros((GRID_W, HEAD_DIM), np.float32)
        for quarter, sign in enumerate(signs):
            tab, ang = (by_row, ang_r) if quarter % 2 == 0 else (by_col, ang_c)
            tab[:, quarter * q:(quarter + 1) * q] = sign * fn(ang)
        return (jnp.broadcast_to(jnp.asarray(by_row)[:, None, :], (rows_n, GRID_W, HEAD_DIM))
                + jnp.asarray(by_col)[None, :, :]).reshape(L, HEAD_DIM)

    return table(np.cos, (1.0, 1.0, 1.0, 1.0)), table(np.sin, (-1.0, -1.0, 1.0, 1.0))


def _qkv_kernel(*refs, rope, want_q):
    refs = list(refs)
    h_ref, mod_ref, g_ref, w_ref, qg_ref, kg_ref = refs[:6]
    refs = refs[6:]
    if rope:
        cos_ref, sin_ref = refs[:2]
        refs = refs[2:]
    if want_q:
        q_ref = refs[0]
        refs = refs[1:]
    k_ref, vt_ref = refs

    tm = h_ref.shape[1]
    sub = min(tm, QKV_SUB_ROWS)
    chains = []
    for r0 in range(0, tm, sub):
        rows = slice(r0, r0 + sub)
        a = _norm_mod(h_ref[0, rows, :], g_ref[...], mod_ref[0, 0:1, :], mod_ref[0, 1:2, :]).astype(BF16)
        yq = _dot(a, w_ref[:, :Q_WIDTH]) if want_q else None
        yk = _dot(a, w_ref[:, Q_WIDTH:Q_WIDTH + KV_WIDTH])
        yv = _dot(a, w_ref[:, Q_WIDTH + KV_WIDTH:])
        chains.append((rows, yq, yk, yv))

    for rows, yq, yk, yv in chains:
        def heads(y, gain, out_ref):
            for hd in range(y.shape[1] // HEAD_DIM):
                t = _rms(y[:, hd * HEAD_DIM:(hd + 1) * HEAD_DIM], gain)
                if rope:
                    t = t * cos_ref[rows, :] + pltpu.roll(t, HEAD_DIM // 2, axis=1) * sin_ref[rows, :]
                out_ref[0, rows, hd * HEAD_DIM:(hd + 1) * HEAD_DIM] = t.astype(out_ref.dtype)

        if want_q:
            heads(yq, qg_ref[...], q_ref)
        heads(yk, kg_ref[...], k_ref)
        for hd in range(N_KV_HEADS):
            vt_ref[0, hd, 0:HEAD_DIM, rows] = yv[:, hd * HEAD_DIM:(hd + 1) * HEAD_DIM].T.astype(vt_ref.dtype)
            vt_ref[0, hd, HEAD_DIM:, rows] = jnp.ones((VT_ROWS - HEAD_DIM, sub), vt_ref.dtype)


def _qkv_call(h, mod, g, w, q_g, k_g, rope_tabs, want_q, tm=1024):
    B, L, D = h.shape
    tm = min(tm, L)
    tok = lambda width: pl.BlockSpec((1, tm, width), lambda b, i: (b, i, 0))
    args = [h, mod, g.reshape(1, D), w, q_g.reshape(1, HEAD_DIM), k_g.reshape(1, HEAD_DIM)]
    specs = [tok(D), _mod_spec(mod), _resident((1, D)), _resident(w.shape),
             _resident((1, HEAD_DIM)), _resident((1, HEAD_DIM))]
    if rope_tabs is not None:
        args += list(rope_tabs)
        specs += [pl.BlockSpec((tm, HEAD_DIM), lambda b, i: (i, 0))] * 2
    out_shape, out_specs = [], []
    if want_q:
        out_shape.append(jax.ShapeDtypeStruct((B, L, Q_WIDTH), BF16))
        out_specs.append(tok(Q_WIDTH))
    out_shape += [jax.ShapeDtypeStruct((B, L, KV_WIDTH), BF16),
                  jax.ShapeDtypeStruct((B, N_KV_HEADS, VT_ROWS, L), BF16)]
    out_specs += [tok(KV_WIDTH), pl.BlockSpec((1, N_KV_HEADS, VT_ROWS, tm), lambda b, i: (b, 0, 0, i))]
    return pl.pallas_call(
        functools.partial(_qkv_kernel, rope=rope_tabs is not None, want_q=want_q),
        grid=(B, L // tm),
        in_specs=specs,
        out_specs=out_specs,
        out_shape=out_shape,
        compiler_params=_params("parallel", "parallel"),
        name="qkv_proj",
    )(*args)


def _flash_kernel(*refs, tq, tk):
    q_ref, o_ref = refs[0], refs[-1]
    sources = [(refs[1 + 2 * s], refs[2 + 2 * s]) for s in range((len(refs) - 2) // 2)]
    q = jnp.concatenate([q_ref[0, :, hd * HEAD_DIM:(hd + 1) * HEAD_DIM] for hd in range(Q_PER_KV)], axis=0)
    c = (HEAD_DIM ** -0.5) * LOG2E
    nt = (((1,), (1,)), ((), ()))

    def block_size(k_ref):
        return min(tk, k_ref.shape[1])

    def finish(acc):
        o = (acc[:HEAD_DIM] / acc[HEAD_DIM:HEAD_DIM + 1]).T
        for hd in range(Q_PER_KV):
            o_ref[0, :, hd * HEAD_DIM:(hd + 1) * HEAD_DIM] = o[hd * tq:(hd + 1) * tq].astype(o_ref.dtype)

    blocks = [(k_ref, vt_ref, start, block_size(k_ref)) for k_ref, vt_ref in sources
              for start in range(0, k_ref.shape[1], block_size(k_ref))]
    acc = jnp.zeros((VT_ROWS, q.shape[0]), F32)
    excess = jnp.zeros((1, q.shape[0]), F32)
    for j, (k_ref, vt_ref, start, size) in enumerate(blocks):
        st = jax.lax.dot_general(k_ref[0, start:start + size, :], q, nt, preferred_element_type=F32)
        block_max = jnp.max(st, axis=0, keepdims=True)
        if j == 0:
            shift = block_max
        else:
            excess = jnp.maximum(excess, (block_max - shift) * c)
        p = jnp.exp2((st - shift) * c).astype(BF16)
        acc = acc + _dot(vt_ref[0, 0, :, start:start + size], p)
        if 0 < j < len(blocks) - 1:
            new_shift = jnp.maximum(shift, block_max)
            acc = acc * jnp.exp2((shift - new_shift) * c)
            shift = new_shift
    finish(acc)

    @pl.when(jnp.max(excess) > FLASH_MAX_EXCESS)
    def _():
        carry = (jnp.full((1, q.shape[0]), -jnp.inf, F32), jnp.zeros((VT_ROWS, q.shape[0]), F32))
        for k_ref, vt_ref in sources:
            size = block_size(k_ref)

            def body(j, carry, k_ref=k_ref, vt_ref=vt_ref, size=size):
                m, acc = carry
                start = pl.multiple_of(j * size, size)
                st = jax.lax.dot_general(k_ref[0, pl.ds(start, size), :], q, nt, preferred_element_type=F32)
                m_next = jnp.maximum(m, jnp.max(st, axis=0, keepdims=True))
                p = jnp.exp2((st - m_next) * c).astype(BF16)
                acc = acc * jnp.exp2((m - m_next) * c) + _dot(vt_ref[0, 0, :, pl.ds(start, size)], p)
                return m_next, acc

            carry = jax.lax.fori_loop(0, k_ref.shape[1] // size, body, carry)
        finish(carry[1])


def _flash_call(q, kv_sources, tq=512, tk=1024):
    B, L, _ = q.shape
    args, specs = [q], [pl.BlockSpec((1, tq, Q_PER_KV * HEAD_DIM), lambda b, kh, i: (b, i, kh))]
    for k, vt in kv_sources:
        Lk = k.shape[1]
        assert Lk % min(tk, Lk) == 0
        args += [k, vt]
        specs += [pl.BlockSpec((1, Lk, HEAD_DIM), lambda b, kh, i: (b, 0, kh)),
                  pl.BlockSpec((1, 1, VT_ROWS, Lk), lambda b, kh, i: (b, kh, 0, 0))]
    return pl.pallas_call(
        functools.partial(_flash_kernel, tq=tq, tk=tk),
        grid=(B, N_KV_HEADS, L // tq),
        in_specs=specs,
        out_specs=pl.BlockSpec((1, tq, Q_PER_KV * HEAD_DIM), lambda b, kh, i: (b, i, kh)),
        out_shape=jax.ShapeDtypeStruct(q.shape, BF16),
        compiler_params=_params("parallel", "parallel", "parallel"),
        name="flash_gqa",
    )(*args)


def _gmlp_kernel(h_ref, mod_ref, g_ref, win_ref, lng_ref, lnb_ref, ws_ref, bst_ref, o_ref, *, tm):
    sub = min(tm, GMLP_SUB_ROWS)
    starts = range(0, tm, sub)
    a = [_norm_mod(h_ref[0, r0:r0 + sub, :], g_ref[...], mod_ref[0, 0:1, :], mod_ref[0, 1:2, :]).astype(BF16)
         for r0 in starts]
    zv = [_dot(a_c, win_ref[:, GMLP_HALF:]) for a_c in a]
    zu = [_dot(a_c, win_ref[:, :GMLP_HALF]) for a_c in a]
    for c, r0 in enumerate(starts):
        v = jax.nn.gelu(zv[c], approximate=True)
        mu = jnp.mean(v, axis=-1, keepdims=True)
        vc = v - mu
        v = vc * jax.lax.rsqrt(jnp.mean(vc * vc, axis=-1, keepdims=True) + EPS) * lng_ref[...] + lnb_ref[...]
        v = v.astype(BF16)
        u = jax.nn.gelu(zu[c], approximate=True)
        for ch in range(sub // CHUNK):
            rows = slice(ch * CHUNK, (ch + 1) * CHUNK)
            out_rows = slice(r0 + ch * CHUNK, r0 + (ch + 1) * CHUNK)
            for grp in range(GMLP_GROUPS):
                cols = slice(grp * GMLP_GW, (grp + 1) * GMLP_GW)
                sv = _dot(ws_ref[grp], v[rows, cols]) + bst_ref[:, grp:grp + 1]
                o_ref[0, out_rows, cols] = (u[rows, cols] * sv).astype(o_ref.dtype)


def _gmlp_call(h, mod, g, w_in, ln_g, ln_b, ws, bs, tm=1024):
    B, L, D = h.shape
    return pl.pallas_call(
        functools.partial(_gmlp_kernel, tm=tm),
        grid=(B, L // tm),
        in_specs=[
            pl.BlockSpec((1, tm, D), lambda b, i: (b, i, 0)),
            _mod_spec(mod),
            _resident((1, D)),
            _resident(w_in.shape),
            _resident((1, GMLP_HALF)),
            _resident((1, GMLP_HALF)),
            _resident(ws.shape),
            _resident((CHUNK, GMLP_GROUPS)),
        ],
        out_specs=pl.BlockSpec((1, tm, GMLP_HALF), lambda b, i: (b, i, 0)),
        out_shape=jax.ShapeDtypeStruct((B, L, GMLP_HALF), BF16),
        compiler_params=_params("parallel", "parallel"),
        name="gmlp_gate",
    )(h, mod, g.reshape(1, D), w_in, ln_g.reshape(1, GMLP_HALF), ln_b.reshape(1, GMLP_HALF), ws, bs.T)


def kernel(x, c, ctx, c_ctx, ada_w, ada_b, norm_g, mlp_w1, mlp_w2, pool_w, pool_scale, attn_w_qkv, attn_w_o,
           attn_q_g, attn_k_g, gm_w_in, gm_ln_g, gm_ln_b, gm_ws, gm_bs, gm_w_out, final_g):
    B, S, D = x.shape
    last_ctx_read = max([i for i in range(DEPTH) if i % N_MIXERS == 1], default=-1)

    pad = (-(B + 1)) % 8
    s_in = jnp.concatenate([c, c_ctx[None, :], jnp.zeros((pad, D), F32)], axis=0)
    mods = _ada_call(s_in, ada_w, ada_b)
    mod_lat = mods[:, :B].reshape(DEPTH, B, 6, D)
    mod_ctx = mods[:, B:B + 1].reshape(DEPTH, 1, 6, D)

    def layer_weight_stacks(i):
        kind, j = i % N_MIXERS, i // N_MIXERS
        mixer = {0: [(pool_w, j, 0)], 1: [(attn_w_qkv, j, N_HEADS + N_KV_HEADS), (attn_w_o, j, 0)],
                 2: [(gm_w_in, j, 0), (gm_ws, j, 0), (gm_w_out, j, 0)]}[kind]
        return mixer + [(mlp_w1, i, 0), (mlp_w2, i, 0)]

    h_lat, h_ctx = x, ctx
    weights = [_cast_call(*entry) for entry in layer_weight_stacks(0)]
    for i in range(DEPTH):
        kind, j = i % N_MIXERS, i // N_MIXERS
        ctx_in = i <= last_ctx_read
        ctx_out = i < last_ctx_read
        *mixer_w, w1, w2 = weights
        nxt = layer_weight_stacks(i + 1) if i + 1 < DEPTH else ()
        fin = final_g if i == DEPTH - 1 else None
        if kind == 0:
            pw, = mixer_w
            if ctx_out:
                h_ctx, _ = _pool_mlp_call(h_ctx, mod_ctx[i], norm_g[i, 0], pw, pool_scale[j], norm_g[i, 1], w1, w2)
            h_lat, weights = _pool_mlp_call(h_lat, mod_lat[i], norm_g[i, 0], pw, pool_scale[j], norm_g[i, 1], w1, w2,
                                            final_g=fin, casts=nxt)
        elif kind == 1:
            wqkv, wo = mixer_w
            q_g, k_g = (g_[j].reshape(2, 2, HEAD_DIM // 4).transpose(1, 0, 2).reshape(HEAD_DIM)
                        for g_ in (attn_q_g, attn_k_g))
            q, k_l, vt_l = _qkv_call(h_lat, mod_lat[i], norm_g[i, 0], wqkv, q_g, k_g, _rope_tables(S), True)
            k_c, vt_c = _qkv_call(h_ctx, mod_ctx[i], norm_g[i, 0], wqkv, q_g, k_g, None, False)
            o = _flash_call(q, [(k_c, vt_c), (k_l, vt_l)])
            if ctx_out:
                raise NotImplementedError("context stream output of an attention layer")
            h_lat, weights = _mlp_call(h_lat, mod_lat[i], norm_g[i, 1], w1, w2, proj=(o, wo), final_g=fin, casts=nxt)
        else:
            w_in, ws, w_out = mixer_w
            t = _gmlp_call(h_lat, mod_lat[i], norm_g[i, 0], w_in, gm_ln_g[j], gm_ln_b[j], ws, gm_bs[j])
            if ctx_out:
                raise NotImplementedError("context stream output of a gMLP layer")
            h_lat, weights = _mlp_call(h_lat, mod_lat[i], norm_g[i, 1], w1, w2, proj=(t, w_out), final_g=fin,
                                       casts=nxt)
    return h_lat
```

```python
import functools

import jax
import jax.numpy as jnp
import numpy as np
from jax.experimental import pallas as pl
from jax.experimental.pallas import tpu as pltpu

D_MODEL = 1024
DEPTH = 4
N_MIXERS = 3
GRID_W = 64
EPS = 1e-6
POOL_WINDOWS = (2, 4, 8, 16)
POOL_GW = D_MODEL // len(POOL_WINDOWS)
POOL_HALO = 16
HEAD_DIM = 128
N_HEADS = D_MODEL // HEAD_DIM
N_KV_HEADS = N_HEADS // 2
Q_PER_KV = N_HEADS // N_KV_HEADS
Q_WIDTH = N_HEADS * HEAD_DIM
KV_WIDTH = N_KV_HEADS * HEAD_DIM
VT_ROWS = HEAD_DIM + 16
ROPE_BASE = 10000.0
CHUNK = 128
GMLP_HALF = 2 * D_MODEL
GMLP_GROUPS = 8
GMLP_GW = GMLP_HALF // GMLP_GROUPS
D_FF = 4 * D_MODEL
LOG2E = 1.4426950408889634
FLASH_MAX_EXCESS = 64.0

MLP_TM = 512
QKV_TM = 1024
QKV_SUB_ROWS = 256
GMLP_TM = 1024
GMLP_SUB_ROWS = 256
FLASH_TQ = 512
FLASH_TK = 1024
ADA_TN = 3072
CAST_BLOCK_ELEMS = 1024 * 1024

VMEM_LIMIT_BYTES = 56 * 1024 * 1024
BF16 = jnp.bfloat16
F32 = jnp.float32


def _params(*semantics):
    return pltpu.CompilerParams(dimension_semantics=semantics, vmem_limit_bytes=VMEM_LIMIT_BYTES)


def _resident(shape):
    zeros = (0,) * len(shape)
    return pl.BlockSpec(shape, lambda *_: zeros, pipeline_mode=pl.Buffered(1))


def _mod_spec(mod):
    if mod.shape[0] == 1:
        return pl.BlockSpec((1, 6, D_MODEL), lambda b, i: (0, 0, 0))
    return pl.BlockSpec((1, 6, D_MODEL), lambda b, i: (b, 0, 0))


def _rms(x, g):
    return x * jax.lax.rsqrt(jnp.mean(x * x, axis=-1, keepdims=True) + EPS) * g


def _norm_mod(x, g, shift, scale):
    return _rms(x, g) * (1.0 + scale) + shift


def _dot(a, b):
    return jnp.dot(a, b, preferred_element_type=F32)


def _cast_block(w, permute_heads):
    if permute_heads:
        heads = [_permute_head_lanes(w[:, hd * HEAD_DIM:(hd + 1) * HEAD_DIM]) for hd in range(permute_heads)]
        w = jnp.concatenate(heads + [w[:, permute_heads * HEAD_DIM:]], axis=-1)
    return w.astype(BF16)


def _cast_kernel(w_ref, o_ref, *, permute_heads):
    o_ref[...] = _cast_block(w_ref[0], permute_heads)


def _cast_call(w_stack, layer, permute_heads=0):
    shape = w_stack.shape[1:]
    C = shape[-1]
    R = int(np.prod(shape[:-1]))
    br = min(R, CAST_BLOCK_ELEMS // C)
    out = pl.pallas_call(
        functools.partial(_cast_kernel, permute_heads=permute_heads),
        grid=(R // br,),
        in_specs=[pl.BlockSpec((1, br, C), lambda r: (layer, r, 0))],
        out_specs=pl.BlockSpec((br, C), lambda r: (r, 0)),
        out_shape=jax.ShapeDtypeStruct((R, C), BF16),
        compiler_params=_params("parallel"),
        name="cast_bf16",
    )(w_stack.reshape(w_stack.shape[0], R, C))
    return out.reshape(shape)


def _side_casts(casts, n_steps, step_of):
    args, in_specs, out_shapes, out_specs, shapes = [], [], [], [], []
    for w_stack, layer, _ in casts:
        shape = w_stack.shape[1:]
        C = shape[-1]
        R = int(np.prod(shape[:-1]))
        assert R % (n_steps * 16) == 0
        br = R // n_steps
        args.append(w_stack.reshape(w_stack.shape[0], R, C))
        in_specs.append(pl.BlockSpec((1, br, C), lambda *idx, layer=layer: (layer, step_of(*idx), 0)))
        out_shapes.append(jax.ShapeDtypeStruct((R, C), BF16))
        out_specs.append(pl.BlockSpec((br, C), lambda *idx: (step_of(*idx), 0)))
        shapes.append(shape)
    return args, in_specs, out_shapes, out_specs, shapes


def _run_side_casts(in_refs, out_refs, permute_heads):
    for w_ref, o_ref, heads in zip(in_refs, out_refs, permute_heads):
        o_ref[...] = _cast_block(w_ref[0], heads)


def _ada_kernel(s_ref, w_ref, b_ref, o_ref):
    s = s_ref[...]
    s = s * jax.nn.sigmoid(s)
    o_ref[0] = _dot(s.astype(BF16), w_ref[0].astype(BF16)) + b_ref[0]


def _ada_call(s_in, ada_w, ada_b):
    rows = s_in.shape[0]
    tn = ADA_TN
    return pl.pallas_call(
        _ada_kernel,
        grid=(DEPTH, 6 * D_MODEL // tn),
        in_specs=[
            pl.BlockSpec((rows, D_MODEL), lambda l, j: (0, 0)),
            pl.BlockSpec((1, D_MODEL, tn), lambda l, j: (l, 0, j)),
            pl.BlockSpec((1, 1, tn), lambda l, j: (l, 0, j)),
        ],
        out_specs=pl.BlockSpec((1, rows, tn), lambda l, j: (l, 0, j)),
        out_shape=jax.ShapeDtypeStruct((DEPTH, rows, 6 * D_MODEL), F32),
        compiler_params=_params("arbitrary", "arbitrary"),
        name="ada_mod",
    )(s_in, ada_w, ada_b.reshape(DEPTH, 1, 6 * D_MODEL))


def _pool_window(a_ext, a_main, pos, seq_len, grp):
    n_ext, n_main = a_ext.shape[0], a_main.shape[0]
    win = POOL_WINDOWS[grp]
    cols = slice(grp * POOL_GW, (grp + 1) * POOL_GW)
    lo_off, hi_off = -(win // 2), win - win // 2
    s = a_ext[:, cols]
    if hi_off > 1:
        s = pltpu.roll(s, n_ext - (hi_off - 1), axis=0)
    k = 1
    while k < win:
        s = s + pltpu.roll(s, k, axis=0)
        k *= 2
    s = s[POOL_HALO:POOL_HALO + n_main]
    cnt = (jnp.minimum(pos + hi_off, seq_len) - jnp.maximum(pos + lo_off, 0)).astype(F32)
    return (s / cnt - a_main[:, cols]).astype(BF16)


def _mlp_rows(x, mod_ref, g_ref, w1_ref, w2_ref, fg_ref):
    m = _norm_mod(x, g_ref[...], mod_ref[0, 3:4, :], mod_ref[0, 4:5, :]).astype(BF16)
    a = jnp.maximum(_dot(m, w1_ref[...]), 0.0)
    a = (a * a).astype(BF16)
    y = x + mod_ref[0, 5:6, :] * _dot(a, w2_ref[...])
    return y if fg_ref is None else _rms(y, fg_ref[...])


def _mlp_kernel(*refs, has_proj, has_final, cast_heads):
    refs = list(refs)
    h_ref, mod_ref, g_ref = refs[:3]
    refs = refs[3:]
    if has_proj:
        t_ref, wp_ref = refs[:2]
        refs = refs[2:]
    w1_ref, w2_ref = refs[:2]
    refs = refs[2:]
    fg_ref = None
    if has_final:
        fg_ref = refs[0]
        refs = refs[1:]
    n_cast = len(cast_heads)
    cast_in, o_ref, cast_out = refs[:n_cast], refs[n_cast], refs[n_cast + 1:]

    x = h_ref[0]
    if has_proj:
        x = x + mod_ref[0, 2:3, :] * _dot(t_ref[0], wp_ref[...])
    o_ref[0] = _mlp_rows(x, mod_ref, g_ref, w1_ref, w2_ref, fg_ref)
    _run_side_casts(cast_in, cast_out, cast_heads)


def _mlp_call(h, mod, g, w1, w2, proj=None, final_g=None, casts=(), tm=MLP_TM):
    B, L, D = h.shape
    tm = min(tm, L)
    n = L // tm
    tok = lambda width: pl.BlockSpec((1, tm, width), lambda b, i: (b, i, 0))
    args = [h, mod, g.reshape(1, D)]
    specs = [tok(D), _mod_spec(mod), _resident((1, D))]
    if proj is not None:
        t, wp = proj
        args += [t, wp]
        specs += [tok(t.shape[-1]), _resident(wp.shape)]
    args += [w1, w2]
    specs += [_resident(w1.shape), _resident(w2.shape)]
    if final_g is not None:
        args.append(final_g.reshape(1, D))
        specs.append(_resident((1, D)))
    c_args, c_in, c_shapes, c_out, c_orig = _side_casts(casts, B * n, lambda b, i: b * n + i)
    outs = pl.pallas_call(
        functools.partial(_mlp_kernel, has_proj=proj is not None, has_final=final_g is not None,
                          cast_heads=tuple(c[2] for c in casts)),
        grid=(B, n),
        in_specs=specs + c_in,
        out_specs=[tok(D)] + c_out,
        out_shape=[jax.ShapeDtypeStruct(h.shape, F32)] + c_shapes,
        compiler_params=_params("parallel", "parallel"),
        name="mlp",
    )(*args, *c_args)
    return outs[0], [w.reshape(s) for w, s in zip(outs[1:], c_orig)]


def _pool_mlp_kernel(*refs, has_final, seq_len, tiles_per_seq, n_tiles, cast_heads):
    refs = list(refs)
    h_ref, hp_ref, hn_ref, moda_ref, g0_ref, pw_ref, psc_ref, modb_ref, g_ref, w1_ref, w2_ref = refs[:11]
    refs = refs[11:]
    fg_ref = None
    if has_final:
        fg_ref = refs[0]
        refs = refs[1:]
    n_cast = len(cast_heads)
    cast_in, o_ref, cast_out, x1_ref = refs[:n_cast], refs[n_cast], refs[n_cast + 1:-1], refs[-1]
    s = pl.program_id(0)
    tm = h_ref.shape[1]

    @pl.when(s == 0)
    def _():
        x1_ref[1] = jnp.zeros(x1_ref.shape[1:], F32)

    i = jnp.minimum(s, n_tiles - 1) % tiles_per_seq

    def stages(write_slot, read_slot):
        xb = x1_ref[read_slot]
        m = _norm_mod(xb, g_ref[...], modb_ref[0, 3:4, :], modb_ref[0, 4:5, :]).astype(BF16)
        norm = functools.partial(_norm_mod, g=g0_ref[...], shift=moda_ref[0, 0:1, :], scale=moda_ref[0, 1:2, :])
        x = h_ref[0]
        a_main = norm(x)
        top = jnp.where(i > 0, norm(hp_ref[0]), 0.0)
        bottom = jnp.where(i < tiles_per_seq - 1, norm(hn_ref[0]), 0.0)
        pos = i * tm + jax.lax.broadcasted_iota(jnp.int32, (tm, 1), 0)
        a_ext = jnp.concatenate([top, a_main, bottom], axis=0)

        n_grp = len(POOL_WINDOWS)
        fc = D_FF // n_grp
        acc, ys = None, []
        for grp in range(n_grp):
            p = _pool_window(a_ext, a_main, pos, seq_len, grp)
            a = jnp.maximum(_dot(m, w1_ref[:, grp * fc:(grp + 1) * fc]), 0.0)
            d = _dot((a * a).astype(BF16), w2_ref[grp * fc:(grp + 1) * fc, :])
            acc = d if acc is None else acc + d
            ys.append(_dot(p, pw_ref[grp]))
        y = xb + modb_ref[0, 5:6, :] * acc
        o_ref[0] = y if fg_ref is None else _rms(y, fg_ref[...])
        x1_ref[write_slot] = x + moda_ref[0, 2:3, :] * (jnp.concatenate(ys, axis=-1) * psc_ref[...])
        _run_side_casts(cast_in, cast_out, cast_heads)

    pl.when(s % 2 == 0)(functools.partial(stages, 0, 1))
    pl.when(s % 2 == 1)(functools.partial(stages, 1, 0))


def _pool_mlp_call(h, mod, g0, pool_w, pool_scale, g, w1, w2, final_g=None, casts=(), tm=MLP_TM):
    B, L, D = h.shape
    tm = min(tm, L)
    tps = L // tm
    n_tiles = B * tps
    r = tm // POOL_HALO
    last = L // POOL_HALO - 1
    shared_mod = mod.shape[0] == 1

    def tile_a(s):
        t = jnp.minimum(s, n_tiles - 1)
        return t // tps, t % tps

    def tile_b(s):
        t = jnp.maximum(s - 1, 0)
        return t // tps, t % tps

    def mod_spec(tile):
        return pl.BlockSpec((1, 6, D), lambda s: (0 if shared_mod else tile(s)[0], 0, 0))

    args = [h, h, h, mod, g0.reshape(1, D), pool_w, pool_scale.reshape(1, D), mod, g.reshape(1, D), w1, w2]
    specs = [
        pl.BlockSpec((1, tm, D), lambda s: (*tile_a(s), 0)),
        pl.BlockSpec((1, POOL_HALO, D), lambda s: (tile_a(s)[0], jnp.maximum(tile_a(s)[1] * r - 1, 0), 0)),
        pl.BlockSpec((1, POOL_HALO, D), lambda s: (tile_a(s)[0], jnp.minimum((tile_a(s)[1] + 1) * r, last), 0)),
        mod_spec(tile_a), _resident((1, D)), _resident(pool_w.shape), _resident((1, D)),
        mod_spec(tile_b), _resident((1, D)), _resident(w1.shape), _resident(w2.shape),
    ]
    if final_g is not None:
        args.append(final_g.reshape(1, D))
        specs.append(_resident((1, D)))
    c_args, c_in, c_shapes, c_out, c_orig = _side_casts(casts, n_tiles, lambda s: jnp.minimum(s, n_tiles - 1))
    outs = pl.pallas_call(
        functools.partial(_pool_mlp_kernel, has_final=final_g is not None, seq_len=L, tiles_per_seq=tps,
                          n_tiles=n_tiles, cast_heads=tuple(c[2] for c in casts)),
        grid=(n_tiles + 1,),
        in_specs=specs + c_in,
        out_specs=[pl.BlockSpec((1, tm, D), lambda s: (*tile_b(s), 0))] + c_out,
        out_shape=[jax.ShapeDtypeStruct(h.shape, F32)] + c_shapes,
        scratch_shapes=[pltpu.VMEM((2, tm, D), F32)],
        compiler_params=_params("arbitrary"),
        name="pool_mlp",
    )(*args, *c_args)
    return outs[0], [w.reshape(s) for w, s in zip(outs[1:], c_orig)]


def _permute_head_lanes(t):
    q = HEAD_DIM // 4
    lane = jax.lax.broadcasted_iota(jnp.int32, t.shape, 1)
    from_right = pltpu.roll(t, HEAD_DIM - q, axis=1)
    from_left = pltpu.roll(t, q, axis=1)
    return jnp.where(lane < q, t, jnp.where(lane < 2 * q, from_right, jnp.where(lane < 3 * q, from_left, t)))


def _rope_tables(L):
    rows_n = L // GRID_W
    q = HEAD_DIM // 4
    half = HEAD_DIM // 2
    inv = (np.float32(ROPE_BASE) ** (-np.arange(0, half, 2, dtype=np.float32) / np.float32(half))).astype(np.float32)
    ang_r = (np.arange(rows_n, dtype=np.float32)[:, None] * inv[None, :]).astype(np.float64)
    ang_c = (np.arange(GRID_W, dtype=np.float32)[:, None] * inv[None, :]).astype(np.float64)

    def table(fn, signs):
        by_row = np.zeros((rows_n, HEAD_DIM), np.float32)
        by_col = np.zeros((GRID_W, HEAD_DIM), np.float32)
        for quarter, sign in enumerate(signs):
            tab, ang = (by_row, ang_r) if quarter % 2 == 0 else (by_col, ang_c)
            tab[:, quarter * q:(quarter + 1) * q] = sign * fn(ang)
        return (jnp.broadcast_to(jnp.asarray(by_row)[:, None, :], (rows_n, GRID_W, HEAD_DIM))
                + jnp.asarray(by_col)[None, :, :]).reshape(L, HEAD_DIM)

    return table(np.cos, (1.0, 1.0, 1.0, 1.0)), table(np.sin, (-1.0, -1.0, 1.0, 1.0))


def _qkv_kernel(*refs, rope, want_q):
    refs = list(refs)
    h_ref, mod_ref, g_ref, w_ref, qg_ref, kg_ref = refs[:6]
    refs = refs[6:]
    if rope:
        cos_ref, sin_ref = refs[:2]
        refs = refs[2:]
    if want_q:
        q_ref = refs[0]
        refs = refs[1:]
    k_ref, vt_ref = refs

    tm = h_ref.shape[1]
    sub = min(tm, QKV_SUB_ROWS)
    chains = []
    for r0 in range(0, tm, sub):
        rows = slice(r0, r0 + sub)
        a = _norm_mod(h_ref[0, rows, :], g_ref[...], mod_ref[0, 0:1, :], mod_ref[0, 1:2, :]).astype(BF16)
        yq = _dot(a, w_ref[:, :Q_WIDTH]) if want_q else None
        yk = _dot(a, w_ref[:, Q_WIDTH:Q_WIDTH + KV_WIDTH])
        yv = _dot(a, w_ref[:, Q_WIDTH + KV_WIDTH:])
        chains.append((rows, yq, yk, yv))

    for rows, yq, yk, yv in chains:
        def heads(y, gain, out_ref):
            for hd in range(y.shape[1] // HEAD_DIM):
                t = _rms(y[:, hd * HEAD_DIM:(hd + 1) * HEAD_DIM], gain)
                if rope:
                    t = t * cos_ref[rows, :] + pltpu.roll(t, HEAD_DIM // 2, axis=1) * sin_ref[rows, :]
                out_ref[0, rows, hd * HEAD_DIM:(hd + 1) * HEAD_DIM] = t.astype(out_ref.dtype)

        if want_q:
            heads(yq, qg_ref[...], q_ref)
        heads(yk, kg_ref[...], k_ref)
        for hd in range(N_KV_HEADS):
            vt_ref[0, hd, 0:HEAD_DIM, rows] = yv[:, hd * HEAD_DIM:(hd + 1) * HEAD_DIM].T.astype(vt_ref.dtype)
            vt_ref[0, hd, HEAD_DIM:, rows] = jnp.ones((VT_ROWS - HEAD_DIM, sub), vt_ref.dtype)


def _qkv_call(h, mod, g, w, q_g, k_g, rope_tabs, want_q, tm=QKV_TM):
    B, L, D = h.shape
    tm = min(tm, L)
    tok = lambda width: pl.BlockSpec((1, tm, width), lambda b, i: (b, i, 0))
    args = [h, mod, g.reshape(1, D), w, q_g.reshape(1, HEAD_DIM), k_g.reshape(1, HEAD_DIM)]
    specs = [tok(D), _mod_spec(mod), _resident((1, D)), _resident(w.shape),
             _resident((1, HEAD_DIM)), _resident((1, HEAD_DIM))]
    if rope_tabs is not None:
        args += list(rope_tabs)
        specs += [pl.BlockSpec((tm, HEAD_DIM), lambda b, i: (i, 0))] * 2
    out_shape, out_specs = [], []
    if want_q:
        out_shape.append(jax.ShapeDtypeStruct((B, L, Q_WIDTH), BF16))
        out_specs.append(tok(Q_WIDTH))
    out_shape += [jax.ShapeDtypeStruct((B, L, KV_WIDTH), BF16),
                  jax.ShapeDtypeStruct((B, N_KV_HEADS, VT_ROWS, L), BF16)]
    out_specs += [tok(KV_WIDTH), pl.BlockSpec((1, N_KV_HEADS, VT_ROWS, tm), lambda b, i: (b, 0, 0, i))]
    return pl.pallas_call(
        functools.partial(_qkv_kernel, rope=rope_tabs is not None, want_q=want_q),
        grid=(B, L // tm),
        in_specs=specs,
        out_specs=out_specs,
        out_shape=out_shape,
        compiler_params=_params("parallel", "parallel"),
        name="qkv_proj",
    )(*args)


def _flash_kernel(*refs, tq, tk):
    q_ref, o_ref = refs[0], refs[-1]
    sources = [(refs[1 + 2 * s], refs[2 + 2 * s]) for s in range((len(refs) - 2) // 2)]
    q = jnp.concatenate([q_ref[0, :, hd * HEAD_DIM:(hd + 1) * HEAD_DIM] for hd in range(Q_PER_KV)], axis=0)
    c = (HEAD_DIM ** -0.5) * LOG2E
    nt = (((1,), (1,)), ((), ()))

    def block_size(k_ref):
        return min(tk, k_ref.shape[1])

    def finish(acc):
        o = (acc[:HEAD_DIM] / acc[HEAD_DIM:HEAD_DIM + 1]).T
        for hd in range(Q_PER_KV):
            o_ref[0, :, hd * HEAD_DIM:(hd + 1) * HEAD_DIM] = o[hd * tq:(hd + 1) * tq].astype(o_ref.dtype)

    blocks = [(k_ref, vt_ref, start, block_size(k_ref)) for k_ref, vt_ref in sources
              for start in range(0, k_ref.shape[1], block_size(k_ref))]
    acc = jnp.zeros((VT_ROWS, q.shape[0]), F32)
    excess = jnp.zeros((1, q.shape[0]), F32)
    for j, (k_ref, vt_ref, start, size) in enumerate(blocks):
        st = jax.lax.dot_general(k_ref[0, start:start + size, :], q, nt, preferred_element_type=F32)
        block_max = jnp.max(st, axis=0, keepdims=True)
        if j == 0:
            shift = block_max
        else:
            excess = jnp.maximum(excess, (block_max - shift) * c)
        p = jnp.exp2((st - shift) * c).astype(BF16)
        acc = acc + _dot(vt_ref[0, 0, :, start:start + size], p)
        if 0 < j < len(blocks) - 1:
            new_shift = jnp.maximum(shift, block_max)
            acc = acc * jnp.exp2((shift - new_shift) * c)
            shift = new_shift
    finish(acc)

    @pl.when(jnp.max(excess) > FLASH_MAX_EXCESS)
    def _():
        carry = (jnp.full((1, q.shape[0]), -jnp.inf, F32), jnp.zeros((VT_ROWS, q.shape[0]), F32))
        for k_ref, vt_ref in sources:
            size = block_size(k_ref)

            def body(j, carry, k_ref=k_ref, vt_ref=vt_ref, size=size):
                m, acc = carry
                start = pl.multiple_of(j * size, size)
                st = jax.lax.dot_general(k_ref[0, pl.ds(start, size), :], q, nt, preferred_element_type=F32)
                m_next = jnp.maximum(m, jnp.max(st, axis=0, keepdims=True))
                p = jnp.exp2((st - m_next) * c).astype(BF16)
                acc = acc * jnp.exp2((m - m_next) * c) + _dot(vt_ref[0, 0, :, pl.ds(start, size)], p)
                return m_next, acc

            carry = jax.lax.fori_loop(0, k_ref.shape[1] // size, body, carry)
        finish(carry[1])


def _flash_call(q, kv_sources, tq=FLASH_TQ, tk=FLASH_TK):
    B, L, _ = q.shape
    args, specs = [q], [pl.BlockSpec((1, tq, Q_PER_KV * HEAD_DIM), lambda b, kh, i: (b, i, kh))]
    for k, vt in kv_sources:
        Lk = k.shape[1]
        assert Lk % min(tk, Lk) == 0
        args += [k, vt]
        specs += [pl.BlockSpec((1, Lk, HEAD_DIM), lambda b, kh, i: (b, 0, kh)),
                  pl.BlockSpec((1, 1, VT_ROWS, Lk), lambda b, kh, i: (b, kh, 0, 0))]
    return pl.pallas_call(
        functools.partial(_flash_kernel, tq=tq, tk=tk),
        grid=(B, N_KV_HEADS, L // tq),
        in_specs=specs,
        out_specs=pl.BlockSpec((1, tq, Q_PER_KV * HEAD_DIM), lambda b, kh, i: (b, i, kh)),
        out_shape=jax.ShapeDtypeStruct(q.shape, BF16),
        compiler_params=_params("parallel", "parallel", "parallel"),
        name="flash_gqa",
    )(*args)


def _gmlp_kernel(h_ref, mod_ref, g_ref, win_ref, lng_ref, lnb_ref, ws_ref, bst_ref, o_ref, *, tm):
    sub = min(tm, GMLP_SUB_ROWS)
    starts = range(0, tm, sub)
    a = [_norm_mod(h_ref[0, r0:r0 + sub, :], g_ref[...], mod_ref[0, 0:1, :], mod_ref[0, 1:2, :]).astype(BF16)
         for r0 in starts]
    zv = [_dot(a_c, win_ref[:, GMLP_HALF:]) for a_c in a]
    zu = [_dot(a_c, win_ref[:, :GMLP_HALF]) for a_c in a]
    for c, r0 in enumerate(starts):
        v = jax.nn.gelu(zv[c], approximate=True)
        mu = jnp.mean(v, axis=-1, keepdims=True)
        vc = v - mu
        v = vc * jax.lax.rsqrt(jnp.mean(vc * vc, axis=-1, keepdims=True) + EPS) * lng_ref[...] + lnb_ref[...]
        v = v.astype(BF16)
        u = jax.nn.gelu(zu[c], approximate=True)
        for ch in range(sub // CHUNK):
            rows = slice(ch * CHUNK, (ch + 1) * CHUNK)
            out_rows = slice(r0 + ch * CHUNK, r0 + (ch + 1) * CHUNK)
            for grp in range(GMLP_GROUPS):
                cols = slice(grp * GMLP_GW, (grp + 1) * GMLP_GW)
                sv = _dot(ws_ref[grp], v[rows, cols]) + bst_ref[:, grp:grp + 1]
                o_ref[0, out_rows, cols] = (u[rows, cols] * sv).astype(o_ref.dtype)


def _gmlp_call(h, mod, g, w_in, ln_g, ln_b, ws, bs, tm=GMLP_TM):
    B, L, D = h.shape
    return pl.pallas_call(
        functools.partial(_gmlp_kernel, tm=tm),
        grid=(B, L // tm),
        in_specs=[
            pl.BlockSpec((1, tm, D), lambda b, i: (b, i, 0)),
            _mod_spec(mod),
            _resident((1, D)),
            _resident(w_in.shape),
            _resident((1, GMLP_HALF)),
            _resident((1, GMLP_HALF)),
            _resident(ws.shape),
            _resident((CHUNK, GMLP_GROUPS)),
        ],
        out_specs=pl.BlockSpec((1, tm, GMLP_HALF), lambda b, i: (b, i, 0)),
        out_shape=jax.ShapeDtypeStruct((B, L, GMLP_HALF), BF16),
        compiler_params=_params("parallel", "parallel"),
        name="gmlp_gate",
    )(h, mod, g.reshape(1, D), w_in, ln_g.reshape(1, GMLP_HALF), ln_b.reshape(1, GMLP_HALF), ws, bs.T)


def kernel(x, c, ctx, c_ctx, ada_w, ada_b, norm_g, mlp_w1, mlp_w2, pool_w, pool_scale, attn_w_qkv, attn_w_o,
           attn_q_g, attn_k_g, gm_w_in, gm_ln_g, gm_ln_b, gm_ws, gm_bs, gm_w_out, final_g):
    B, S, D = x.shape
    last_ctx_read = max([i for i in range(DEPTH) if i % N_MIXERS == 1], default=-1)

    pad = (-(B + 1)) % 8
    s_in = jnp.concatenate([c, c_ctx[None, :], jnp.zeros((pad, D), F32)], axis=0)
    mods = _ada_call(s_in, ada_w, ada_b)
    mod_lat = mods[:, :B].reshape(DEPTH, B, 6, D)
    mod_ctx = mods[:, B:B + 1].reshape(DEPTH, 1, 6, D)

    def layer_weight_stacks(i):
        kind, j = i % N_MIXERS, i // N_MIXERS
        mixer = {0: [(pool_w, j, 0)], 1: [(attn_w_qkv, j, N_HEADS + N_KV_HEADS), (attn_w_o, j, 0)],
                 2: [(gm_w_in, j, 0), (gm_ws, j, 0), (gm_w_out, j, 0)]}[kind]
        return mixer + [(mlp_w1, i, 0), (mlp_w2, i, 0)]

    h_lat, h_ctx = x, ctx
    weights = [_cast_call(*entry) for entry in layer_weight_stacks(0)]
    for i in range(DEPTH):
        kind, j = i % N_MIXERS, i // N_MIXERS
        ctx_out = i < last_ctx_read
        *mixer_w, w1, w2 = weights
        nxt = layer_weight_stacks(i + 1) if i + 1 < DEPTH else ()
        fin = final_g if i == DEPTH - 1 else None
        if kind == 0:
            pw, = mixer_w
            if ctx_out:
                h_ctx, _ = _pool_mlp_call(h_ctx, mod_ctx[i], norm_g[i, 0], pw, pool_scale[j], norm_g[i, 1], w1, w2)
            h_lat, weights = _pool_mlp_call(h_lat, mod_lat[i], norm_g[i, 0], pw, pool_scale[j], norm_g[i, 1], w1, w2,
                                            final_g=fin, casts=nxt)
        elif kind == 1:
            wqkv, wo = mixer_w
            q_g, k_g = (g_[j].reshape(2, 2, HEAD_DIM // 4).transpose(1, 0, 2).reshape(HEAD_DIM)
                        for g_ in (attn_q_g, attn_k_g))
            q, k_l, vt_l = _qkv_call(h_lat, mod_lat[i], norm_g[i, 0], wqkv, q_g, k_g, _rope_tables(S), True)
            k_c, vt_c = _qkv_call(h_ctx, mod_ctx[i], norm_g[i, 0], wqkv, q_g, k_g, None, False)
            o = _flash_call(q, [(k_c, vt_c), (k_l, vt_l)])
            if ctx_out:
                raise NotImplementedError("context stream output of an attention layer")
            h_lat, weights = _mlp_call(h_lat, mod_lat[i], norm_g[i, 1], w1, w2, proj=(o, wo), final_g=fin, casts=nxt)
        else:
            w_in, ws, w_out = mixer_w
            t = _gmlp_call(h_lat, mod_lat[i], norm_g[i, 0], w_in, gm_ln_g[j], gm_ln_b[j], ws, gm_bs[j])
            if ctx_out:
                raise NotImplementedError("context stream output of a gMLP layer")
            h_lat, weights = _mlp_call(h_lat, mod_lat[i], norm_g[i, 1], w1, w2, proj=(t, w_out), final_g=fin,
                                       casts=nxt)
    return h_lat
```

```python
import functools

import jax
import jax.numpy as jnp
import numpy as np
from jax.experimental import pallas as pl
from jax.experimental.pallas import tpu as pltpu

D_MODEL = 1024
DEPTH = 4
N_MIXERS = 3
GRID_W = 64
EPS = 1e-6
POOL_WINDOWS = (2, 4, 8, 16)
POOL_GW = D_MODEL // len(POOL_WINDOWS)
POOL_HALO = 16
HEAD_DIM = 128
N_HEADS = D_MODEL // HEAD_DIM
N_KV_HEADS = N_HEADS // 2
Q_PER_KV = N_HEADS // N_KV_HEADS
Q_WIDTH = N_HEADS * HEAD_DIM
KV_WIDTH = N_KV_HEADS * HEAD_DIM
VT_ROWS = HEAD_DIM + 16
ROPE_BASE = 10000.0
CHUNK = 128
GMLP_HALF = 2 * D_MODEL
GMLP_GROUPS = 8
GMLP_GW = GMLP_HALF // GMLP_GROUPS
D_FF = 4 * D_MODEL
LOG2E = 1.4426950408889634
FLASH_MAX_EXCESS = 64.0

MLP_TM = 512
QKV_TM = 1024
QKV_SUB_ROWS = 256
GMLP_TM = 1024
GMLP_SUB_ROWS = 256
FLASH_TQ = 512
FLASH_TK = 1024
ADA_TN = 3072
CAST_BLOCK_ELEMS = 1024 * 1024

VMEM_LIMIT_BYTES = 56 * 1024 * 1024
BF16 = jnp.bfloat16
F32 = jnp.float32


def _params(*semantics):
    return pltpu.CompilerParams(dimension_semantics=semantics, vmem_limit_bytes=VMEM_LIMIT_BYTES)


def _resident(shape):
    zeros = (0,) * len(shape)
    return pl.BlockSpec(shape, lambda *_: zeros, pipeline_mode=pl.Buffered(1))


def _mod_spec(mod):
    if mod.shape[0] == 1:
        return pl.BlockSpec((1, 6, D_MODEL), lambda b, i: (0, 0, 0))
    return pl.BlockSpec((1, 6, D_MODEL), lambda b, i: (b, 0, 0))


def _rms(x, g):
    return x * jax.lax.rsqrt(jnp.mean(x * x, axis=-1, keepdims=True) + EPS) * g


def _norm_mod(x, g, shift, scale):
    return _rms(x, g) * (1.0 + scale) + shift


def _dot(a, b):
    return jnp.dot(a, b, preferred_element_type=F32)


def _cast_block(w, permute_heads):
    if permute_heads:
        heads = [_permute_head_lanes(w[:, hd * HEAD_DIM:(hd + 1) * HEAD_DIM]) for hd in range(permute_heads)]
        w = jnp.concatenate(heads + [w[:, permute_heads * HEAD_DIM:]], axis=-1)
    return w.astype(BF16)


def _cast_kernel(w_ref, o_ref, *, permute_heads):
    o_ref[...] = _cast_block(w_ref[0], permute_heads)


def _cast_call(w_stack, layer, permute_heads=0):
    shape = w_stack.shape[1:]
    C = shape[-1]
    R = int(np.prod(shape[:-1]))
    br = min(R, CAST_BLOCK_ELEMS // C)
    out = pl.pallas_call(
        functools.partial(_cast_kernel, permute_heads=permute_heads),
        grid=(R // br,),
        in_specs=[pl.BlockSpec((1, br, C), lambda r: (layer, r, 0))],
        out_specs=pl.BlockSpec((br, C), lambda r: (r, 0)),
        out_shape=jax.ShapeDtypeStruct((R, C), BF16),
        compiler_params=_params("parallel"),
        name="cast_bf16",
    )(w_stack.reshape(w_stack.shape[0], R, C))
    return out.reshape(shape)


def _side_casts(casts, n_steps, step_of):
    args, in_specs, out_shapes, out_specs, shapes = [], [], [], [], []
    for w_stack, layer, _ in casts:
        shape = w_stack.shape[1:]
        C = shape[-1]
        R = int(np.prod(shape[:-1]))
        assert R % (n_steps * 16) == 0
        br = R // n_steps
        args.append(w_stack.reshape(w_stack.shape[0], R, C))
        in_specs.append(pl.BlockSpec((1, br, C), lambda *idx, layer=layer: (layer, step_of(*idx), 0)))
        out_shapes.append(jax.ShapeDtypeStruct((R, C), BF16))
        out_specs.append(pl.BlockSpec((br, C), lambda *idx: (step_of(*idx), 0)))
        shapes.append(shape)
    return args, in_specs, out_shapes, out_specs, shapes


def _run_side_casts(in_refs, out_refs, permute_heads):
    for w_ref, o_ref, heads in zip(in_refs, out_refs, permute_heads):
        o_ref[...] = _cast_block(w_ref[0], heads)


def _ada_kernel(s_ref, w_ref, b_ref, o_ref):
    s = s_ref[...]
    s = s * jax.nn.sigmoid(s)
    o_ref[0] = _dot(s.astype(BF16), w_ref[0].astype(BF16)) + b_ref[0]


def _ada_call(s_in, ada_w, ada_b):
    rows = s_in.shape[0]
    tn = ADA_TN
    return pl.pallas_call(
        _ada_kernel,
        grid=(DEPTH, 6 * D_MODEL // tn),
        in_specs=[
            pl.BlockSpec((rows, D_MODEL), lambda l, j: (0, 0)),
            pl.BlockSpec((1, D_MODEL, tn), lambda l, j: (l, 0, j)),
            pl.BlockSpec((1, 1, tn), lambda l, j: (l, 0, j)),
        ],
        out_specs=pl.BlockSpec((1, rows, tn), lambda l, j: (l, 0, j)),
        out_shape=jax.ShapeDtypeStruct((DEPTH, rows, 6 * D_MODEL), F32),
        compiler_params=_params("arbitrary", "arbitrary"),
        name="ada_mod",
    )(s_in, ada_w, ada_b.reshape(DEPTH, 1, 6 * D_MODEL))


def _pool_window(a_ext, a_main, pos, seq_len, grp):
    n_ext, n_main = a_ext.shape[0], a_main.shape[0]
    win = POOL_WINDOWS[grp]
    cols = slice(grp * POOL_GW, (grp + 1) * POOL_GW)
    lo_off, hi_off = -(win // 2), win - win // 2
    s = a_ext[:, cols]
    if hi_off > 1:
        s = pltpu.roll(s, n_ext - (hi_off - 1), axis=0)
    k = 1
    while k < win:
        s = s + pltpu.roll(s, k, axis=0)
        k *= 2
    s = s[POOL_HALO:POOL_HALO + n_main]
    cnt = (jnp.minimum(pos + hi_off, seq_len) - jnp.maximum(pos + lo_off, 0)).astype(F32)
    return (s / cnt - a_main[:, cols]).astype(BF16)


def _mlp_rows(x, mod_ref, g_ref, w1_ref, w2_ref, fg_ref):
    m = _norm_mod(x, g_ref[...], mod_ref[0, 3:4, :], mod_ref[0, 4:5, :]).astype(BF16)
    a = jnp.maximum(_dot(m, w1_ref[...]), 0.0)
    a = (a * a).astype(BF16)
    y = x + mod_ref[0, 5:6, :] * _dot(a, w2_ref[...])
    return y if fg_ref is None else _rms(y, fg_ref[...])


def _mlp_kernel(*refs, has_proj, has_final, cast_heads):
    refs = list(refs)
    h_ref, mod_ref, g_ref = refs[:3]
    refs = refs[3:]
    if has_proj:
        t_ref, wp_ref = refs[:2]
        refs = refs[2:]
    w1_ref, w2_ref = refs[:2]
    refs = refs[2:]
    fg_ref = None
    if has_final:
        fg_ref = refs[0]
        refs = refs[1:]
    n_cast = len(cast_heads)
    cast_in, o_ref, cast_out = refs[:n_cast], refs[n_cast], refs[n_cast + 1:]

    x = h_ref[0]
    if has_proj:
        x = x + mod_ref[0, 2:3, :] * _dot(t_ref[0], wp_ref[...])
    o_ref[0] = _mlp_rows(x, mod_ref, g_ref, w1_ref, w2_ref, fg_ref)
    _run_side_casts(cast_in, cast_out, cast_heads)


def _mlp_call(h, mod, g, w1, w2, proj=None, final_g=None, casts=(), tm=MLP_TM):
    B, L, D = h.shape
    tm = min(tm, L)
    n = L // tm
    tok = lambda width: pl.BlockSpec((1, tm, width), lambda b, i: (b, i, 0))
    args = [h, mod, g.reshape(1, D)]
    specs = [tok(D), _mod_spec(mod), _resident((1, D))]
    if proj is not None:
        t, wp = proj
        args += [t, wp]
        specs += [tok(t.shape[-1]), _resident(wp.shape)]
    args += [w1, w2]
    specs += [_resident(w1.shape), _resident(w2.shape)]
    if final_g is not None:
        args.append(final_g.reshape(1, D))
        specs.append(_resident((1, D)))
    c_args, c_in, c_shapes, c_out, c_orig = _side_casts(casts, B * n, lambda b, i: b * n + i)
    outs = pl.pallas_call(
        functools.partial(_mlp_kernel, has_proj=proj is not None, has_final=final_g is not None,
                          cast_heads=tuple(c[2] for c in casts)),
        grid=(B, n),
        in_specs=specs + c_in,
        out_specs=[tok(D)] + c_out,
        out_shape=[jax.ShapeDtypeStruct(h.shape, F32)] + c_shapes,
        compiler_params=_params("parallel", "parallel"),
        name="mlp",
    )(*args, *c_args)
    return outs[0], [w.reshape(s) for w, s in zip(outs[1:], c_orig)]


def _pool_mlp_kernel(*refs, has_final, seq_len, tiles_per_seq, n_tiles, cast_heads):
    refs = list(refs)
    h_ref, hp_ref, hn_ref, moda_ref, g0_ref, pw_ref, psc_ref, modb_ref, g_ref, w1_ref, w2_ref = refs[:11]
    refs = refs[11:]
    fg_ref = None
    if has_final:
        fg_ref = refs[0]
        refs = refs[1:]
    n_cast = len(cast_heads)
    cast_in, o_ref, cast_out, x1_ref = refs[:n_cast], refs[n_cast], refs[n_cast + 1:-1], refs[-1]
    s = pl.program_id(0)
    tm = h_ref.shape[1]

    i = jnp.minimum(s, n_tiles - 1) % tiles_per_seq

    def stages(write_slot, read_slot, with_mlp=True):
        if with_mlp:
            xb = x1_ref[read_slot]
            m = _norm_mod(xb, g_ref[...], modb_ref[0, 3:4, :], modb_ref[0, 4:5, :]).astype(BF16)
        norm = functools.partial(_norm_mod, g=g0_ref[...], shift=moda_ref[0, 0:1, :], scale=moda_ref[0, 1:2, :])
        x = h_ref[0]
        a_main = norm(x)
        top = jnp.where(i > 0, norm(hp_ref[0]), 0.0)
        bottom = jnp.where(i < tiles_per_seq - 1, norm(hn_ref[0]), 0.0)
        pos = i * tm + jax.lax.broadcasted_iota(jnp.int32, (tm, 1), 0)
        a_ext = jnp.concatenate([top, a_main, bottom], axis=0)

        n_grp = len(POOL_WINDOWS)
        fc = D_FF // n_grp
        acc, ys = None, []
        for grp in range(n_grp):
            p = _pool_window(a_ext, a_main, pos, seq_len, grp)
            if with_mlp:
                a = jnp.maximum(_dot(m, w1_ref[:, grp * fc:(grp + 1) * fc]), 0.0)
                d = _dot((a * a).astype(BF16), w2_ref[grp * fc:(grp + 1) * fc, :])
                acc = d if acc is None else acc + d
            ys.append(_dot(p, pw_ref[grp]))
        if with_mlp:
            y = xb + modb_ref[0, 5:6, :] * acc
            o_ref[0] = y if fg_ref is None else _rms(y, fg_ref[...])
        x1_ref[write_slot] = x + moda_ref[0, 2:3, :] * (jnp.concatenate(ys, axis=-1) * psc_ref[...])
        _run_side_casts(cast_in, cast_out, cast_heads)

    pl.when(s == 0)(functools.partial(stages, 0, 1, with_mlp=False))
    pl.when((s > 0) & (s % 2 == 0))(functools.partial(stages, 0, 1))
    pl.when(s % 2 == 1)(functools.partial(stages, 1, 0))


def _pool_mlp_call(h, mod, g0, pool_w, pool_scale, g, w1, w2, final_g=None, casts=(), tm=MLP_TM):
    B, L, D = h.shape
    tm = min(tm, L)
    tps = L // tm
    n_tiles = B * tps
    r = tm // POOL_HALO
    last = L // POOL_HALO - 1
    shared_mod = mod.shape[0] == 1

    def tile_a(s):
        t = jnp.minimum(s, n_tiles - 1)
        return t // tps, t % tps

    def tile_b(s):
        t = jnp.maximum(s - 1, 0)
        return t // tps, t % tps

    def mod_spec(tile):
        return pl.BlockSpec((1, 6, D), lambda s: (0 if shared_mod else tile(s)[0], 0, 0))

    args = [h, h, h, mod, g0.reshape(1, D), pool_w, pool_scale.reshape(1, D), mod, g.reshape(1, D), w1, w2]
    specs = [
        pl.BlockSpec((1, tm, D), lambda s: (*tile_a(s), 0)),
        pl.BlockSpec((1, POOL_HALO, D), lambda s: (tile_a(s)[0], jnp.maximum(tile_a(s)[1] * r - 1, 0), 0)),
        pl.BlockSpec((1, POOL_HALO, D), lambda s: (tile_a(s)[0], jnp.minimum((tile_a(s)[1] + 1) * r, last), 0)),
        mod_spec(tile_a), _resident((1, D)), _resident(pool_w.shape), _resident((1, D)),
        mod_spec(tile_b), _resident((1, D)), _resident(w1.shape), _resident(w2.shape),
    ]
    if final_g is not None:
        args.append(final_g.reshape(1, D))
        specs.append(_resident((1, D)))
    c_args, c_in, c_shapes, c_out, c_orig = _side_casts(casts, n_tiles, lambda s: jnp.minimum(s, n_tiles - 1))
    outs = pl.pallas_call(
        functools.partial(_pool_mlp_kernel, has_final=final_g is not None, seq_len=L, tiles_per_seq=tps,
                          n_tiles=n_tiles, cast_heads=tuple(c[2] for c in casts)),
        grid=(n_tiles + 1,),
        in_specs=specs + c_in,
        out_specs=[pl.BlockSpec((1, tm, D), lambda s: (*tile_b(s), 0))] + c_out,
        out_shape=[jax.ShapeDtypeStruct(h.shape, F32)] + c_shapes,
        scratch_shapes=[pltpu.VMEM((2, tm, D), F32)],
        compiler_params=_params("arbitrary"),
        name="pool_mlp",
    )(*args, *c_args)
    return outs[0], [w.reshape(s) for w, s in zip(outs[1:], c_orig)]


def _permute_head_lanes(t):
    q = HEAD_DIM // 4
    lane = jax.lax.broadcasted_iota(jnp.int32, t.shape, 1)
    from_right = pltpu.roll(t, HEAD_DIM - q, axis=1)
    from_left = pltpu.roll(t, q, axis=1)
    return jnp.where(lane < q, t, jnp.where(lane < 2 * q, from_right, jnp.where(lane < 3 * q, from_left, t)))


def _rope_tables(L):
    rows_n = L // GRID_W
    q = HEAD_DIM // 4
    half = HEAD_DIM // 2
    inv = (np.float32(ROPE_BASE) ** (-np.arange(0, half, 2, dtype=np.float32) / np.float32(half))).astype(np.float32)
    ang_r = (np.arange(rows_n, dtype=np.float32)[:, None] * inv[None, :]).astype(np.float64)
    ang_c = (np.arange(GRID_W, dtype=np.float32)[:, None] * inv[None, :]).astype(np.float64)

    def table(fn, signs):
        by_row = np.zeros((rows_n, HEAD_DIM), np.float32)
        by_col = np.zeros((GRID_W, HEAD_DIM), np.float32)
        for quarter, sign in enumerate(signs):
            tab, ang = (by_row, ang_r) if quarter % 2 == 0 else (by_col, ang_c)
            tab[:, quarter * q:(quarter + 1) * q] = sign * fn(ang)
        return (jnp.broadcast_to(jnp.asarray(by_row)[:, None, :], (rows_n, GRID_W, HEAD_DIM))
                + jnp.asarray(by_col)[None, :, :]).reshape(L, HEAD_DIM)

    return table(np.cos, (1.0, 1.0, 1.0, 1.0)), table(np.sin, (-1.0, -1.0, 1.0, 1.0))


def _qkv_kernel(*refs, rope, want_q):
    refs = list(refs)
    h_ref, mod_ref, g_ref, w_ref, qg_ref, kg_ref = refs[:6]
    refs = refs[6:]
    if rope:
        cos_ref, sin_ref = refs[:2]
        refs = refs[2:]
    if want_q:
        q_ref = refs[0]
        refs = refs[1:]
    k_ref, vt_ref = refs

    tm = h_ref.shape[1]
    sub = min(tm, QKV_SUB_ROWS)
    chains = []
    for r0 in range(0, tm, sub):
        rows = slice(r0, r0 + sub)
        a = _norm_mod(h_ref[0, rows, :], g_ref[...], mod_ref[0, 0:1, :], mod_ref[0, 1:2, :]).astype(BF16)
        yq = _dot(a, w_ref[:, :Q_WIDTH]) if want_q else None
        yk = _dot(a, w_ref[:, Q_WIDTH:Q_WIDTH + KV_WIDTH])
        yv = _dot(a, w_ref[:, Q_WIDTH + KV_WIDTH:])
        chains.append((rows, yq, yk, yv))

    for rows, yq, yk, yv in chains:
        def heads(y, gain, out_ref):
            for hd in range(y.shape[1] // HEAD_DIM):
                t = _rms(y[:, hd * HEAD_DIM:(hd + 1) * HEAD_DIM], gain)
                if rope:
                    t = t * cos_ref[rows, :] + pltpu.roll(t, HEAD_DIM // 2, axis=1) * sin_ref[rows, :]
                out_ref[0, rows, hd * HEAD_DIM:(hd + 1) * HEAD_DIM] = t.astype(out_ref.dtype)

        if want_q:
            heads(yq, qg_ref[...], q_ref)
        heads(yk, kg_ref[...], k_ref)
        for hd in range(N_KV_HEADS):
            vt_ref[0, hd, 0:HEAD_DIM, rows] = yv[:, hd * HEAD_DIM:(hd + 1) * HEAD_DIM].T.astype(vt_ref.dtype)
            vt_ref[0, hd, HEAD_DIM:, rows] = jnp.ones((VT_ROWS - HEAD_DIM, sub), vt_ref.dtype)


def _qkv_call(h, mod, g, w, q_g, k_g, rope_tabs, want_q, tm=QKV_TM):
    B, L, D = h.shape
    tm = min(tm, L)
    tok = lambda width: pl.BlockSpec((1, tm, width), lambda b, i: (b, i, 0))
    args = [h, mod, g.reshape(1, D), w, q_g.reshape(1, HEAD_DIM), k_g.reshape(1, HEAD_DIM)]
    specs = [tok(D), _mod_spec(mod), _resident((1, D)), _resident(w.shape),
             _resident((1, HEAD_DIM)), _resident((1, HEAD_DIM))]
    if rope_tabs is not None:
        args += list(rope_tabs)
        specs += [pl.BlockSpec((tm, HEAD_DIM), lambda b, i: (i, 0))] * 2
    out_shape, out_specs = [], []
    if want_q:
        out_shape.append(jax.ShapeDtypeStruct((B, L, Q_WIDTH), BF16))
        out_specs.append(tok(Q_WIDTH))
    out_shape += [jax.ShapeDtypeStruct((B, L, KV_WIDTH), BF16),
                  jax.ShapeDtypeStruct((B, N_KV_HEADS, VT_ROWS, L), BF16)]
    out_specs += [tok(KV_WIDTH), pl.BlockSpec((1, N_KV_HEADS, VT_ROWS, tm), lambda b, i: (b, 0, 0, i))]
    return pl.pallas_call(
        functools.partial(_qkv_kernel, rope=rope_tabs is not None, want_q=want_q),
        grid=(B, L // tm),
        in_specs=specs,
        out_specs=out_specs,
        out_shape=out_shape,
        compiler_params=_params("parallel", "parallel"),
        name="qkv_proj",
    )(*args)


def _flash_kernel(*refs, tq, tk):
    q_ref, o_ref = refs[0], refs[-1]
    sources = [(refs[1 + 2 * s], refs[2 + 2 * s]) for s in range((len(refs) - 2) // 2)]
    q = jnp.concatenate([q_ref[0, :, hd * HEAD_DIM:(hd + 1) * HEAD_DIM] for hd in range(Q_PER_KV)], axis=0)
    c = (HEAD_DIM ** -0.5) * LOG2E
    nt = (((1,), (1,)), ((), ()))

    def block_size(k_ref):
        return min(tk, k_ref.shape[1])

    def finish(acc):
        o = (acc[:HEAD_DIM] / acc[HEAD_DIM:HEAD_DIM + 1]).T
        for hd in range(Q_PER_KV):
            o_ref[0, :, hd * HEAD_DIM:(hd + 1) * HEAD_DIM] = o[hd * tq:(hd + 1) * tq].astype(o_ref.dtype)

    blocks = [(k_ref, vt_ref, start, block_size(k_ref)) for k_ref, vt_ref in sources
              for start in range(0, k_ref.shape[1], block_size(k_ref))]
    acc = jnp.zeros((VT_ROWS, q.shape[0]), F32)
    excess = jnp.zeros((1, q.shape[0]), F32)
    for j, (k_ref, vt_ref, start, size) in enumerate(blocks):
        st = jax.lax.dot_general(k_ref[0, start:start + size, :], q, nt, preferred_element_type=F32)
        block_max = jnp.max(st, axis=0, keepdims=True)
        if j == 0:
            shift = block_max
        else:
            excess = jnp.maximum(excess, (block_max - shift) * c)
        p = jnp.exp2((st - shift) * c).astype(BF16)
        acc = acc + _dot(vt_ref[0, 0, :, start:start + size], p)
        if 0 < j < len(blocks) - 1:
            new_shift = jnp.maximum(shift, block_max)
            acc = acc * jnp.exp2((shift - new_shift) * c)
            shift = new_shift
    finish(acc)

    @pl.when(jnp.max(excess) > FLASH_MAX_EXCESS)
    def _():
        carry = (jnp.full((1, q.shape[0]), -jnp.inf, F32), jnp.zeros((VT_ROWS, q.shape[0]), F32))
        for k_ref, vt_ref in sources:
            size = block_size(k_ref)

            def body(j, carry, k_ref=k_ref, vt_ref=vt_ref, size=size):
                m, acc = carry
                start = pl.multiple_of(j * size, size)
                st = jax.lax.dot_general(k_ref[0, pl.ds(start, size), :], q, nt, preferred_element_type=F32)
                m_next = jnp.maximum(m, jnp.max(st, axis=0, keepdims=True))
                p = jnp.exp2((st - m_next) * c).astype(BF16)
                acc = acc * jnp.exp2((m - m_next) * c) + _dot(vt_ref[0, 0, :, pl.ds(start, size)], p)
                return m_next, acc

            carry = jax.lax.fori_loop(0, k_ref.shape[1] // size, body, carry)
        finish(carry[1])


def _flash_call(q, kv_sources, tq=FLASH_TQ, tk=FLASH_TK):
    B, L, _ = q.shape
    args, specs = [q], [pl.BlockSpec((1, tq, Q_PER_KV * HEAD_DIM), lambda b, kh, i: (b, i, kh))]
    for k, vt in kv_sources:
        Lk = k.shape[1]
        assert Lk % min(tk, Lk) == 0
        args += [k, vt]
        specs += [pl.BlockSpec((1, Lk, HEAD_DIM), lambda b, kh, i: (b, 0, kh)),
                  pl.BlockSpec((1, 1, VT_ROWS, Lk), lambda b, kh, i: (b, kh, 0, 0))]
    return pl.pallas_call(
        functools.partial(_flash_kernel, tq=tq, tk=tk),
        grid=(B, N_KV_HEADS, L // tq),
        in_specs=specs,
        out_specs=pl.BlockSpec((1, tq, Q_PER_KV * HEAD_DIM), lambda b, kh, i: (b, i, kh)),
        out_shape=jax.ShapeDtypeStruct(q.shape, BF16),
        compiler_params=_params("parallel", "parallel", "parallel"),
        name="flash_gqa",
    )(*args)


def _gmlp_kernel(h_ref, mod_ref, g_ref, win_ref, lng_ref, lnb_ref, ws_ref, bst_ref, o_ref, *, tm):
    sub = min(tm, GMLP_SUB_ROWS)
    starts = range(0, tm, sub)
    a = [_norm_mod(h_ref[0, r0:r0 + sub, :], g_ref[...], mod_ref[0, 0:1, :], mod_ref[0, 1:2, :]).astype(BF16)
         for r0 in starts]
    zv = [_dot(a_c, win_ref[:, GMLP_HALF:]) for a_c in a]
    zu = [_dot(a_c, win_ref[:, :GMLP_HALF]) for a_c in a]
    for c, r0 in enumerate(starts):
        v = jax.nn.gelu(zv[c], approximate=True)
        mu = jnp.mean(v, axis=-1, keepdims=True)
        vc = v - mu
        v = vc * jax.lax.rsqrt(jnp.mean(vc * vc, axis=-1, keepdims=True) + EPS) * lng_ref[...] + lnb_ref[...]
        v = v.astype(BF16)
        u = jax.nn.gelu(zu[c], approximate=True)
        for ch in range(sub // CHUNK):
            rows = slice(ch * CHUNK, (ch + 1) * CHUNK)
            out_rows = slice(r0 + ch * CHUNK, r0 + (ch + 1) * CHUNK)
            for grp in range(GMLP_GROUPS):
                cols = slice(grp * GMLP_GW, (grp + 1) * GMLP_GW)
                sv = _dot(ws_ref[grp], v[rows, cols]) + bst_ref[:, grp:grp + 1]
                o_ref[0, out_rows, cols] = (u[rows, cols] * sv).astype(o_ref.dtype)


def _gmlp_call(h, mod, g, w_in, ln_g, ln_b, ws, bs, tm=GMLP_TM):
    B, L, D = h.shape
    return pl.pallas_call(
        functools.partial(_gmlp_kernel, tm=tm),
        grid=(B, L // tm),
        in_specs=[
            pl.BlockSpec((1, tm, D), lambda b, i: (b, i, 0)),
            _mod_spec(mod),
            _resident((1, D)),
            _resident(w_in.shape),
            _resident((1, GMLP_HALF)),
            _resident((1, GMLP_HALF)),
            _resident(ws.shape),
            _resident((CHUNK, GMLP_GROUPS)),
        ],
        out_specs=pl.BlockSpec((1, tm, GMLP_HALF), lambda b, i: (b, i, 0)),
        out_shape=jax.ShapeDtypeStruct((B, L, GMLP_HALF), BF16),
        compiler_params=_params("parallel", "parallel"),
        name="gmlp_gate",
    )(h, mod, g.reshape(1, D), w_in, ln_g.reshape(1, GMLP_HALF), ln_b.reshape(1, GMLP_HALF), ws, bs.T)


def kernel(x, c, ctx, c_ctx, ada_w, ada_b, norm_g, mlp_w1, mlp_w2, pool_w, pool_scale, attn_w_qkv, attn_w_o,
           attn_q_g, attn_k_g, gm_w_in, gm_ln_g, gm_ln_b, gm_ws, gm_bs, gm_w_out, final_g):
    B, S, D = x.shape
    last_ctx_read = max([i for i in range(DEPTH) if i % N_MIXERS == 1], default=-1)

    pad = (-(B + 1)) % 8
    s_in = jnp.concatenate([c, c_ctx[None, :], jnp.zeros((pad, D), F32)], axis=0)
    mods = _ada_call(s_in, ada_w, ada_b)
    mod_lat = mods[:, :B].reshape(DEPTH, B, 6, D)
    mod_ctx = mods[:, B:B + 1].reshape(DEPTH, 1, 6, D)

    def layer_weight_stacks(i):
        kind, j = i % N_MIXERS, i // N_MIXERS
        mixer = {0: [(pool_w, j, 0)], 1: [(attn_w_qkv, j, N_HEADS + N_KV_HEADS), (attn_w_o, j, 0)],
                 2: [(gm_w_in, j, 0), (gm_ws, j, 0), (gm_w_out, j, 0)]}[kind]
        return mixer + [(mlp_w1, i, 0), (mlp_w2, i, 0)]

    h_lat, h_ctx = x, ctx
    weights = [_cast_call(*entry) for entry in layer_weight_stacks(0)]
    for i in range(DEPTH):
        kind, j = i % N_MIXERS, i // N_MIXERS
        ctx_out = i < last_ctx_read
        *mixer_w, w1, w2 = weights
        nxt = layer_weight_stacks(i + 1) if i + 1 < DEPTH else ()
        fin = final_g if i == DEPTH - 1 else None
        if kind == 0:
            pw, = mixer_w
            if ctx_out:
                h_ctx, _ = _pool_mlp_call(h_ctx, mod_ctx[i], norm_g[i, 0], pw, pool_scale[j], norm_g[i, 1], w1, w2)
            h_lat, weights = _pool_mlp_call(h_lat, mod_lat[i], norm_g[i, 0], pw, pool_scale[j], norm_g[i, 1], w1, w2,
                                            final_g=fin, casts=nxt)
        elif kind == 1:
            wqkv, wo = mixer_w
            q_g, k_g = (g_[j].reshape(2, 2, HEAD_DIM // 4).transpose(1, 0, 2).reshape(HEAD_DIM)
                        for g_ in (attn_q_g, attn_k_g))
            q, k_l, vt_l = _qkv_call(h_lat, mod_lat[i], norm_g[i, 0], wqkv, q_g, k_g, _rope_tables(S), True)
            k_c, vt_c = _qkv_call(h_ctx, mod_ctx[i], norm_g[i, 0], wqkv, q_g, k_g, None, False)
            o = _flash_call(q, [(k_c, vt_c), (k_l, vt_l)])
            if ctx_out:
                raise NotImplementedError("context stream output of an attention layer")
            h_lat, weights = _mlp_call(h_lat, mod_lat[i], norm_g[i, 1], w1, w2, proj=(o, wo), final_g=fin, casts=nxt)
        else:
            w_in, ws, w_out = mixer_w
            t = _gmlp_call(h_lat, mod_lat[i], norm_g[i, 0], w_in, gm_ln_g[j], gm_ln_b[j], ws, gm_bs[j])
            if ctx_out:
                raise NotImplementedError("context stream output of a gMLP layer")
            h_lat, weights = _mlp_call(h_lat, mod_lat[i], norm_g[i, 1], w1, w2, proj=(t, w_out), final_g=fin,
                                       casts=nxt)
    return h_lat
```

```python
import functools

import jax
import jax.numpy as jnp
import numpy as np
from jax.experimental import pallas as pl
from jax.experimental.pallas import tpu as pltpu

D_MODEL = 1024
DEPTH = 4
N_MIXERS = 3
GRID_W = 64
EPS = 1e-6
POOL_WINDOWS = (2, 4, 8, 16)
POOL_GW = D_MODEL // len(POOL_WINDOWS)
POOL_HALO = 16
HEAD_DIM = 128
N_HEADS = D_MODEL // HEAD_DIM
N_KV_HEADS = N_HEADS // 2
Q_PER_KV = N_HEADS // N_KV_HEADS
Q_WIDTH = N_HEADS * HEAD_DIM
KV_WIDTH = N_KV_HEADS * HEAD_DIM
VT_ROWS = HEAD_DIM + 16
ROPE_BASE = 10000.0
CHUNK = 128
GMLP_HALF = 2 * D_MODEL
GMLP_GROUPS = 8
GMLP_GW = GMLP_HALF // GMLP_GROUPS
D_FF = 4 * D_MODEL
LOG2E = 1.4426950408889634
FLASH_MAX_EXCESS = 64.0

MLP_TM = 512
QKV_TM = 1024
QKV_SUB_ROWS = 256
GMLP_TM = 1024
GMLP_SUB_ROWS = 256
FLASH_TQ = 512
FLASH_TK = 1024
ADA_TN = 1536
CAST_BLOCK_ELEMS = 1024 * 1024

VMEM_LIMIT_BYTES = 56 * 1024 * 1024
BF16 = jnp.bfloat16
F32 = jnp.float32


def _params(*semantics):
    return pltpu.CompilerParams(dimension_semantics=semantics, vmem_limit_bytes=VMEM_LIMIT_BYTES)


def _resident(shape):
    zeros = (0,) * len(shape)
    return pl.BlockSpec(shape, lambda *_: zeros, pipeline_mode=pl.Buffered(1))


def _mod_spec(mod):
    if mod.shape[0] == 1:
        return pl.BlockSpec((1, 6, D_MODEL), lambda b, i: (0, 0, 0))
    return pl.BlockSpec((1, 6, D_MODEL), lambda b, i: (b, 0, 0))


def _rms(x, g):
    return x * jax.lax.rsqrt(jnp.mean(x * x, axis=-1, keepdims=True) + EPS) * g


def _norm_mod(x, g, shift, scale):
    return _rms(x, g) * (1.0 + scale) + shift


def _dot(a, b):
    return jnp.dot(a, b, preferred_element_type=F32)


def _cast_block(w, permute_heads):
    if permute_heads:
        heads = [_permute_head_lanes(w[:, hd * HEAD_DIM:(hd + 1) * HEAD_DIM]) for hd in range(permute_heads)]
        w = jnp.concatenate(heads + [w[:, permute_heads * HEAD_DIM:]], axis=-1)
    return w.astype(BF16)


def _cast_kernel(w_ref, o_ref, *, permute_heads):
    o_ref[...] = _cast_block(w_ref[0], permute_heads)


def _cast_call(w_stack, layer, permute_heads=0):
    shape = w_stack.shape[1:]
    C = shape[-1]
    R = int(np.prod(shape[:-1]))
    br = min(R, CAST_BLOCK_ELEMS // C)
    out = pl.pallas_call(
        functools.partial(_cast_kernel, permute_heads=permute_heads),
        grid=(R // br,),
        in_specs=[pl.BlockSpec((1, br, C), lambda r: (layer, r, 0))],
        out_specs=pl.BlockSpec((br, C), lambda r: (r, 0)),
        out_shape=jax.ShapeDtypeStruct((R, C), BF16),
        compiler_params=_params("parallel"),
        name="cast_bf16",
    )(w_stack.reshape(w_stack.shape[0], R, C))
    return out.reshape(shape)


def _side_casts(casts, n_steps, step_of):
    args, in_specs, out_shapes, out_specs, shapes = [], [], [], [], []
    for w_stack, layer, _ in casts:
        shape = w_stack.shape[1:]
        C = shape[-1]
        R = int(np.prod(shape[:-1]))
        assert R % (n_steps * 16) == 0
        br = R // n_steps
        args.append(w_stack.reshape(w_stack.shape[0], R, C))
        in_specs.append(pl.BlockSpec((1, br, C), lambda *idx, layer=layer: (layer, step_of(*idx), 0)))
        out_shapes.append(jax.ShapeDtypeStruct((R, C), BF16))
        out_specs.append(pl.BlockSpec((br, C), lambda *idx: (step_of(*idx), 0)))
        shapes.append(shape)
    return args, in_specs, out_shapes, out_specs, shapes


def _run_side_casts(in_refs, out_refs, permute_heads):
    for w_ref, o_ref, heads in zip(in_refs, out_refs, permute_heads):
        o_ref[...] = _cast_block(w_ref[0], heads)


def _ada_kernel(*refs, cast_heads):
    s_ref, w_ref, b_ref = refs[:3]
    n_cast = len(cast_heads)
    cast_in, o_ref, cast_out = refs[3:3 + n_cast], refs[3 + n_cast], refs[4 + n_cast:]
    s = s_ref[...]
    s = s * jax.nn.sigmoid(s)
    o_ref[0] = _dot(s.astype(BF16), w_ref[0].astype(BF16)) + b_ref[0]
    _run_side_casts(cast_in, cast_out, cast_heads)


def _ada_call(s_in, ada_w, ada_b, casts=()):
    rows = s_in.shape[0]
    tn = ADA_TN
    n_col = 6 * D_MODEL // tn
    c_args, c_in, c_shapes, c_out, c_orig = _side_casts(casts, DEPTH * n_col, lambda l, j: l * n_col + j)
    outs = pl.pallas_call(
        functools.partial(_ada_kernel, cast_heads=tuple(c[2] for c in casts)),
        grid=(DEPTH, n_col),
        in_specs=[
            pl.BlockSpec((rows, D_MODEL), lambda l, j: (0, 0)),
            pl.BlockSpec((1, D_MODEL, tn), lambda l, j: (l, 0, j)),
            pl.BlockSpec((1, 1, tn), lambda l, j: (l, 0, j)),
        ] + c_in,
        out_specs=[pl.BlockSpec((1, rows, tn), lambda l, j: (l, 0, j))] + c_out,
        out_shape=[jax.ShapeDtypeStruct((DEPTH, rows, 6 * D_MODEL), F32)] + c_shapes,
        compiler_params=_params("arbitrary", "arbitrary"),
        name="ada_mod",
    )(s_in, ada_w, ada_b.reshape(DEPTH, 1, 6 * D_MODEL), *c_args)
    return outs[0], [w.reshape(s) for w, s in zip(outs[1:], c_orig)]


def _pool_window(a_ext, a_main, pos, seq_len, grp):
    n_ext, n_main = a_ext.shape[0], a_main.shape[0]
    win = POOL_WINDOWS[grp]
    cols = slice(grp * POOL_GW, (grp + 1) * POOL_GW)
    lo_off, hi_off = -(win // 2), win - win // 2
    s = a_ext[:, cols]
    if hi_off > 1:
        s = pltpu.roll(s, n_ext - (hi_off - 1), axis=0)
    k = 1
    while k < win:
        s = s + pltpu.roll(s, k, axis=0)
        k *= 2
    s = s[POOL_HALO:POOL_HALO + n_main]
    cnt = (jnp.minimum(pos + hi_off, seq_len) - jnp.maximum(pos + lo_off, 0)).astype(F32)
    return (s / cnt - a_main[:, cols]).astype(BF16)


def _mlp_rows(x, mod_ref, g_ref, w1_ref, w2_ref, fg_ref):
    m = _norm_mod(x, g_ref[...], mod_ref[0, 3:4, :], mod_ref[0, 4:5, :]).astype(BF16)
    a = jnp.maximum(_dot(m, w1_ref[...]), 0.0)
    a = (a * a).astype(BF16)
    y = x + mod_ref[0, 5:6, :] * _dot(a, w2_ref[...])
    return y if fg_ref is None else _rms(y, fg_ref[...])


def _mlp_kernel(*refs, has_proj, has_final, cast_heads):
    refs = list(refs)
    h_ref, mod_ref, g_ref = refs[:3]
    refs = refs[3:]
    if has_proj:
        t_ref, wp_ref = refs[:2]
        refs = refs[2:]
    w1_ref, w2_ref = refs[:2]
    refs = refs[2:]
    fg_ref = None
    if has_final:
        fg_ref = refs[0]
        refs = refs[1:]
    n_cast = len(cast_heads)
    cast_in, o_ref, cast_out = refs[:n_cast], refs[n_cast], refs[n_cast + 1:]

    x = h_ref[0]
    if has_proj:
        x = x + mod_ref[0, 2:3, :] * _dot(t_ref[0], wp_ref[...])
    o_ref[0] = _mlp_rows(x, mod_ref, g_ref, w1_ref, w2_ref, fg_ref)
    _run_side_casts(cast_in, cast_out, cast_heads)


def _mlp_call(h, mod, g, w1, w2, proj=None, final_g=None, casts=(), tm=MLP_TM):
    B, L, D = h.shape
    tm = min(tm, L)
    n = L // tm
    tok = lambda width: pl.BlockSpec((1, tm, width), lambda b, i: (b, i, 0))
    args = [h, mod, g.reshape(1, D)]
    specs = [tok(D), _mod_spec(mod), _resident((1, D))]
    if proj is not None:
        t, wp = proj
        args += [t, wp]
        specs += [tok(t.shape[-1]), _resident(wp.shape)]
    args += [w1, w2]
    specs += [_resident(w1.shape), _resident(w2.shape)]
    if final_g is not None:
        args.append(final_g.reshape(1, D))
        specs.append(_resident((1, D)))
    c_args, c_in, c_shapes, c_out, c_orig = _side_casts(casts, B * n, lambda b, i: b * n + i)
    outs = pl.pallas_call(
        functools.partial(_mlp_kernel, has_proj=proj is not None, has_final=final_g is not None,
                          cast_heads=tuple(c[2] for c in casts)),
        grid=(B, n),
        in_specs=specs + c_in,
        out_specs=[tok(D)] + c_out,
        out_shape=[jax.ShapeDtypeStruct(h.shape, F32)] + c_shapes,
        compiler_params=_params("parallel", "parallel"),
        name="mlp",
    )(*args, *c_args)
    return outs[0], [w.reshape(s) for w, s in zip(outs[1:], c_orig)]


def _pool_mlp_kernel(*refs, has_final, seq_len, tiles_per_seq, n_tiles, cast_heads):
    refs = list(refs)
    h_ref, hp_ref, hn_ref, moda_ref, g0_ref, pw_ref, psc_ref, modb_ref, g_ref, w1_ref, w2_ref = refs[:11]
    refs = refs[11:]
    fg_ref = None
    if has_final:
        fg_ref = refs[0]
        refs = refs[1:]
    n_cast = len(cast_heads)
    cast_in, o_ref, cast_out, x1_ref = refs[:n_cast], refs[n_cast], refs[n_cast + 1:-1], refs[-1]
    s = pl.program_id(0)
    tm = h_ref.shape[1]

    i = jnp.minimum(s, n_tiles - 1) % tiles_per_seq

    def stages(write_slot, read_slot, with_mlp=True):
        if with_mlp:
            xb = x1_ref[read_slot]
            m = _norm_mod(xb, g_ref[...], modb_ref[0, 3:4, :], modb_ref[0, 4:5, :]).astype(BF16)
        norm = functools.partial(_norm_mod, g=g0_ref[...], shift=moda_ref[0, 0:1, :], scale=moda_ref[0, 1:2, :])
        x = h_ref[0]
        a_main = norm(x)
        top = jnp.where(i > 0, norm(hp_ref[0]), 0.0)
        bottom = jnp.where(i < tiles_per_seq - 1, norm(hn_ref[0]), 0.0)
        pos = i * tm + jax.lax.broadcasted_iota(jnp.int32, (tm, 1), 0)
        a_ext = jnp.concatenate([top, a_main, bottom], axis=0)

        n_grp = len(POOL_WINDOWS)
        fc = D_FF // n_grp
        acc, ys = None, []
        for grp in range(n_grp):
            p = _pool_window(a_ext, a_main, pos, seq_len, grp)
            if with_mlp:
                a = jnp.maximum(_dot(m, w1_ref[:, grp * fc:(grp + 1) * fc]), 0.0)
                d = _dot((a * a).astype(BF16), w2_ref[grp * fc:(grp + 1) * fc, :])
                acc = d if acc is None else acc + d
            ys.append(_dot(p, pw_ref[grp]))
        if with_mlp:
            y = xb + modb_ref[0, 5:6, :] * acc
            o_ref[0] = y if fg_ref is None else _rms(y, fg_ref[...])
        x1_ref[write_slot] = x + moda_ref[0, 2:3, :] * (jnp.concatenate(ys, axis=-1) * psc_ref[...])
        _run_side_casts(cast_in, cast_out, cast_heads)

    pl.when(s == 0)(functools.partial(stages, 0, 1, with_mlp=False))
    pl.when((s > 0) & (s % 2 == 0))(functools.partial(stages, 0, 1))
    pl.when(s % 2 == 1)(functools.partial(stages, 1, 0))


def _pool_mlp_call(h, mod, g0, pool_w, pool_scale, g, w1, w2, final_g=None, casts=(), tm=MLP_TM):
    B, L, D = h.shape
    tm = min(tm, L)
    tps = L // tm
    n_tiles = B * tps
    r = tm // POOL_HALO
    last = L // POOL_HALO - 1
    shared_mod = mod.shape[0] == 1

    def tile_a(s):
        t = jnp.minimum(s, n_tiles - 1)
        return t // tps, t % tps

    def tile_b(s):
        t = jnp.maximum(s - 1, 0)
        return t // tps, t % tps

    def mod_spec(tile):
        return pl.BlockSpec((1, 6, D), lambda s: (0 if shared_mod else tile(s)[0], 0, 0))

    args = [h, h, h, mod, g0.reshape(1, D), pool_w, pool_scale.reshape(1, D), mod, g.reshape(1, D), w1, w2]
    specs = [
        pl.BlockSpec((1, tm, D), lambda s: (*tile_a(s), 0)),
        pl.BlockSpec((1, POOL_HALO, D), lambda s: (tile_a(s)[0], jnp.maximum(tile_a(s)[1] * r - 1, 0), 0)),
        pl.BlockSpec((1, POOL_HALO, D), lambda s: (tile_a(s)[0], jnp.minimum((tile_a(s)[1] + 1) * r, last), 0)),
        mod_spec(tile_a), _resident((1, D)), _resident(pool_w.shape), _resident((1, D)),
        mod_spec(tile_b), _resident((1, D)), _resident(w1.shape), _resident(w2.shape),
    ]
    if final_g is not None:
        args.append(final_g.reshape(1, D))
        specs.append(_resident((1, D)))
    c_args, c_in, c_shapes, c_out, c_orig = _side_casts(casts, n_tiles, lambda s: jnp.minimum(s, n_tiles - 1))
    outs = pl.pallas_call(
        functools.partial(_pool_mlp_kernel, has_final=final_g is not None, seq_len=L, tiles_per_seq=tps,
                          n_tiles=n_tiles, cast_heads=tuple(c[2] for c in casts)),
        grid=(n_tiles + 1,),
        in_specs=specs + c_in,
        out_specs=[pl.BlockSpec((1, tm, D), lambda s: (*tile_b(s), 0))] + c_out,
        out_shape=[jax.ShapeDtypeStruct(h.shape, F32)] + c_shapes,
        scratch_shapes=[pltpu.VMEM((2, tm, D), F32)],
        compiler_params=_params("arbitrary"),
        name="pool_mlp",
    )(*args, *c_args)
    return outs[0], [w.reshape(s) for w, s in zip(outs[1:], c_orig)]


def _permute_head_lanes(t):
    q = HEAD_DIM // 4
    lane = jax.lax.broadcasted_iota(jnp.int32, t.shape, 1)
    from_right = pltpu.roll(t, HEAD_DIM - q, axis=1)
    from_left = pltpu.roll(t, q, axis=1)
    return jnp.where(lane < q, t, jnp.where(lane < 2 * q, from_right, jnp.where(lane < 3 * q, from_left, t)))


def _rope_tables(L):
    rows_n = L // GRID_W
    q = HEAD_DIM // 4
    half = HEAD_DIM // 2
    inv = (np.float32(ROPE_BASE) ** (-np.arange(0, half, 2, dtype=np.float32) / np.float32(half))).astype(np.float32)
    ang_r = (np.arange(rows_n, dtype=np.float32)[:, None] * inv[None, :]).astype(np.float64)
    ang_c = (np.arange(GRID_W, dtype=np.float32)[:, None] * inv[None, :]).astype(np.float64)

    def table(fn, signs):
        by_row = np.zeros((rows_n, HEAD_DIM), np.float32)
        by_col = np.zeros((GRID_W, HEAD_DIM), np.float32)
        for quarter, sign in enumerate(signs):
            tab, ang = (by_row, ang_r) if quarter % 2 == 0 else (by_col, ang_c)
            tab[:, quarter * q:(quarter + 1) * q] = sign * fn(ang)
        return (jnp.broadcast_to(jnp.asarray(by_row)[:, None, :], (rows_n, GRID_W, HEAD_DIM))
                + jnp.asarray(by_col)[None, :, :]).reshape(L, HEAD_DIM)

    return table(np.cos, (1.0, 1.0, 1.0, 1.0)), table(np.sin, (-1.0, -1.0, 1.0, 1.0))


def _qkv_kernel(*refs, rope, want_q):
    refs = list(refs)
    h_ref, mod_ref, g_ref, w_ref, qg_ref, kg_ref = refs[:6]
    refs = refs[6:]
    if rope:
        cos_ref, sin_ref = refs[:2]
        refs = refs[2:]
    if want_q:
        q_ref = refs[0]
        refs = refs[1:]
    k_ref, vt_ref = refs

    tm = h_ref.shape[1]
    sub = min(tm, QKV_SUB_ROWS)
    chains = []
    for r0 in range(0, tm, sub):
        rows = slice(r0, r0 + sub)
        a = _norm_mod(h_ref[0, rows, :], g_ref[...], mod_ref[0, 0:1, :], mod_ref[0, 1:2, :]).astype(BF16)
        yq = _dot(a, w_ref[:, :Q_WIDTH]) if want_q else None
        yk = _dot(a, w_ref[:, Q_WIDTH:Q_WIDTH + KV_WIDTH])
        yv = _dot(a, w_ref[:, Q_WIDTH + KV_WIDTH:])
        chains.append((rows, yq, yk, yv))

    for rows, yq, yk, yv in chains:
        def heads(y, gain, out_ref):
            for hd in range(y.shape[1] // HEAD_DIM):
                t = _rms(y[:, hd * HEAD_DIM:(hd + 1) * HEAD_DIM], gain)
                if rope:
                    t = t * cos_ref[rows, :] + pltpu.roll(t, HEAD_DIM // 2, axis=1) * sin_ref[rows, :]
                out_ref[0, rows, hd * HEAD_DIM:(hd + 1) * HEAD_DIM] = t.astype(out_ref.dtype)

        if want_q:
            heads(yq, qg_ref[...], q_ref)
        heads(yk, kg_ref[...], k_ref)
        for hd in range(N_KV_HEADS):
            vt_ref[0, hd, 0:HEAD_DIM, rows] = yv[:, hd * HEAD_DIM:(hd + 1) * HEAD_DIM].T.astype(vt_ref.dtype)
            vt_ref[0, hd, HEAD_DIM:, rows] = jnp.ones((VT_ROWS - HEAD_DIM, sub), vt_ref.dtype)


def _qkv_call(h, mod, g, w, q_g, k_g, rope_tabs, want_q, tm=QKV_TM):
    B, L, D = h.shape
    tm = min(tm, L)
    tok = lambda width: pl.BlockSpec((1, tm, width), lambda b, i: (b, i, 0))
    args = [h, mod, g.reshape(1, D), w, q_g.reshape(1, HEAD_DIM), k_g.reshape(1, HEAD_DIM)]
    specs = [tok(D), _mod_spec(mod), _resident((1, D)), _resident(w.shape),
             _resident((1, HEAD_DIM)), _resident((1, HEAD_DIM))]
    if rope_tabs is not None:
        args += list(rope_tabs)
        specs += [pl.BlockSpec((tm, HEAD_DIM), lambda b, i: (i, 0))] * 2
    out_shape, out_specs = [], []
    if want_q:
        out_shape.append(jax.ShapeDtypeStruct((B, L, Q_WIDTH), BF16))
        out_specs.append(tok(Q_WIDTH))
    out_shape += [jax.ShapeDtypeStruct((B, L, KV_WIDTH), BF16),
                  jax.ShapeDtypeStruct((B, N_KV_HEADS, VT_ROWS, L), BF16)]
    out_specs += [tok(KV_WIDTH), pl.BlockSpec((1, N_KV_HEADS, VT_ROWS, tm), lambda b, i: (b, 0, 0, i))]
    return pl.pallas_call(
        functools.partial(_qkv_kernel, rope=rope_tabs is not None, want_q=want_q),
        grid=(B, L // tm),
        in_specs=specs,
        out_specs=out_specs,
        out_shape=out_shape,
        compiler_params=_params("parallel", "parallel"),
        name="qkv_proj",
    )(*args)


def _flash_kernel(*refs, tq, tk):
    q_ref, o_ref = refs[0], refs[-1]
    sources = [(refs[1 + 2 * s], refs[2 + 2 * s]) for s in range((len(refs) - 2) // 2)]
    q = jnp.concatenate([q_ref[0, :, hd * HEAD_DIM:(hd + 1) * HEAD_DIM] for hd in range(Q_PER_KV)], axis=0)
    c = (HEAD_DIM ** -0.5) * LOG2E
    nt = (((1,), (1,)), ((), ()))

    def block_size(k_ref):
        return min(tk, k_ref.shape[1])

    def finish(acc):
        o = (acc[:HEAD_DIM] / acc[HEAD_DIM:HEAD_DIM + 1]).T
        for hd in range(Q_PER_KV):
            o_ref[0, :, hd * HEAD_DIM:(hd + 1) * HEAD_DIM] = o[hd * tq:(hd + 1) * tq].astype(o_ref.dtype)

    blocks = [(k_ref, vt_ref, start, block_size(k_ref)) for k_ref, vt_ref in sources
              for start in range(0, k_ref.shape[1], block_size(k_ref))]
    acc = jnp.zeros((VT_ROWS, q.shape[0]), F32)
    excess = jnp.zeros((1, q.shape[0]), F32)
    for j, (k_ref, vt_ref, start, size) in enumerate(blocks):
        st = jax.lax.dot_general(k_ref[0, start:start + size, :], q, nt, preferred_element_type=F32)
        block_max = jnp.max(st, axis=0, keepdims=True)
        if j == 0:
            shift = block_max
        else:
            excess = jnp.maximum(excess, (block_max - shift) * c)
        p = jnp.exp2((st - shift) * c).astype(BF16)
        acc = acc + _dot(vt_ref[0, 0, :, start:start + size], p)
        if 0 < j < len(blocks) - 1:
            new_shift = jnp.maximum(shift, block_max)
            acc = acc * jnp.exp2((shift - new_shift) * c)
            shift = new_shift
    finish(acc)

    @pl.when(jnp.max(excess) > FLASH_MAX_EXCESS)
    def _():
        carry = (jnp.full((1, q.shape[0]), -jnp.inf, F32), jnp.zeros((VT_ROWS, q.shape[0]), F32))
        for k_ref, vt_ref in sources:
            size = block_size(k_ref)

            def body(j, carry, k_ref=k_ref, vt_ref=vt_ref, size=size):
                m, acc = carry
                start = pl.multiple_of(j * size, size)
                st = jax.lax.dot_general(k_ref[0, pl.ds(start, size), :], q, nt, preferred_element_type=F32)
                m_next = jnp.maximum(m, jnp.max(st, axis=0, keepdims=True))
                p = jnp.exp2((st - m_next) * c).astype(BF16)
                acc = acc * jnp.exp2((m - m_next) * c) + _dot(vt_ref[0, 0, :, pl.ds(start, size)], p)
                return m_next, acc

            carry = jax.lax.fori_loop(0, k_ref.shape[1] // size, body, carry)
        finish(carry[1])


def _flash_call(q, kv_sources, tq=FLASH_TQ, tk=FLASH_TK):
    B, L, _ = q.shape
    args, specs = [q], [pl.BlockSpec((1, tq, Q_PER_KV * HEAD_DIM), lambda b, kh, i: (b, i, kh))]
    for k, vt in kv_sources:
        Lk = k.shape[1]
        assert Lk % min(tk, Lk) == 0
        args += [k, vt]
        specs += [pl.BlockSpec((1, Lk, HEAD_DIM), lambda b, kh, i: (b, 0, kh)),
                  pl.BlockSpec((1, 1, VT_ROWS, Lk), lambda b, kh, i: (b, kh, 0, 0))]
    return pl.pallas_call(
        functools.partial(_flash_kernel, tq=tq, tk=tk),
        grid=(B, N_KV_HEADS, L // tq),
        in_specs=specs,
        out_specs=pl.BlockSpec((1, tq, Q_PER_KV * HEAD_DIM), lambda b, kh, i: (b, i, kh)),
        out_shape=jax.ShapeDtypeStruct(q.shape, BF16),
        compiler_params=_params("parallel", "parallel", "parallel"),
        name="flash_gqa",
    )(*args)


def _gmlp_kernel(h_ref, mod_ref, g_ref, win_ref, lng_ref, lnb_ref, ws_ref, bst_ref, o_ref, *, tm):
    sub = min(tm, GMLP_SUB_ROWS)
    starts = range(0, tm, sub)
    a = [_norm_mod(h_ref[0, r0:r0 + sub, :], g_ref[...], mod_ref[0, 0:1, :], mod_ref[0, 1:2, :]).astype(BF16)
         for r0 in starts]
    zv = [_dot(a_c, win_ref[:, GMLP_HALF:]) for a_c in a]
    zu = [_dot(a_c, win_ref[:, :GMLP_HALF]) for a_c in a]
    for c, r0 in enumerate(starts):
        v = jax.nn.gelu(zv[c], approximate=True)
        mu = jnp.mean(v, axis=-1, keepdims=True)
        vc = v - mu
        v = vc * jax.lax.rsqrt(jnp.mean(vc * vc, axis=-1, keepdims=True) + EPS) * lng_ref[...] + lnb_ref[...]
        v = v.astype(BF16)
        u = jax.nn.gelu(zu[c], approximate=True)
        for ch in range(sub // CHUNK):
            rows = slice(ch * CHUNK, (ch + 1) * CHUNK)
            out_rows = slice(r0 + ch * CHUNK, r0 + (ch + 1) * CHUNK)
            for grp in range(GMLP_GROUPS):
                cols = slice(grp * GMLP_GW, (grp + 1) * GMLP_GW)
                sv = _dot(ws_ref[grp], v[rows, cols]) + bst_ref[:, grp:grp + 1]
                o_ref[0, out_rows, cols] = (u[rows, cols] * sv).astype(o_ref.dtype)


def _gmlp_call(h, mod, g, w_in, ln_g, ln_b, ws, bs, tm=GMLP_TM):
    B, L, D = h.shape
    return pl.pallas_call(
        functools.partial(_gmlp_kernel, tm=tm),
        grid=(B, L // tm),
        in_specs=[
            pl.BlockSpec((1, tm, D), lambda b, i: (b, i, 0)),
            _mod_spec(mod),
            _resident((1, D)),
            _resident(w_in.shape),
            _resident((1, GMLP_HALF)),
            _resident((1, GMLP_HALF)),
            _resident(ws.shape),
            _resident((CHUNK, GMLP_GROUPS)),
        ],
        out_specs=pl.BlockSpec((1, tm, GMLP_HALF), lambda b, i: (b, i, 0)),
        out_shape=jax.ShapeDtypeStruct((B, L, GMLP_HALF), BF16),
        compiler_params=_params("parallel", "parallel"),
        name="gmlp_gate",
    )(h, mod, g.reshape(1, D), w_in, ln_g.reshape(1, GMLP_HALF), ln_b.reshape(1, GMLP_HALF), ws, bs.T)


def kernel(x, c, ctx, c_ctx, ada_w, ada_b, norm_g, mlp_w1, mlp_w2, pool_w, pool_scale, attn_w_qkv, attn_w_o,
           attn_q_g, attn_k_g, gm_w_in, gm_ln_g, gm_ln_b, gm_ws, gm_bs, gm_w_out, final_g):
    B, S, D = x.shape
    last_ctx_read = max([i for i in range(DEPTH) if i % N_MIXERS == 1], default=-1)

    pad = (-(B + 1)) % 8
    s_in = jnp.concatenate([c, c_ctx[None, :], jnp.zeros((pad, D), F32)], axis=0)
    def layer_weight_stacks(i):
        kind, j = i % N_MIXERS, i // N_MIXERS
        mixer = {0: [(pool_w, j, 0)], 1: [(attn_w_qkv, j, N_HEADS + N_KV_HEADS), (attn_w_o, j, 0)],
                 2: [(gm_w_in, j, 0), (gm_ws, j, 0), (gm_w_out, j, 0)]}[kind]
        return mixer + [(mlp_w1, i, 0), (mlp_w2, i, 0)]

    mods, weights = _ada_call(s_in, ada_w, ada_b, casts=layer_weight_stacks(0))
    mod_lat = mods[:, :B].reshape(DEPTH, B, 6, D)
    mod_ctx = mods[:, B:B + 1].reshape(DEPTH, 1, 6, D)

    h_lat, h_ctx = x, ctx
    for i in range(DEPTH):
        kind, j = i % N_MIXERS, i // N_MIXERS
        ctx_out = i < last_ctx_read
        *mixer_w, w1, w2 = weights
        nxt = layer_weight_stacks(i + 1) if i + 1 < DEPTH else ()
        fin = final_g if i == DEPTH - 1 else None
        if kind == 0:
            pw, = mixer_w
            if ctx_out:
                h_ctx, _ = _pool_mlp_call(h_ctx, mod_ctx[i], norm_g[i, 0], pw, pool_scale[j], norm_g[i, 1], w1, w2)
            h_lat, weights = _pool_mlp_call(h_lat, mod_lat[i], norm_g[i, 0], pw, pool_scale[j], norm_g[i, 1], w1, w2,
                                            final_g=fin, casts=nxt)
        elif kind == 1:
            wqkv, wo = mixer_w
            q_g, k_g = (g_[j].reshape(2, 2, HEAD_DIM // 4).transpose(1, 0, 2).reshape(HEAD_DIM)
                        for g_ in (attn_q_g, attn_k_g))
            q, k_l, vt_l = _qkv_call(h_lat, mod_lat[i], norm_g[i, 0], wqkv, q_g, k_g, _rope_tables(S), True)
            k_c, vt_c = _qkv_call(h_ctx, mod_ctx[i], norm_g[i, 0], wqkv, q_g, k_g, None, False)
            o = _flash_call(q, [(k_c, vt_c), (k_l, vt_l)])
            if ctx_out:
                raise NotImplementedError("context stream output of an attention layer")
            h_lat, weights = _mlp_call(h_lat, mod_lat[i], norm_g[i, 1], w1, w2, proj=(o, wo), final_g=fin, casts=nxt)
        else:
            w_in, ws, w_out = mixer_w
            t = _gmlp_call(h_lat, mod_lat[i], norm_g[i, 0], w_in, gm_ln_g[j], gm_ln_b[j], ws, gm_bs[j])
            if ctx_out:
                raise NotImplementedError("context stream output of a gMLP layer")
            h_lat, weights = _mlp_call(h_lat, mod_lat[i], norm_g[i, 1], w1, w2, proj=(t, w_out), final_g=fin,
                                       casts=nxt)
    return h_lat
```
